```python
import jax, jax.numpy as jnp
from jax import lax
import numpy as np

D_MODEL = 4096
BATCH = 8
SEQ = 2048
DEPTH = 2

HEAD_DIM = 128
D_MIX = D_MODEL
N_MIXERS = 4
GROUP_WIDTH = D_MIX // N_MIXERS
N_HEADS = GROUP_WIDTH // HEAD_DIM

MOBA_BLOCK = 256
MOBA_TOPK = 3
MOBA_Q_CHUNK = 16

SGU_CHUNK = 128

Q_LORA_RANK = 768
KV_LORA_RANK = 256
QK_NOPE_DIM = 128
QK_ROPE_DIM = 64
V_HEAD_DIM = 128
ROPE_THETA = 10000.0

Q_BLOCK = 128

D_FF = 11008
CONV_WIDTH = 3

NORM_EPS = 1e-6
NEG_INF = -1e30

SPLIT_SIZES = (GROUP_WIDTH, GROUP_WIDTH, GROUP_WIDTH,
               GROUP_WIDTH, GROUP_WIDTH,
               Q_LORA_RANK, KV_LORA_RANK, QK_ROPE_DIM,
               GROUP_WIDTH, GROUP_WIDTH, GROUP_WIDTH, N_HEADS)
IN_COLS = sum(SPLIT_SIZES)

kernel_name = "hymba_style_moba_gmlp_mla_fox_convffn"


def split_points():
    return [int(s) for s in np.cumsum(SPLIT_SIZES)[:-1]]


def rmsnorm(x, g):
    xf = x.astype(jnp.float32)
    y = xf * lax.rsqrt(jnp.mean(xf * xf, axis=-1, keepdims=True) + NORM_EPS)
    return (y * g.astype(jnp.float32)).astype(x.dtype)


def layernorm(x, g, b):
    xf = x.astype(jnp.float32)
    xc = xf - jnp.mean(xf, axis=-1, keepdims=True)
    var = jnp.mean(xc * xc, axis=-1, keepdims=True)
    y = xc * lax.rsqrt(var + NORM_EPS) * g.astype(jnp.float32) + b.astype(jnp.float32)
    return y.astype(x.dtype)


def alibi_slopes(n):
    return jnp.exp2(-8.0 * jnp.arange(1, n + 1, dtype=jnp.float32) / n)


def apply_rope(x, pos):
    half = x.shape[-1] // 2
    inv_freq = ROPE_THETA ** (-jnp.arange(half, dtype=jnp.float32) / half)
    ang = pos.astype(jnp.float32)[:, None] * inv_freq[None, :]
    cos, sin = jnp.cos(ang), jnp.sin(ang)
    xf = x.astype(jnp.float32)
    x1, x2 = xf[..., :half], xf[..., half:]
    return jnp.concatenate([x1 * cos - x2 * sin, x2 * cos + x1 * sin], axis=-1).astype(x.dtype)


def to_heads(t, n_heads):
    b, s, w = t.shape
    return t.reshape(b, s, n_heads, w // n_heads).transpose(0, 2, 1, 3)


def merge_heads(t):
    b, h, s, d = t.shape
    return t.transpose(0, 2, 1, 3).reshape(b, s, h * d)


def causal_block_attention(q, k, v, scale, cum_log_f=None):
    S = q.shape[2]
    outs = []
    for i in range(S // Q_BLOCK):
        lo, hi = i * Q_BLOCK, (i + 1) * Q_BLOCK
        s = jnp.einsum("bhqd,bhkd->bhqk", q[:, :, lo:hi], k[:, :, :hi]).astype(jnp.float32) * scale
        if cum_log_f is not None:
            s = s + cum_log_f[:, :, lo:hi, None] - cum_log_f[:, :, None, :hi]
        t_pos = jnp.arange(lo, hi)[:, None]
        s_pos = jnp.arange(hi)[None, :]
        s = jnp.where(s_pos <= t_pos, s, NEG_INF)
        p = jax.nn.softmax(s, axis=-1).astype(v.dtype)
        outs.append(jnp.einsum("bhqk,bhkd->bhqd", p, v[:, :, :hi]))
    return jnp.concatenate(outs, axis=2)


def moba_attention(q, k, v, slopes):
    B, H, S, Dh = q.shape
    n_blk = -(-S // MOBA_BLOCK)
    pad = n_blk * MOBA_BLOCK - S
    k_blk = jnp.pad(k, ((0, 0), (0, 0), (0, pad), (0, 0))).reshape(B, H, n_blk, MOBA_BLOCK, Dh)
    v_blk = jnp.pad(v, ((0, 0), (0, 0), (0, pad), (0, 0))).reshape(B, H, n_blk, MOBA_BLOCK, Dh)
    k_mean = jnp.mean(k_blk.astype(jnp.float32), axis=3)
    top_k = min(MOBA_TOPK, max(n_blk - 1, 1))
    scale = Dh ** -0.5
    b_ix = jnp.arange(B)[:, None, None, None]
    h_ix = jnp.arange(H)[None, :, None, None]
    offs = jnp.arange(MOBA_BLOCK)
    m = slopes[None, :, None, None]

    def one_chunk(start):
        blk = start // MOBA_BLOCK
        q_c = lax.dynamic_slice_in_dim(q, start, MOBA_Q_CHUNK, axis=2)
        t = start + jnp.arange(MOBA_Q_CHUNK)
        k_own = lax.dynamic_index_in_dim(k_blk, blk, axis=2, keepdims=False)
        v_own = lax.dynamic_index_in_dim(v_blk, blk, axis=2, keepdims=False)
        own_pos = blk * MOBA_BLOCK + offs
        s_own = (jnp.einsum("bhqd,bhkd->bhqk", q_c, k_own).astype(jnp.float32) * scale
                 - m * jnp.abs(t[:, None] - own_pos[None, :]).astype(jnp.float32))
        s_own = jnp.where(own_pos[None, :] <= t[:, None], s_own, NEG_INF)
        gate = jnp.einsum("bhqd,bhnd->bhqn", q_c.astype(jnp.float32), k_mean)
        gate = jnp.where(jnp.arange(n_blk) < blk, gate, -jnp.inf)
        _, idx = lax.top_k(gate, top_k)
        k_sel = k_blk[b_ix, h_ix, idx]
        v_sel = v_blk[b_ix, h_ix, idx]
        sel_pos = idx[..., None] * MOBA_BLOCK + offs
        s_sel = (jnp.einsum("bhqd,bhqnkd->bhqnk", q_c, k_sel).astype(jnp.float32) * scale
                 - m[..., None] * jnp.abs(t[:, None, None] - sel_pos).astype(jnp.float32))
        s_sel = jnp.where((idx < blk)[..., None], s_sel, NEG_INF)
        logits = jnp.concatenate(
            [s_own, s_sel.reshape(B, H, MOBA_Q_CHUNK, top_k * MOBA_BLOCK)], axis=-1)
        p = jax.nn.softmax(logits, axis=-1).astype(v.dtype)
        p_own = p[..., :MOBA_BLOCK]
        p_sel = p[..., MOBA_BLOCK:].reshape(B, H, MOBA_Q_CHUNK, top_k, MOBA_BLOCK)
        return (jnp.einsum("bhqk,bhkd->bhqd", p_own, v_own)
                + jnp.einsum("bhqnk,bhqnkd->bhqd", p_sel, v_sel))

    starts = jnp.arange(S // MOBA_Q_CHUNK, dtype=jnp.int32) * MOBA_Q_CHUNK
    outs = lax.map(one_chunk, starts)
    return outs.transpose(1, 2, 0, 3, 4).reshape(B, H, S, Dh)


def chunked_spatial_gating(u, v, ln_g, ln_b, w_s, b_s):
    B, S, _ = v.shape
    vg = layernorm(v.reshape(B, S, N_HEADS, HEAD_DIM), ln_g, ln_b)
    vc = vg.reshape(B, S // SGU_CHUNK, SGU_CHUNK, N_HEADS, HEAD_DIM)
    causal = jnp.tril(jnp.ones((SGU_CHUNK, SGU_CHUNK), dtype=bool))
    w = jnp.where(causal[None], w_s, jnp.zeros_like(w_s))
    mixed = jnp.einsum("gts,bcsgd->bctgd", w, vc) + b_s.T[:, :, None]
    return u * mixed.reshape(B, S, N_HEADS * HEAD_DIM)


def latent_attention(c_q, c_kv, k_rope_raw, pos, q_norm_g, kv_norm_g, w_uq, w_ukv):
    B, S, _ = c_q.shape
    q = to_heads(rmsnorm(c_q, q_norm_g) @ w_uq, N_HEADS)
    q_nope, q_rope = q[..., :QK_NOPE_DIM], q[..., QK_NOPE_DIM:]
    kv = to_heads(rmsnorm(c_kv, kv_norm_g) @ w_ukv, N_HEADS)
    k_nope, v = kv[..., :QK_NOPE_DIM], kv[..., QK_NOPE_DIM:]
    q_rope = apply_rope(q_rope, pos)
    k_rope = apply_rope(k_rope_raw[:, None], pos)
    q_full = jnp.concatenate([q_nope, q_rope], axis=-1)
    k_full = jnp.concatenate(
        [k_nope, jnp.broadcast_to(k_rope, (B, N_HEADS, S, QK_ROPE_DIM))], axis=-1)
    return causal_block_attention(q_full, k_full, v, (QK_NOPE_DIM + QK_ROPE_DIM) ** -0.5)


def forgetting_attention(q, k, v, f_logit, b_f):
    log_f = jax.nn.log_sigmoid(f_logit.astype(jnp.float32) + b_f.astype(jnp.float32))
    cum = jnp.cumsum(log_f, axis=1).transpose(0, 2, 1)
    return causal_block_attention(to_heads(q, N_HEADS), to_heads(k, N_HEADS),
                                  to_heads(v, N_HEADS), HEAD_DIM ** -0.5, cum)


def causal_depthwise_conv(g, w, b):
    S = g.shape[1]
    gp = jnp.pad(g, ((0, 0), (CONV_WIDTH - 1, 0), (0, 0)))
    y = b
    for j in range(CONV_WIDTH):
        y = y + w[j] * gp[:, j:j + S]
    return y


def hybrid_layer(x, pos, slopes, norm_mix_g, w_in, sgu_ln_g, sgu_ln_b, sgu_w, sgu_b,
                 mla_q_norm_g, mla_kv_norm_g, mla_w_uq, mla_w_ukv, fox_b_f, group_norm_g,
                 w_o, norm_ffn_g, w_gate, w_val, conv_w, conv_b, w_down):
    h = rmsnorm(x, norm_mix_g)
    proj = h @ w_in
    (a_q, a_k, a_v, b_u, b_v, c_q, c_kv, c_kr,
     d_q, d_k, d_v, d_f) = jnp.split(proj, split_points(), axis=-1)
    y_a = merge_heads(moba_attention(to_heads(a_q, N_HEADS), to_heads(a_k, N_HEADS),
                                     to_heads(a_v, N_HEADS), slopes))
    y_b = chunked_spatial_gating(jax.nn.gelu(b_u), jax.nn.gelu(b_v),
                                 sgu_ln_g, sgu_ln_b, sgu_w, sgu_b)
    y_c = merge_heads(latent_attention(c_q, c_kv, c_kr, pos, mla_q_norm_g, mla_kv_norm_g,
                                       mla_w_uq, mla_w_ukv))
    y_d = merge_heads(forgetting_attention(d_q, d_k, d_v, d_f, fox_b_f))
    groups = [y_a, y_b, y_c, y_d]
    y = jnp.concatenate([rmsnorm(g, group_norm_g[i]) for i, g in enumerate(groups)], axis=-1)
    x = x + y @ w_o
    h = rmsnorm(x, norm_ffn_g)
    gate = causal_depthwise_conv(h @ w_gate, conv_w, conv_b)
    x = x + (jax.nn.silu(gate) * (h @ w_val)) @ w_down
    return x


def setup_inputs(seed: int = 0) -> dict:
    key = jax.random.key(seed)
    ks = jax.random.split(key, 24)
    f32 = jnp.float32

    def nrm(k, shape, scale):
        return jax.random.normal(k, shape, f32) * scale

    def gain(k, shape):
        return 1.0 + 0.02 * jax.random.normal(k, shape, f32)

    L = DEPTH
    return {
        "x": jax.random.normal(ks[0], (BATCH, SEQ, D_MODEL), f32),
        "norm_mix_g": gain(ks[1], (L, D_MODEL)),
        "w_in": nrm(ks[2], (L, D_MODEL, IN_COLS), D_MODEL ** -0.5),
        "sgu_ln_g": gain(ks[3], (L, N_HEADS, HEAD_DIM)),
        "sgu_ln_b": nrm(ks[4], (L, N_HEADS, HEAD_DIM), 0.02),
        "sgu_w": nrm(ks[5], (L, N_HEADS, SGU_CHUNK, SGU_CHUNK), SGU_CHUNK ** -0.5),
        "sgu_b": gain(ks[6], (L, N_HEADS, SGU_CHUNK)),
        "mla_q_norm_g": gain(ks[7], (L, Q_LORA_RANK)),
        "mla_kv_norm_g": gain(ks[8], (L, KV_LORA_RANK)),
        "mla_w_uq": nrm(ks[9], (L, Q_LORA_RANK, N_HEADS * (QK_NOPE_DIM + QK_ROPE_DIM)),
                        Q_LORA_RANK ** -0.5),
        "mla_w_ukv": nrm(ks[10], (L, KV_LORA_RANK, N_HEADS * (QK_NOPE_DIM + V_HEAD_DIM)),
                         KV_LORA_RANK ** -0.5),
        "fox_b_f": nrm(ks[11], (L, N_HEADS), 0.1),
        "group_norm_g": gain(ks[12], (L, N_MIXERS, GROUP_WIDTH)),
        "w_o": nrm(ks[13], (L, D_MIX, D_MODEL), D_MIX ** -0.5),
        "norm_ffn_g": gain(ks[14], (L, D_MODEL)),
        "w_gate": nrm(ks[15], (L, D_MODEL, D_FF), D_MODEL ** -0.5),
        "w_val": nrm(ks[16], (L, D_MODEL, D_FF), D_MODEL ** -0.5),
        "conv_w": nrm(ks[17], (L, CONV_WIDTH, D_FF), CONV_WIDTH ** -0.5),
        "conv_b": nrm(ks[18], (L, D_FF), 0.01),
        "w_down": nrm(ks[19], (L, D_FF, D_MODEL), D_FF ** -0.5),
        "final_norm_g": gain(ks[20], (D_MODEL,)),
    }


def reference(x, norm_mix_g, w_in, sgu_ln_g, sgu_ln_b, sgu_w, sgu_b, mla_q_norm_g,
              mla_kv_norm_g, mla_w_uq, mla_w_ukv, fox_b_f, group_norm_g, w_o, norm_ffn_g,
              w_gate, w_val, conv_w, conv_b, w_down, final_norm_g):
    pos = jnp.arange(x.shape[1])
    slopes = alibi_slopes(N_HEADS)
    for l in range(DEPTH):
        x = hybrid_layer(x, pos, slopes, norm_mix_g[l], w_in[l], sgu_ln_g[l], sgu_ln_b[l],
                         sgu_w[l], sgu_b[l], mla_q_norm_g[l], mla_kv_norm_g[l], mla_w_uq[l],
                         mla_w_ukv[l], fox_b_f[l], group_norm_g[l], w_o[l], norm_ffn_g[l],
                         w_gate[l], w_val[l], conv_w[l], conv_b[l], w_down[l])
    return rmsnorm(x, final_norm_g)
```

```python
import functools

import jax
import jax.numpy as jnp
from jax import lax
from jax.experimental import pallas as pl
from jax.experimental.pallas import tpu as pltpu

F32 = jnp.float32
MXU_DTYPE = jnp.bfloat16

LANE = 128
HEAD_DIM = 128
MOBA_BLOCK = 256
MOBA_TOPK = 3
SGU_CHUNK = 128
ROPE_DIM = 64
ROPE_THETA = 10000.0
CONV_WIDTH = 3
NORM_EPS = 1e-6
NEG_INF = -1e30

ATTN_BLOCK = 256
CONV_HALO = 16
PROJ_COL_CHUNK = 1024
PROJ_ROW_CHUNK = 128
VMEM_LIMIT_BYTES = 56 * 1024 * 1024


def _params(semantics):
    return pltpu.CompilerParams(dimension_semantics=semantics, vmem_limit_bytes=VMEM_LIMIT_BYTES)


def _rmsnorm_rows(x, g):
    ms = jnp.mean(x * x, axis=-1, keepdims=True)
    return x * lax.rsqrt(ms + NORM_EPS) * g


def _norm_kernel(x_ref, g_ref, o_ref):
    o_ref[...] = _rmsnorm_rows(x_ref[...], g_ref[...]).astype(o_ref.dtype)


def _rmsnorm(x, g, out_dtype):
    t, d = x.shape
    tm = min(256, t)
    return pl.pallas_call(
        _norm_kernel,
        out_shape=jax.ShapeDtypeStruct((t, d), out_dtype),
        grid=(t // tm,),
        in_specs=[pl.BlockSpec((tm, d), lambda i: (i, 0)),
                  pl.BlockSpec((1, d), lambda i: (0, 0))],
        out_specs=pl.BlockSpec((tm, d), lambda i: (i, 0)),
        compiler_params=_params(("parallel",)),
        name="rmsnorm",
    )(x, g.reshape(1, d))


def _mm_groups_kernel(a_ref, w_ref, o_ref):
    acc = jnp.dot(a_ref[...], w_ref[...], preferred_element_type=F32)
    for g in range(o_ref.shape[0]):
        o_ref[g] = acc[:, g * LANE:(g + 1) * LANE].astype(o_ref.dtype)


def _matmul_groups(a, w_stack, layer, tm, bn):
    t, k = a.shape
    n = w_stack.shape[-1]
    tm, bn = min(tm, t), min(bn, n)
    return pl.pallas_call(
        _mm_groups_kernel,
        out_shape=jax.ShapeDtypeStruct((n // LANE, t, LANE), MXU_DTYPE),
        grid=(t // tm, n // bn),
        in_specs=[pl.BlockSpec((tm, k), lambda i, j: (i, 0)),
                  pl.BlockSpec((None, k, bn), lambda i, j: (layer, 0, j))],
        out_specs=pl.BlockSpec((bn // LANE, tm, LANE), lambda i, j: (j, i, 0)),
        compiler_params=_params(("parallel", "arbitrary")),
        name="in_proj_heads",
    )(a, w_stack)


def _mm_plain_kernel(a_ref, w_ref, o_ref):
    o_ref[...] = jnp.dot(a_ref[...], w_ref[...], preferred_element_type=F32).astype(o_ref.dtype)


def _matmul_f32(a, w_stack, layer, tm):
    t, k = a.shape
    n = w_stack.shape[-1]
    tm = min(tm, t)
    return pl.pallas_call(
        _mm_plain_kernel,
        out_shape=jax.ShapeDtypeStruct((t, n), F32),
        grid=(t // tm,),
        in_specs=[pl.BlockSpec((tm, k), lambda i: (i, 0)),
                  pl.BlockSpec((None, k, n), lambda i: (layer, 0, 0))],
        out_specs=pl.BlockSpec((tm, n), lambda i: (i, 0)),
        compiler_params=_params(("parallel",)),
        name="in_proj_latent",
    )(a, w_stack)


def _mla_up_kernel(c_ref, gq_ref, gkv_ref, wq_ref, wkv_ref, cos_ref, sin_ref,
                   qn_ref, qr_ref, kn_ref, v_ref, kr_ref, *, q_rank, kv_rank, scale):
    n_heads = qn_ref.shape[0]
    hw = n_heads * HEAD_DIM
    c = c_ref[...]
    cq = _rmsnorm_rows(c[:, :q_rank], gq_ref[...]).astype(MXU_DTYPE)
    ckv = _rmsnorm_rows(c[:, q_rank:q_rank + kv_rank], gkv_ref[...]).astype(MXU_DTYPE)
    q = jnp.dot(cq, wq_ref[...], preferred_element_type=F32)
    kv = jnp.dot(ckv, wkv_ref[...], preferred_element_type=F32)
    cos, sin = cos_ref[...], sin_ref[...]
    for h in range(n_heads):
        lo, hi = h * HEAD_DIM, (h + 1) * HEAD_DIM
        qn_ref[h] = (q[:, lo:hi] * scale).astype(qn_ref.dtype)
        roped = q[:, hw + lo:hw + hi] * cos + q[:, 2 * hw + lo:2 * hw + hi] * sin
        qr_ref[h] = (roped * scale).astype(qr_ref.dtype)
        kn_ref[h] = kv[:, lo:hi].astype(kn_ref.dtype)
        v_ref[h] = kv[:, hw + lo:hw + hi].astype(v_ref.dtype)
    o = q_rank + kv_rank
    kr_ref[...] = (c[:, o:o + LANE] * cos + c[:, o + LANE:o + 2 * LANE] * sin).astype(kr_ref.dtype)


def _mla_up(c, gq, gkv, wq_stack, wkv_stack, cos, sin, layer, n_heads, seq, scale):
    t, cw = c.shape
    q_rank, kv_rank = gq.shape[-1], gkv.shape[-1]
    tm = min(512, seq)
    per_seq = seq // tm
    heads = jax.ShapeDtypeStruct((n_heads, t, LANE), MXU_DTYPE)
    head_spec = pl.BlockSpec((n_heads, tm, LANE), lambda i: (0, i, 0))
    return pl.pallas_call(
        functools.partial(_mla_up_kernel, q_rank=q_rank, kv_rank=kv_rank, scale=scale),
        out_shape=(heads, heads, heads, heads, jax.ShapeDtypeStruct((t, LANE), MXU_DTYPE)),
        grid=(t // tm,),
        in_specs=[pl.BlockSpec((tm, cw), lambda i: (i, 0)),
                  pl.BlockSpec((1, q_rank), lambda i: (0, 0)),
                  pl.BlockSpec((1, kv_rank), lambda i: (0, 0)),
                  pl.BlockSpec((None,) + wq_stack.shape[1:], lambda i: (layer, 0, 0)),
                  pl.BlockSpec((None,) + wkv_stack.shape[1:], lambda i: (layer, 0, 0)),
                  pl.BlockSpec((tm, LANE), lambda i: (i % per_seq, 0)),
                  pl.BlockSpec((tm, LANE), lambda i: (i % per_seq, 0))],
        out_specs=(head_spec, head_spec, head_spec, head_spec,
                   pl.BlockSpec((tm, LANE), lambda i: (i, 0))),
        compiler_params=_params(("parallel",)),
        name="mla_up",
    )(c, gq.reshape(1, q_rank), gkv.reshape(1, kv_rank), wq_stack, wkv_stack, cos, sin)


def _fox_gate_kernel(f_ref, b_ref, row_ref, col_ref):
    n_heads = col_ref.shape[0]
    z = f_ref[...] + b_ref[...]
    x = -(jnp.maximum(-z, 0.0) + jnp.log1p(jnp.exp(-jnp.abs(z))))
    s = z.shape[0]
    rows = lax.broadcasted_iota(jnp.int32, z.shape, 0)
    shift = 1
    while shift < s:
        x = x + jnp.where(rows >= shift, pltpu.roll(x, shift, axis=0), 0.0)
        shift *= 2
    row_ref[...] = x.T[:n_heads, :]
    for h in range(n_heads):
        col_ref[h] = jnp.broadcast_to(x[:, h:h + 1], x.shape)


def _fox_gate(c, b_f, f_col_block, batch, seq, n_heads):
    b_pad = jnp.zeros((1, LANE), F32).at[0, :n_heads].set(b_f)
    return pl.pallas_call(
        _fox_gate_kernel,
        out_shape=(jax.ShapeDtypeStruct((batch, n_heads, seq), F32),
                   jax.ShapeDtypeStruct((batch, n_heads, seq, LANE), F32)),
        grid=(batch,),
        in_specs=[pl.BlockSpec((seq, LANE), lambda b: (b, f_col_block)),
                  pl.BlockSpec((1, LANE), lambda b: (0, 0))],
        out_specs=(pl.BlockSpec((None, n_heads, seq), lambda b: (b, 0, 0)),
                   pl.BlockSpec((None, n_heads, seq, LANE), lambda b: (b, 0, 0, 0))),
        compiler_params=_params(("parallel",)),
        name="fox_gate",
    )(c, b_pad)


def _online_softmax_step(s, v, m, l, acc):
    m_new = jnp.maximum(m, jnp.max(s, axis=-1, keepdims=True))
    alpha = jnp.exp(m - m_new)
    p = jnp.exp(s - m_new)
    l = alpha * l + jnp.sum(p, axis=-1, keepdims=True)
    acc = alpha * acc + jnp.dot(p.astype(v.dtype), v, preferred_element_type=F32)
    return m_new, l, acc


def _qk(q, k):
    return lax.dot_general(q, k, (((1,), (1,)), ((), ())), preferred_element_type=F32)


def _softmax_init(tq):
    return (jnp.full((tq, 1), NEG_INF, F32), jnp.zeros((tq, 1), F32), jnp.zeros((tq, HEAD_DIM), F32))


def _moba_kernel(slopes_ref, q_ref, k_ref, v_ref, o_ref, kmean_ref, *, scale):
    h, qi = pl.program_id(1), pl.program_id(2)
    blk = q_ref.shape[0]
    n_blk = k_ref.shape[0] // blk

    @pl.when(qi == 0)
    def _():
        kmean_ref[...] = jnp.zeros_like(kmean_ref)
        for n in range(n_blk):
            kb = k_ref[n * blk:(n + 1) * blk, :].astype(F32)
            kmean_ref[n:n + 1, :] = jnp.mean(kb, axis=0, keepdims=True)

    q = q_ref[...]
    gate = lax.dot_general(q.astype(F32), kmean_ref[...], (((1,), (1,)), ((), ())),
                           precision=lax.Precision.HIGHEST, preferred_element_type=F32)
    lane = lax.broadcasted_iota(jnp.int32, gate.shape, 1)
    rank = jnp.zeros(gate.shape, F32)
    for n in range(n_blk):
        g = gate[:, n:n + 1]
        beats = (g > gate) | ((g == gate) & (n < lane))
        rank = rank + jnp.where(beats, jnp.where(n < qi, 1.0, 0.0), 0.0)
    chosen = jnp.where((rank < MOBA_TOPK) & (lane < qi), 1.0, 0.0)

    slope = slopes_ref[h]
    r = lax.broadcasted_iota(jnp.int32, (blk, blk), 0)
    c = lax.broadcasted_iota(jnp.int32, (blk, blk), 1)
    in_block_penalty = -slope * (r - c).astype(F32)

    def past_block(j, carry):
        start = pl.multiple_of(j * blk, blk)
        kj = k_ref[pl.ds(start, blk), :]
        vj = v_ref[pl.ds(start, blk), :]
        s = _qk(q, kj) * scale + in_block_penalty - slope * ((qi - j) * blk).astype(F32)
        keep = jnp.sum(jnp.where(lane == j, chosen, 0.0), axis=1, keepdims=True) > 0.5
        s = jnp.where(keep, s, NEG_INF)
        return _online_softmax_step(s, vj, *carry)

    m, l, acc = lax.fori_loop(0, qi, past_block, _softmax_init(blk))
    start = pl.multiple_of(qi * blk, blk)
    s = _qk(q, k_ref[pl.ds(start, blk), :]) * scale + in_block_penalty
    s = jnp.where(c <= r, s, NEG_INF)
    m, l, acc = _online_softmax_step(s, v_ref[pl.ds(start, blk), :], m, l, acc)
    o_ref[...] = (acc / l).astype(o_ref.dtype)


def _moba(proj, batch, seq, n_heads, q_group, k_group, v_group):
    t = proj.shape[1]
    blk = MOBA_BLOCK
    nq = seq // blk
    slopes = jnp.exp2(-8.0 * jnp.arange(1, n_heads + 1, dtype=F32) / n_heads)
    return pl.pallas_call(
        functools.partial(_moba_kernel, scale=HEAD_DIM ** -0.5),
        out_shape=jax.ShapeDtypeStruct((n_heads, t, HEAD_DIM), MXU_DTYPE),
        grid=(batch, n_heads, nq),
        in_specs=[pl.BlockSpec(memory_space=pltpu.SMEM),
                  pl.BlockSpec((None, blk, HEAD_DIM), lambda b, h, i: (q_group + h, b * nq + i, 0)),
                  pl.BlockSpec((None, seq, HEAD_DIM), lambda b, h, i: (k_group + h, b, 0)),
                  pl.BlockSpec((None, seq, HEAD_DIM), lambda b, h, i: (v_group + h, b, 0))],
        out_specs=pl.BlockSpec((None, blk, HEAD_DIM), lambda b, h, i: (h, b * nq + i, 0)),
        scratch_shapes=[pltpu.VMEM((LANE, HEAD_DIM), F32)],
        compiler_params=_params(("parallel", "parallel", "arbitrary")),
        name="moba_attention",
    )(slopes, proj, proj, proj)


def _causal_sweep(q, get_k, get_v, bias, qi, tq):
    r = lax.broadcasted_iota(jnp.int32, (tq, tq), 0)
    c = lax.broadcasted_iota(jnp.int32, (tq, tq), 1)

    def past_block(j, carry):
        start = pl.multiple_of(j * tq, tq)
        s = bias(_qk(q, get_k(start)), j)
        return _online_softmax_step(s, get_v(start), *carry)

    m, l, acc = lax.fori_loop(0, qi, past_block, _softmax_init(tq))
    start = pl.multiple_of(qi * tq, tq)
    s = jnp.where(c <= r, bias(_qk(q, get_k(start)), qi), NEG_INF)
    m, l, acc = _online_softmax_step(s, get_v(start), m, l, acc)
    return acc / l


def _mla_attn_kernel(qn_ref, qr_ref, kn_ref, kr_ref, v_ref, o_ref):
    qi = pl.program_id(2)
    tq = qn_ref.shape[0]
    q = jnp.concatenate([qn_ref[...], qr_ref[...]], axis=1)

    def get_k(start):
        return jnp.concatenate([kn_ref[pl.ds(start, tq), :], kr_ref[pl.ds(start, tq), :]], axis=1)

    out = _causal_sweep(q, get_k, lambda start: v_ref[pl.ds(start, tq), :], lambda s, j: s, qi, tq)
    o_ref[...] = out.astype(o_ref.dtype)


def _mla_attention(qn, qr, kn, kr, v, batch, seq):
    n_heads, t, _ = qn.shape
    tq = min(ATTN_BLOCK, seq)
    nq = seq // tq
    q_spec = pl.BlockSpec((None, tq, HEAD_DIM), lambda b, h, i: (h, b * nq + i, 0))
    kv_spec = pl.BlockSpec((None, seq, HEAD_DIM), lambda b, h, i: (h, b, 0))
    return pl.pallas_call(
        _mla_attn_kernel,
        out_shape=jax.ShapeDtypeStruct((n_heads, t, HEAD_DIM), MXU_DTYPE),
        grid=(batch, n_heads, nq),
        in_specs=[q_spec, q_spec, kv_spec,
                  pl.BlockSpec((seq, HEAD_DIM), lambda b, h, i: (b, 0)), kv_spec],
        out_specs=q_spec,
        compiler_params=_params(("parallel", "parallel", "arbitrary")),
        name="mla_attention",
    )(qn, qr, kn, kr, v)


def _fox_attn_kernel(q_ref, k_ref, v_ref, ccol_ref, crow_ref, o_ref, *, scale):
    qi = pl.program_id(2)
    tq = q_ref.shape[0]
    cq = ccol_ref[...]
    cq = jnp.concatenate([cq] * (tq // LANE), axis=1)

    def bias(s, j):
        return s * scale + cq - crow_ref[pl.ds(j, 1), :]

    out = _causal_sweep(q_ref[...], lambda start: k_ref[pl.ds(start, tq), :],
                        lambda start: v_ref[pl.ds(start, tq), :], bias, qi, tq)
    o_ref[...] = out.astype(o_ref.dtype)


def _fox_attention(proj, cum_row, cum_col, batch, seq, n_heads, q_group, k_group, v_group):
    t = proj.shape[1]
    tq = min(ATTN_BLOCK, seq)
    nq = seq // tq
    cum_row = cum_row.reshape(batch, n_heads, nq, tq)
    return pl.pallas_call(
        functools.partial(_fox_attn_kernel, scale=HEAD_DIM ** -0.5),
        out_shape=jax.ShapeDtypeStruct((n_heads, t, HEAD_DIM), MXU_DTYPE),
        grid=(batch, n_heads, nq),
        in_specs=[pl.BlockSpec((None, tq, HEAD_DIM), lambda b, h, i: (q_group + h, b * nq + i, 0)),
                  pl.BlockSpec((None, seq, HEAD_DIM), lambda b, h, i: (k_group + h, b, 0)),
                  pl.BlockSpec((None, seq, HEAD_DIM), lambda b, h, i: (v_group + h, b, 0)),
                  pl.BlockSpec((None, None, tq, LANE), lambda b, h, i: (b, h, i, 0)),
                  pl.BlockSpec((None, None, nq, tq), lambda b, h, i: (b, h, 0, 0))],
        out_specs=pl.BlockSpec((None, tq, HEAD_DIM), lambda b, h, i: (h, b * nq + i, 0)),
        compiler_params=_params(("parallel", "parallel", "arbitrary")),
        name="fox_attention",
    )(proj, proj, proj, cum_col, cum_row)


def _sgu_kernel(u_ref, v_ref, lg_ref, lb_ref, w_ref, b_ref, o_ref):
    tm = u_ref.shape[0]
    nc = tm // SGU_CHUNK
    v = jax.nn.gelu(v_ref[...].astype(F32))
    vc = v - jnp.mean(v, axis=-1, keepdims=True)
    var = jnp.mean(vc * vc, axis=-1, keepdims=True)
    vn = (vc * lax.rsqrt(var + NORM_EPS) * lg_ref[...] + lb_ref[...]).astype(MXU_DTYPE)
    v_side = jnp.concatenate([vn[c * SGU_CHUNK:(c + 1) * SGU_CHUNK, :] for c in range(nc)], axis=1)
    r = lax.broadcasted_iota(jnp.int32, (SGU_CHUNK, SGU_CHUNK), 0)
    c_ix = lax.broadcasted_iota(jnp.int32, (SGU_CHUNK, SGU_CHUNK), 1)
    w = jnp.where(c_ix <= r, w_ref[...], 0.0).astype(MXU_DTYPE)
    mixed = jnp.dot(w, v_side, preferred_element_type=F32) + b_ref[...]
    u = jax.nn.gelu(u_ref[...].astype(F32))
    for c in range(nc):
        rows = slice(c * SGU_CHUNK, (c + 1) * SGU_CHUNK)
        o_ref[rows, :] = (u[rows, :] * mixed[:, c * HEAD_DIM:(c + 1) * HEAD_DIM]).astype(o_ref.dtype)


def _sgu(proj, ln_g, ln_b, w_s, b_s, n_heads, u_group, v_group):
    t = proj.shape[1]
    tm = min(1024, t)
    vec = pl.BlockSpec((None, 1, HEAD_DIM), lambda g, i: (g, 0, 0))
    return pl.pallas_call(
        _sgu_kernel,
        out_shape=jax.ShapeDtypeStruct((n_heads, t, HEAD_DIM), MXU_DTYPE),
        grid=(n_heads, t // tm),
        in_specs=[pl.BlockSpec((None, tm, HEAD_DIM), lambda g, i: (u_group + g, i, 0)),
                  pl.BlockSpec((None, tm, HEAD_DIM), lambda g, i: (v_group + g, i, 0)),
                  vec, vec,
                  pl.BlockSpec((None, SGU_CHUNK, SGU_CHUNK), lambda g, i: (g, 0, 0)),
                  pl.BlockSpec((None, SGU_CHUNK, 1), lambda g, i: (g, 0, 0))],
        out_specs=pl.BlockSpec((None, tm, HEAD_DIM), lambda g, i: (g, i, 0)),
        compiler_params=_params(("parallel", "parallel")),
        name="spatial_gating",
    )(proj, proj, ln_g[:, None, :], ln_b[:, None, :], w_s, b_s[:, :, None])


def _proj_residual_norm_kernel(*refs, grouped, keep_x):
    if not keep_x:
        refs = refs + (refs[-1],)
    if grouped:
        a_ref, gn_ref, w_ref, resid_ref, g_ref, ox_ref, oh_ref = refs
        parts = [a_ref[i].astype(F32) for i in range(a_ref.shape[0])]
        width = len(parts) * LANE
        ssq = sum(jnp.sum(p * p, axis=-1, keepdims=True) for p in parts)
        rstd = lax.rsqrt(ssq / width + NORM_EPS)
        gn = gn_ref[...]
        a = jnp.concatenate([(p * rstd * gn[:, i * LANE:(i + 1) * LANE]).astype(MXU_DTYPE)
                             for i, p in enumerate(parts)], axis=1)
    else:
        a_ref, w_ref, resid_ref, g_ref, ox_ref, oh_ref = refs
        a = a_ref[...]
    k = pl.program_id(1)
    tm, n = ox_ref.shape

    @pl.when(k == 0)
    def _():
        ox_ref[...] = resid_ref[...]

    for c0 in range(0, n, PROJ_COL_CHUNK):
        cols = slice(c0, min(c0 + PROJ_COL_CHUNK, n))
        ox_ref[:, cols] += jnp.dot(a, w_ref[:, cols], preferred_element_type=F32)

    @pl.when(k == pl.num_programs(1) - 1)
    def _():
        for r0 in range(0, tm, PROJ_ROW_CHUNK):
            rows = slice(r0, min(r0 + PROJ_ROW_CHUNK, tm))
            oh_ref[rows, :] = _rmsnorm_rows(ox_ref[rows, :], g_ref[...]).astype(oh_ref.dtype)


def _proj_residual_norm(a, w_stack, layer, resid, g, h_dtype, group_gain=None, keep_x=True):
    t, n = resid.shape
    tm = min(512, t)
    grouped = group_gain is not None
    if grouped:
        n_mix, width = group_gain.shape
        per = width // LANE
        bk, n_k = width, n_mix
        a_specs = [pl.BlockSpec((per, tm, LANE), lambda i, k: (k, i, 0)),
                   pl.BlockSpec((None, 1, width), lambda i, k: (k, 0, 0))]
        a_args = (a, group_gain[:, None, :])
    else:
        bk = min(1024, a.shape[1])
        n_k = a.shape[1] // bk
        a_specs = [pl.BlockSpec((tm, bk), lambda i, k: (i, k))]
        a_args = (a,)
    row_block = pl.BlockSpec((tm, n), lambda i, k: (i, 0))
    h_shape = jax.ShapeDtypeStruct((t, n), h_dtype)
    assert keep_x or h_dtype == F32
    out_shape = (jax.ShapeDtypeStruct((t, n), F32), h_shape) if keep_x else h_shape
    return pl.pallas_call(
        functools.partial(_proj_residual_norm_kernel, grouped=grouped, keep_x=keep_x),
        out_shape=out_shape,
        grid=(t // tm, n_k),
        in_specs=a_specs + [pl.BlockSpec((None, bk, n), lambda i, k: (layer, k, 0)),
                            pl.BlockSpec((tm, n), lambda i, k: (i, 0), pipeline_mode=pl.Buffered(1)),
                            pl.BlockSpec((1, n), lambda i, k: (0, 0))],
        out_specs=(row_block, row_block) if keep_x else row_block,
        compiler_params=_params(("parallel", "arbitrary")),
        name="out_proj_norm" if grouped else "down_proj_norm",
    )(*a_args, w_stack, resid, g.reshape(1, n))


def _ffn_up_kernel(h_ref, halo_ref, wg_ref, wv_ref, cw_ref, cb_ref, o_ref, hext_ref, *, blocks_per_seq):
    i, j = pl.program_id(0), pl.program_id(1)
    tm = h_ref.shape[0]

    @pl.when(j == 0)
    def _():
        halo = halo_ref[...]
        first = i % blocks_per_seq == 0
        hext_ref[:CONV_HALO, :] = jnp.where(first, jnp.zeros_like(halo), halo)
        hext_ref[CONV_HALO:, :] = h_ref[...]

    g = jnp.dot(hext_ref[...], wg_ref[...], preferred_element_type=F32)
    val = jnp.dot(h_ref[...], wv_ref[...], preferred_element_type=F32)
    cw = cw_ref[...]
    gate = cb_ref[...]
    for tap in range(CONV_WIDTH):
        lo = CONV_HALO - (CONV_WIDTH - 1) + tap
        gate = gate + cw[tap:tap + 1, :] * g[lo:lo + tm, :]
    o_ref[...] = (gate * jax.nn.sigmoid(gate) * val).astype(o_ref.dtype)


def _ffn_up(h, wg_stack, wv_stack, conv_w, conv_b, layer, seq):
    t, d = h.shape
    n = wg_stack.shape[-1]
    tm = min(1024, seq)
    bn = min(512, n)
    per_seq = seq // tm
    halo_blocks = tm // CONV_HALO
    return pl.pallas_call(
        functools.partial(_ffn_up_kernel, blocks_per_seq=per_seq),
        out_shape=jax.ShapeDtypeStruct((t, n), MXU_DTYPE),
        grid=(t // tm, n // bn),
        in_specs=[pl.BlockSpec((tm, d), lambda i, j: (i, 0)),
                  pl.BlockSpec((CONV_HALO, d), lambda i, j: (jnp.maximum(i * halo_blocks - 1, 0), 0)),
                  pl.BlockSpec((None, d, bn), lambda i, j: (layer, 0, j)),
                  pl.BlockSpec((None, d, bn), lambda i, j: (layer, 0, j)),
                  pl.BlockSpec((None, CONV_WIDTH, bn), lambda i, j: (layer, 0, j)),
                  pl.BlockSpec((None, 1, bn), lambda i, j: (layer, 0, j))],
        out_specs=pl.BlockSpec((tm, bn), lambda i, j: (i, j)),
        scratch_shapes=[pltpu.VMEM((tm + CONV_HALO, d), MXU_DTYPE)],
        compiler_params=_params(("parallel", "arbitrary")),
        name="ffn_up_conv",
    )(h, h, wg_stack, wv_stack, conv_w, conv_b)


def _pad_last(w, width):
    return jnp.pad(w, [(0, 0)] * (w.ndim - 1) + [(0, width - w.shape[-1])])


def _rotate_half_columns(w):
    half = ROPE_DIM // 2
    return jnp.concatenate([-w[..., half:], w[..., :half]], axis=-1)


def _pack_weights(w_in, mla_w_uq, mla_w_ukv, w_o, w_gate, w_val, conv_w, conv_b, w_down,
                  group_width, q_rank, kv_rank, n_heads):
    gw = group_width
    a_end = 3 * gw
    b_end = a_end + 2 * gw
    cq_end = b_end + q_rank
    ckv_end = cq_end + kv_rank
    kr_end = ckv_end + ROPE_DIM
    d_end = kr_end + 3 * gw
    w_heads = jnp.concatenate([w_in[..., :b_end], w_in[..., kr_end:d_end]], axis=-1).astype(MXU_DTYPE)
    kr = w_in[..., ckv_end:kr_end]
    w_latent = jnp.concatenate(
        [w_in[..., b_end:ckv_end], _pad_last(kr, LANE), _pad_last(_rotate_half_columns(kr), LANE),
         _pad_last(w_in[..., d_end:], LANE)], axis=-1).astype(MXU_DTYPE)

    layers, qr, _ = mla_w_uq.shape
    uq = mla_w_uq.reshape(layers, qr, n_heads, HEAD_DIM + ROPE_DIM)
    nope, rope = uq[..., :HEAD_DIM], uq[..., HEAD_DIM:]
    flat = lambda w: w.reshape(layers, w.shape[1], n_heads * HEAD_DIM)
    wq = jnp.concatenate([flat(nope), flat(_pad_last(rope, HEAD_DIM)),
                          flat(_pad_last(_rotate_half_columns(rope), HEAD_DIM))], axis=-1).astype(MXU_DTYPE)
    ukv = mla_w_ukv.reshape(layers, kv_rank, n_heads, 2 * HEAD_DIM)
    wkv = jnp.concatenate([flat(ukv[..., :HEAD_DIM]), flat(ukv[..., HEAD_DIM:])], axis=-1).astype(MXU_DTYPE)

    d_ff = w_gate.shape[-1]
    ff_pad = -(-d_ff // 1024) * 1024 if d_ff > 1024 else d_ff
    wg = _pad_last(w_gate, ff_pad).astype(MXU_DTYPE)
    wv = _pad_last(w_val, ff_pad).astype(MXU_DTYPE)
    wd = jnp.pad(w_down, ((0, 0), (0, ff_pad - d_ff), (0, 0))).astype(MXU_DTYPE)
    cw = _pad_last(conv_w, ff_pad)
    cb = _pad_last(conv_b, ff_pad)[:, None, :]
    return w_heads, w_latent, wq, wkv, w_o.astype(MXU_DTYPE), wg, wv, wd, cw, cb


def _rope_tables(seq):
    half = ROPE_DIM // 2
    inv_freq = ROPE_THETA ** (-jnp.arange(half, dtype=F32) / half)
    ang = jnp.arange(seq, dtype=F32)[:, None] * inv_freq[None, :]
    cos = jnp.concatenate([jnp.cos(ang), jnp.cos(ang)], axis=-1)
    sin = jnp.concatenate([jnp.sin(ang), jnp.sin(ang)], axis=-1)
    return _pad_last(cos, LANE), _pad_last(sin, LANE)


def kernel(x, norm_mix_g, w_in, sgu_ln_g, sgu_ln_b, sgu_w, sgu_b, mla_q_norm_g, mla_kv_norm_g,
           mla_w_uq, mla_w_ukv, fox_b_f, group_norm_g, w_o, norm_ffn_g, w_gate, w_val, conv_w,
           conv_b, w_down, final_norm_g):
    batch, seq, d_model = x.shape
    depth = w_in.shape[0]
    n_heads = fox_b_f.shape[-1]
    gw = group_norm_g.shape[-1]
    q_rank, kv_rank = mla_q_norm_g.shape[-1], mla_kv_norm_g.shape[-1]
    assert gw == n_heads * HEAD_DIM and seq % MOBA_BLOCK == 0
    assert q_rank % LANE == 0 and kv_rank % LANE == 0

    (w_heads, w_latent, wq, wkv, wo, wg, wv, wd, cw, cb) = _pack_weights(
        w_in, mla_w_uq, mla_w_ukv, w_o, w_gate, w_val, conv_w, conv_b, w_down, gw, q_rank, kv_rank, n_heads)
    cos, sin = _rope_tables(seq)
    f_col_block = (q_rank + kv_rank + 2 * LANE) // LANE
    mla_scale = (HEAD_DIM + ROPE_DIM) ** -0.5
    a_q, a_k, a_v, b_u, b_v, d_q, d_k, d_v = (i * n_heads for i in range(8))

    xf = x.reshape(batch * seq, d_model)
    h = _rmsnorm(xf, norm_mix_g[0], MXU_DTYPE)
    for l in range(depth):
        proj = _matmul_groups(h, w_heads, l, 1024, 1024)
        latent = _matmul_f32(h, w_latent, l, 512)
        qn, qr, kn, vv, kr = _mla_up(latent, mla_q_norm_g[l], mla_kv_norm_g[l], wq, wkv, cos, sin,
                                     l, n_heads, seq, mla_scale)
        cum_row, cum_col = _fox_gate(latent, fox_b_f[l], f_col_block, batch, seq, n_heads)
        y_a = _moba(proj, batch, seq, n_heads, a_q, a_k, a_v)
        y_b = _sgu(proj, sgu_ln_g[l], sgu_ln_b[l], sgu_w[l], sgu_b[l], n_heads, b_u, b_v)
        y_c = _mla_attention(qn, qr, kn, kr, vv, batch, seq)
        y_d = _fox_attention(proj, cum_row, cum_col, batch, seq, n_heads, d_q, d_k, d_v)
        y = jnp.concatenate([y_a, y_b, y_c, y_d], axis=0)
        xf, h = _proj_residual_norm(y, wo, l, xf, norm_ffn_g[l], MXU_DTYPE, group_gain=group_norm_g[l])
        act = _ffn_up(h, wg, wv, cw, cb, l, seq)
        if l == depth - 1:
            h = _proj_residual_norm(act, wd, l, xf, final_norm_g, F32, keep_x=False)
        else:
            xf, h = _proj_residual_norm(act, wd, l, xf, norm_mix_g[l + 1], MXU_DTYPE)
    return h.reshape(batch, seq, d_model)
```

```python
import functools
import math

import jax
import jax.numpy as jnp
from jax import lax
from jax.experimental import pallas as pl
from jax.experimental.pallas import tpu as pltpu

F32 = jnp.float32
MXU_DTYPE = jnp.bfloat16

LANE = 128
HEAD_DIM = 128
MOBA_BLOCK = 256
MOBA_TOPK = 3
SGU_CHUNK = 128
ROPE_DIM = 64
ROPE_THETA = 10000.0
CONV_WIDTH = 3
NORM_EPS = 1e-6
NEG_INF = -1e30
LOG2E = math.log2(math.e)

ATTN_BLOCK = 256
MOBA_MAX_BLOCKS = 8
CONV_HALO = 16
PROJ_COL_CHUNK = 1024
PROJ_ROW_CHUNK = 128
VMEM_LIMIT_BYTES = 56 * 1024 * 1024


def _params(semantics):
    return pltpu.CompilerParams(dimension_semantics=semantics, vmem_limit_bytes=VMEM_LIMIT_BYTES)


def _rmsnorm_rows(x, g):
    ms = jnp.mean(x * x, axis=-1, keepdims=True)
    return x * lax.rsqrt(ms + NORM_EPS) * g


def _split3(x):
    hi = x.astype(MXU_DTYPE).astype(F32)
    r = x - hi
    mid = r.astype(MXU_DTYPE).astype(F32)
    lo = (r - mid).astype(MXU_DTYPE).astype(F32)
    return hi, mid, lo


def _norm_kernel(x_ref, g_ref, o_ref):
    o_ref[...] = _rmsnorm_rows(x_ref[...], g_ref[...]).astype(o_ref.dtype)


def _rmsnorm(x, g, out_dtype):
    t, d = x.shape
    tm = min(256, t)
    return pl.pallas_call(
        _norm_kernel,
        out_shape=jax.ShapeDtypeStruct((t, d), out_dtype),
        grid=(t // tm,),
        in_specs=[pl.BlockSpec((tm, d), lambda i: (i, 0)),
                  pl.BlockSpec((1, d), lambda i: (0, 0))],
        out_specs=pl.BlockSpec((tm, d), lambda i: (i, 0)),
        compiler_params=_params(("parallel",)),
        name="rmsnorm",
    )(x, g.reshape(1, d))


def _mm_groups_kernel(a_ref, w_ref, o_ref):
    acc = jnp.dot(a_ref[...], w_ref[...], preferred_element_type=F32)
    for g in range(o_ref.shape[0]):
        o_ref[g] = acc[:, g * LANE:(g + 1) * LANE].astype(o_ref.dtype)


def _matmul_groups(a, w_stack, layer, tm, bn):
    t, k = a.shape
    n = w_stack.shape[-1]
    tm, bn = min(tm, t), min(bn, n)
    return pl.pallas_call(
        _mm_groups_kernel,
        out_shape=jax.ShapeDtypeStruct((n // LANE, t, LANE), MXU_DTYPE),
        grid=(t // tm, n // bn),
        in_specs=[pl.BlockSpec((tm, k), lambda i, j: (i, 0)),
                  pl.BlockSpec((None, k, bn), lambda i, j: (layer, 0, j))],
        out_specs=pl.BlockSpec((bn // LANE, tm, LANE), lambda i, j: (j, i, 0)),
        compiler_params=_params(("parallel", "arbitrary")),
        name="in_proj_heads",
    )(a, w_stack)


def _mm_plain_kernel(a_ref, w_ref, o_ref):
    o_ref[...] = jnp.dot(a_ref[...], w_ref[...], preferred_element_type=F32).astype(o_ref.dtype)


def _matmul_f32(a, w_stack, layer, tm):
    t, k = a.shape
    n = w_stack.shape[-1]
    tm = min(tm, t)
    return pl.pallas_call(
        _mm_plain_kernel,
        out_shape=jax.ShapeDtypeStruct((t, n), F32),
        grid=(t // tm,),
        in_specs=[pl.BlockSpec((tm, k), lambda i: (i, 0)),
                  pl.BlockSpec((None, k, n), lambda i: (layer, 0, 0))],
        out_specs=pl.BlockSpec((tm, n), lambda i: (i, 0)),
        compiler_params=_params(("parallel",)),
        name="in_proj_latent",
    )(a, w_stack)


def _mla_up_kernel(c_ref, gq_ref, gkv_ref, wq_ref, wkv_ref, cos_ref, sin_ref,
                   qn_ref, qr_ref, kn_ref, v_ref, kr_ref, *, q_rank, kv_rank, scale):
    n_heads = qn_ref.shape[0]
    hw = n_heads * HEAD_DIM
    c = c_ref[...]
    cq = _rmsnorm_rows(c[:, :q_rank], gq_ref[...]).astype(MXU_DTYPE)
    ckv = _rmsnorm_rows(c[:, q_rank:q_rank + kv_rank], gkv_ref[...]).astype(MXU_DTYPE)
    q = jnp.dot(cq, wq_ref[...], preferred_element_type=F32)
    kv = jnp.dot(ckv, wkv_ref[...], preferred_element_type=F32)
    cos, sin = cos_ref[...], sin_ref[...]
    for h in range(n_heads):
        lo, hi = h * HEAD_DIM, (h + 1) * HEAD_DIM
        qn_ref[h] = (q[:, lo:hi] * scale).astype(qn_ref.dtype)
        roped = q[:, hw + lo:hw + hi] * cos + q[:, 2 * hw + lo:2 * hw + hi] * sin
        qr_ref[h] = (roped * scale).astype(qr_ref.dtype)
        kn_ref[h] = kv[:, lo:hi].astype(kn_ref.dtype)
        v_ref[h] = kv[:, hw + lo:hw + hi].astype(v_ref.dtype)
    o = q_rank + kv_rank
    kr_ref[...] = (c[:, o:o + LANE] * cos + c[:, o + LANE:o + 2 * LANE] * sin).astype(kr_ref.dtype)


def _mla_up(c, gq, gkv, wq_stack, wkv_stack, cos, sin, layer, n_heads, seq, scale):
    t, cw = c.shape
    q_rank, kv_rank = gq.shape[-1], gkv.shape[-1]
    tm = min(512, seq)
    per_seq = seq // tm
    heads = jax.ShapeDtypeStruct((n_heads, t, LANE), MXU_DTYPE)
    head_spec = pl.BlockSpec((n_heads, tm, LANE), lambda i: (0, i, 0))
    return pl.pallas_call(
        functools.partial(_mla_up_kernel, q_rank=q_rank, kv_rank=kv_rank, scale=scale),
        out_shape=(heads, heads, heads, heads, jax.ShapeDtypeStruct((t, LANE), MXU_DTYPE)),
        grid=(t // tm,),
        in_specs=[pl.BlockSpec((tm, cw), lambda i: (i, 0)),
                  pl.BlockSpec((1, q_rank), lambda i: (0, 0)),
                  pl.BlockSpec((1, kv_rank), lambda i: (0, 0)),
                  pl.BlockSpec((None,) + wq_stack.shape[1:], lambda i: (layer, 0, 0)),
                  pl.BlockSpec((None,) + wkv_stack.shape[1:], lambda i: (layer, 0, 0)),
                  pl.BlockSpec((tm, LANE), lambda i: (i % per_seq, 0)),
                  pl.BlockSpec((tm, LANE), lambda i: (i % per_seq, 0))],
        out_specs=(head_spec, head_spec, head_spec, head_spec,
                   pl.BlockSpec((tm, LANE), lambda i: (i, 0))),
        compiler_params=_params(("parallel",)),
        name="mla_up",
    )(c, gq.reshape(1, q_rank), gkv.reshape(1, kv_rank), wq_stack, wkv_stack, cos, sin)


def _fox_gate_kernel(f_ref, b_ref, qa_ref, ka_ref):
    n_heads = qa_ref.shape[0]
    z = f_ref[...] + b_ref[...]
    x = -(jnp.maximum(-z, 0.0) + jnp.log1p(jnp.exp(-jnp.abs(z))))
    s = z.shape[0]
    rows = lax.broadcasted_iota(jnp.int32, z.shape, 0)
    shift = 1
    while shift < s:
        x = x + jnp.where(rows >= shift, pltpu.roll(x, shift, axis=0), 0.0)
        shift *= 2
    x = x * LOG2E
    lane = lax.broadcasted_iota(jnp.int32, z.shape, 1)
    ones = jnp.where(lane < 3, 1.0, 0.0)
    for h in range(n_heads):
        hi, mid, lo = _split3(jnp.broadcast_to(x[:, h:h + 1], x.shape))
        pieces = jnp.where(lane == 0, hi, jnp.where(lane == 1, mid, jnp.where(lane == 2, lo, 0.0)))
        qa_ref[h] = (pieces + pltpu.roll(ones, 3, axis=1)).astype(qa_ref.dtype)
        ka_ref[h] = (ones - pltpu.roll(pieces, 3, axis=1)).astype(ka_ref.dtype)


def _fox_gate(c, b_f, f_col_block, batch, seq, n_heads):
    b_pad = jnp.zeros((1, LANE), F32).at[0, :n_heads].set(b_f)
    aux = jax.ShapeDtypeStruct((n_heads, batch * seq, LANE), MXU_DTYPE)
    aux_spec = pl.BlockSpec((n_heads, seq, LANE), lambda b: (0, b, 0))
    return pl.pallas_call(
        _fox_gate_kernel,
        out_shape=(aux, aux),
        grid=(batch,),
        in_specs=[pl.BlockSpec((seq, LANE), lambda b: (b, f_col_block)),
                  pl.BlockSpec((1, LANE), lambda b: (0, 0))],
        out_specs=(aux_spec, aux_spec),
        compiler_params=_params(("parallel",)),
        name="fox_gate",
    )(c, b_pad)


def _moba_query_aux(q, kmean_ref, qconst, slope2, qi, blk):
    g = lax.dot_general(kmean_ref[...], q.astype(F32), (((1,), (1,)), ((), ())),
                        precision=lax.Precision.HIGHEST, preferred_element_type=F32)
    row = lax.broadcasted_iota(jnp.int32, g.shape, 0)
    bid = row & (MOBA_MAX_BLOCKS - 1)
    rank = jnp.zeros(g.shape, F32)
    for n in range(qi):
        gn = g[n:n + 1, :]
        rank = rank + jnp.where((gn > g) | ((gn == g) & (n < bid)), 1.0, 0.0)
    past = bid < qi
    chosen = (rank < MOBA_TOPK) & past
    offset = (qi - bid).astype(F32) * (-slope2 * blk)
    term = jnp.where(chosen, offset, jnp.where(past, NEG_INF, 0.0))
    hi = term.astype(MXU_DTYPE).astype(F32)
    lo = jnp.where(chosen, term - hi, 0.0)
    aux_t = jnp.where(row < MOBA_MAX_BLOCKS, hi, lo)
    aux_t = jnp.concatenate([aux_t, jnp.zeros((LANE - aux_t.shape[0], blk), F32)], axis=0)
    return (aux_t.T + qconst).astype(MXU_DTYPE)


def _attention_kernel(*refs, moba):
    if moba:
        slopes_ref, q1_ref, k1_ref, v_ref, kc_ref, qc_ref, o_ref, kk_ref, vv_ref, kmean_ref = refs
        k2_ref = kc_ref
    else:
        q1_ref, q2_ref, k1_ref, k2_ref, v_ref, o_ref, kk_ref, vv_ref = refs
    seq = k1_ref.shape[0]
    tq = ATTN_BLOCK
    kk_ref[:, :HEAD_DIM] = k1_ref[...]
    kk_ref[:, HEAD_DIM:] = k2_ref[...]
    vv_ref[:, :HEAD_DIM] = v_ref[...]
    vv_ref[:, HEAD_DIM:] = jnp.ones((seq, HEAD_DIM), vv_ref.dtype)
    if moba:
        slope2 = slopes_ref[pl.program_id(1)] * LOG2E
        qconst = qc_ref[...]
        kmean_ref[...] = jnp.zeros_like(kmean_ref)
        for n in range(seq // tq):
            mean = jnp.mean(k1_ref[n * tq:(n + 1) * tq, :].astype(F32), axis=0, keepdims=True)
            kmean_ref[n:n + 1, :] = mean
            kmean_ref[MOBA_MAX_BLOCKS + n:MOBA_MAX_BLOCKS + n + 1, :] = mean
    r = lax.broadcasted_iota(jnp.int32, (tq, tq), 0)
    c = lax.broadcasted_iota(jnp.int32, (tq, tq), 1)
    for qi in range(seq // tq):
        rows = slice(qi * tq, (qi + 1) * tq)
        n = (qi + 1) * tq
        q1 = q1_ref[rows, :]
        q2 = _moba_query_aux(q1, kmean_ref, qconst, slope2, qi, tq) if moba else q2_ref[rows, :]
        s = lax.dot_general(jnp.concatenate([q1, q2], axis=1), kk_ref[:n, :],
                            (((1,), (1,)), ((), ())), preferred_element_type=F32)
        diag = jnp.where(c <= r, s[:, n - tq:], NEG_INF)
        s = diag if qi == 0 else jnp.concatenate([s[:, :n - tq], diag], axis=1)
        m = jnp.max(s, axis=-1, keepdims=True)
        p = jnp.exp2(s - m).astype(vv_ref.dtype)
        acc = jnp.dot(p, vv_ref[:n, :], preferred_element_type=F32)
        o_ref[rows, :] = (acc[:, :HEAD_DIM] / acc[:, HEAD_DIM:]).astype(o_ref.dtype)


def _attention(q1, q2, k1, k2, v, batch, seq, n_heads, q1_group=0, k1_group=0, v_group=0,
               k2_shared=False, moba_slopes=None):
    assert seq % ATTN_BLOCK == 0
    t = batch * seq
    moba = moba_slopes is not None
    head = lambda g: pl.BlockSpec((None, seq, HEAD_DIM), lambda b, h: (g + h, b, 0))
    if moba:
        assert seq // ATTN_BLOCK <= MOBA_MAX_BLOCKS and ATTN_BLOCK == MOBA_BLOCK
        in_specs = [pl.BlockSpec(memory_space=pltpu.SMEM), head(q1_group), head(k1_group), head(v_group),
                    pl.BlockSpec((None, seq, LANE), lambda b, h: (h, 0, 0)),
                    pl.BlockSpec((None, ATTN_BLOCK, LANE), lambda b, h: (h, 0, 0))]
        args = (moba_slopes, q1, k1, v, k2, q2)
        scratch = [pltpu.VMEM((2 * MOBA_MAX_BLOCKS, HEAD_DIM), F32)]
    else:
        k2_spec = pl.BlockSpec((seq, LANE), lambda b, h: (b, 0)) if k2_shared else head(0)
        in_specs = [head(q1_group), head(0), head(k1_group), k2_spec, head(v_group)]
        args = (q1, q2, k1, k2, v)
        scratch = []
    return pl.pallas_call(
        functools.partial(_attention_kernel, moba=moba),
        out_shape=jax.ShapeDtypeStruct((n_heads, t, HEAD_DIM), MXU_DTYPE),
        grid=(batch, n_heads),
        in_specs=in_specs,
        out_specs=head(0),
        scratch_shapes=[pltpu.VMEM((seq, 2 * HEAD_DIM), MXU_DTYPE),
                        pltpu.VMEM((seq, 2 * HEAD_DIM), MXU_DTYPE)] + scratch,
        compiler_params=_params(("parallel", "parallel")),
        name="moba_attention" if moba else "causal_attention",
    )(*args)


def _moba_constants(seq, n_heads):
    slopes = jnp.exp2(-8.0 * jnp.arange(1, n_heads + 1, dtype=F32) / n_heads)
    pieces = jnp.stack(_split3(slopes * LOG2E), axis=-1)
    pos = jnp.arange(seq)
    onehot = (pos[:, None] // MOBA_BLOCK == jnp.arange(MOBA_MAX_BLOCKS)[None, :]).astype(F32)
    within = jnp.broadcast_to((pos % MOBA_BLOCK).astype(F32)[None, :, None], (n_heads, seq, 3))
    k_aux = jnp.concatenate([jnp.broadcast_to(onehot, (n_heads, seq, MOBA_MAX_BLOCKS))] * 2
                            + [jnp.broadcast_to(-pieces[:, None, :], (n_heads, seq, 3)), within], axis=-1)
    q_aux = jnp.concatenate([jnp.zeros((n_heads, MOBA_BLOCK, 2 * MOBA_MAX_BLOCKS), F32),
                             within[:, :MOBA_BLOCK], jnp.broadcast_to(pieces[:, None, :], (n_heads, MOBA_BLOCK, 3))],
                            axis=-1)
    return slopes, _pad_last(k_aux, LANE).astype(MXU_DTYPE), _pad_last(q_aux, LANE)


def _sgu_kernel(u_ref, v_ref, lg_ref, lb_ref, w_ref, b_ref, o_ref):
    tm = u_ref.shape[0]
    nc = tm // SGU_CHUNK
    v = jax.nn.gelu(v_ref[...].astype(F32))
    vc = v - jnp.mean(v, axis=-1, keepdims=True)
    var = jnp.mean(vc * vc, axis=-1, keepdims=True)
    vn = (vc * lax.rsqrt(var + NORM_EPS) * lg_ref[...] + lb_ref[...]).astype(MXU_DTYPE)
    v_side = jnp.concatenate([vn[c * SGU_CHUNK:(c + 1) * SGU_CHUNK, :] for c in range(nc)], axis=1)
    r = lax.broadcasted_iota(jnp.int32, (SGU_CHUNK, SGU_CHUNK), 0)
    c_ix = lax.broadcasted_iota(jnp.int32, (SGU_CHUNK, SGU_CHUNK), 1)
    w = jnp.where(c_ix <= r, w_ref[...], 0.0).astype(MXU_DTYPE)
    mixed = jnp.dot(w, v_side, preferred_element_type=F32) + b_ref[...]
    u = jax.nn.gelu(u_ref[...].astype(F32))
    for c in range(nc):
        rows = slice(c * SGU_CHUNK, (c + 1) * SGU_CHUNK)
        o_ref[rows, :] = (u[rows, :] * mixed[:, c * HEAD_DIM:(c + 1) * HEAD_DIM]).astype(o_ref.dtype)


def _sgu(proj, ln_g, ln_b, w_s, b_s, n_heads, u_group, v_group):
    t = proj.shape[1]
    tm = min(1024, t)
    vec = pl.BlockSpec((None, 1, HEAD_DIM), lambda g, i: (g, 0, 0))
    return pl.pallas_call(
        _sgu_kernel,
        out_shape=jax.ShapeDtypeStruct((n_heads, t, HEAD_DIM), MXU_DTYPE),
        grid=(n_heads, t // tm),
        in_specs=[pl.BlockSpec((None, tm, HEAD_DIM), lambda g, i: (u_group + g, i, 0)),
                  pl.BlockSpec((None, tm, HEAD_DIM), lambda g, i: (v_group + g, i, 0)),
                  vec, vec,
                  pl.BlockSpec((None, SGU_CHUNK, SGU_CHUNK), lambda g, i: (g, 0, 0)),
                  pl.BlockSpec((None, SGU_CHUNK, 1), lambda g, i: (g, 0, 0))],
        out_specs=pl.BlockSpec((None, tm, HEAD_DIM), lambda g, i: (g, i, 0)),
        compiler_params=_params(("parallel", "parallel")),
        name="spatial_gating",
    )(proj, proj, ln_g[:, None, :], ln_b[:, None, :], w_s, b_s[:, :, None])


def _proj_residual_norm_kernel(*refs, grouped, keep_x):
    if not keep_x:
        refs = refs + (refs[-1],)
    if grouped:
        a_ref, gn_ref, w_ref, resid_ref, g_ref, ox_ref, oh_ref = refs
        parts = [a_ref[i].astype(F32) for i in range(a_ref.shape[0])]
        width = len(parts) * LANE
        ssq = sum(jnp.sum(p * p, axis=-1, keepdims=True) for p in parts)
        rstd = lax.rsqrt(ssq / width + NORM_EPS)
        gn = gn_ref[...]
        a = jnp.concatenate([(p * rstd * gn[:, i * LANE:(i + 1) * LANE]).astype(MXU_DTYPE)
                             for i, p in enumerate(parts)], axis=1)
    else:
        a_ref, w_ref, resid_ref, g_ref, ox_ref, oh_ref = refs
        a = a_ref[...]
    k = pl.program_id(1)
    tm, n = ox_ref.shape

    @pl.when(k == 0)
    def _():
        ox_ref[...] = resid_ref[...]

    for c0 in range(0, n, PROJ_COL_CHUNK):
        cols = slice(c0, min(c0 + PROJ_COL_CHUNK, n))
        ox_ref[:, cols] += jnp.dot(a, w_ref[:, cols], preferred_element_type=F32)

    @pl.when(k == pl.num_programs(1) - 1)
    def _():
        for r0 in range(0, tm, PROJ_ROW_CHUNK):
            rows = slice(r0, min(r0 + PROJ_ROW_CHUNK, tm))
            oh_ref[rows, :] = _rmsnorm_rows(ox_ref[rows, :], g_ref[...]).astype(oh_ref.dtype)


def _proj_residual_norm(a, w_stack, layer, resid, g, h_dtype, group_gain=None, keep_x=True):
    t, n = resid.shape
    tm = min(512, t)
    grouped = group_gain is not None
    if grouped:
        n_mix, width = group_gain.shape
        per = width // LANE
        bk, n_k = width, n_mix
        a_specs = [pl.BlockSpec((per, tm, LANE), lambda i, k: (k, i, 0)),
                   pl.BlockSpec((None, 1, width), lambda i, k: (k, 0, 0))]
        a_args = (a, group_gain[:, None, :])
    else:
        bk = min(1024, a.shape[1])
        n_k = a.shape[1] // bk
        a_specs = [pl.BlockSpec((tm, bk), lambda i, k: (i, k))]
        a_args = (a,)
    row_block = pl.BlockSpec((tm, n), lambda i, k: (i, 0))
    h_shape = jax.ShapeDtypeStruct((t, n), h_dtype)
    assert keep_x or h_dtype == F32
    out_shape = (jax.ShapeDtypeStruct((t, n), F32), h_shape) if keep_x else h_shape
    return pl.pallas_call(
        functools.partial(_proj_residual_norm_kernel, grouped=grouped, keep_x=keep_x),
        out_shape=out_shape,
        grid=(t // tm, n_k),
        in_specs=a_specs + [pl.BlockSpec((None, bk, n), lambda i, k: (layer, k, 0)),
                            pl.BlockSpec((tm, n), lambda i, k: (i, 0), pipeline_mode=pl.Buffered(1)),
                            pl.BlockSpec((1, n), lambda i, k: (0, 0))],
        out_specs=(row_block, row_block) if keep_x else row_block,
        compiler_params=_params(("parallel", "arbitrary")),
        name="out_proj_norm" if grouped else "down_proj_norm",
    )(*a_args, w_stack, resid, g.reshape(1, n))


def _ffn_up_kernel(h_ref, halo_ref, wg_ref, wv_ref, cw_ref, cb_ref, o_ref, hext_ref, *, blocks_per_seq):
    i, j = pl.program_id(0), pl.program_id(1)
    tm = h_ref.shape[0]

    @pl.when(j == 0)
    def _():
        halo = halo_ref[...]
        first = i % blocks_per_seq == 0
        hext_ref[:CONV_HALO, :] = jnp.where(first, jnp.zeros_like(halo), halo)
        hext_ref[CONV_HALO:, :] = h_ref[...]

    g = jnp.dot(hext_ref[...], wg_ref[...], preferred_element_type=F32)
    val = jnp.dot(h_ref[...], wv_ref[...], preferred_element_type=F32)
    cw = cw_ref[...]
    gate = cb_ref[...]
    for tap in range(CONV_WIDTH):
        lo = CONV_HALO - (CONV_WIDTH - 1) + tap
        gate = gate + cw[tap:tap + 1, :] * g[lo:lo + tm, :]
    o_ref[...] = (gate * jax.nn.sigmoid(gate) * val).astype(o_ref.dtype)


def _ffn_up(h, wg_stack, wv_stack, conv_w, conv_b, layer, seq):
    t, d = h.shape
    n = wg_stack.shape[-1]
    tm = min(1024, seq)
    bn = min(512, n)
    per_seq = seq // tm
    halo_blocks = tm // CONV_HALO
    return pl.pallas_call(
        functools.partial(_ffn_up_kernel, blocks_per_seq=per_seq),
        out_shape=jax.ShapeDtypeStruct((t, n), MXU_DTYPE),
        grid=(t // tm, n // bn),
        in_specs=[pl.BlockSpec((tm, d), lambda i, j: (i, 0)),
                  pl.BlockSpec((CONV_HALO, d), lambda i, j: (jnp.maximum(i * halo_blocks - 1, 0), 0)),
                  pl.BlockSpec((None, d, bn), lambda i, j: (layer, 0, j)),
                  pl.BlockSpec((None, d, bn), lambda i, j: (layer, 0, j)),
                  pl.BlockSpec((None, CONV_WIDTH, bn), lambda i, j: (layer, 0, j)),
                  pl.BlockSpec((None, 1, bn), lambda i, j: (layer, 0, j))],
        out_specs=pl.BlockSpec((tm, bn), lambda i, j: (i, j)),
        scratch_shapes=[pltpu.VMEM((tm + CONV_HALO, d), MXU_DTYPE)],
        compiler_params=_params(("parallel", "arbitrary")),
        name="ffn_up_conv",
    )(h, h, wg_stack, wv_stack, conv_w, conv_b)


def _pad_last(w, width):
    return jnp.pad(w, [(0, 0)] * (w.ndim - 1) + [(0, width - w.shape[-1])])


def _rotate_half_columns(w):
    half = ROPE_DIM // 2
    return jnp.concatenate([-w[..., half:], w[..., :half]], axis=-1)


def _pack_weights(w_in, mla_w_uq, mla_w_ukv, w_o, w_gate, w_val, conv_w, conv_b, w_down,
                  group_width, q_rank, kv_rank, n_heads, score_scale):
    gw = group_width
    a_end = 3 * gw
    b_end = a_end + 2 * gw
    cq_end = b_end + q_rank
    ckv_end = cq_end + kv_rank
    kr_end = ckv_end + ROPE_DIM
    d_end = kr_end + 3 * gw
    w_heads = jnp.concatenate(
        [w_in[..., :gw] * score_scale, w_in[..., gw:b_end],
         w_in[..., kr_end:kr_end + gw] * score_scale, w_in[..., kr_end + gw:d_end]], axis=-1).astype(MXU_DTYPE)
    kr = w_in[..., ckv_end:kr_end]
    w_latent = jnp.concatenate(
        [w_in[..., b_end:ckv_end], _pad_last(kr, LANE), _pad_last(_rotate_half_columns(kr), LANE),
         _pad_last(w_in[..., d_end:], LANE)], axis=-1).astype(MXU_DTYPE)

    layers, qr, _ = mla_w_uq.shape
    uq = mla_w_uq.reshape(layers, qr, n_heads, HEAD_DIM + ROPE_DIM)
    nope, rope = uq[..., :HEAD_DIM], uq[..., HEAD_DIM:]
    flat = lambda w: w.reshape(layers, w.shape[1], n_heads * HEAD_DIM)
    wq = jnp.concatenate([flat(nope), flat(_pad_last(rope, HEAD_DIM)),
                          flat(_pad_last(_rotate_half_columns(rope), HEAD_DIM))], axis=-1).astype(MXU_DTYPE)
    ukv = mla_w_ukv.reshape(layers, kv_rank, n_heads, 2 * HEAD_DIM)
    wkv = jnp.concatenate([flat(ukv[..., :HEAD_DIM]), flat(ukv[..., HEAD_DIM:])], axis=-1).astype(MXU_DTYPE)

    d_ff = w_gate.shape[-1]
    ff_pad = -(-d_ff // 1024) * 1024 if d_ff > 1024 else d_ff
    wg = _pad_last(w_gate, ff_pad).astype(MXU_DTYPE)
    wv = _pad_last(w_val, ff_pad).astype(MXU_DTYPE)
    wd = jnp.pad(w_down, ((0, 0), (0, ff_pad - d_ff), (0, 0))).astype(MXU_DTYPE)
    cw = _pad_last(conv_w, ff_pad)
    cb = _pad_last(conv_b, ff_pad)[:, None, :]
    return w_heads, w_latent, wq, wkv, w_o.astype(MXU_DTYPE), wg, wv, wd, cw, cb


def _rope_tables(seq):
    half = ROPE_DIM // 2
    inv_freq = ROPE_THETA ** (-jnp.arange(half, dtype=F32) / half)
    ang = jnp.arange(seq, dtype=F32)[:, None] * inv_freq[None, :]
    cos = jnp.concatenate([jnp.cos(ang), jnp.cos(ang)], axis=-1)
    sin = jnp.concatenate([jnp.sin(ang), jnp.sin(ang)], axis=-1)
    return _pad_last(cos, LANE), _pad_last(sin, LANE)


def kernel(x, norm_mix_g, w_in, sgu_ln_g, sgu_ln_b, sgu_w, sgu_b, mla_q_norm_g, mla_kv_norm_g,
           mla_w_uq, mla_w_ukv, fox_b_f, group_norm_g, w_o, norm_ffn_g, w_gate, w_val, conv_w,
           conv_b, w_down, final_norm_g):
    batch, seq, d_model = x.shape
    depth = w_in.shape[0]
    n_heads = fox_b_f.shape[-1]
    gw = group_norm_g.shape[-1]
    q_rank, kv_rank = mla_q_norm_g.shape[-1], mla_kv_norm_g.shape[-1]
    assert gw == n_heads * HEAD_DIM and seq % MOBA_BLOCK == 0
    assert q_rank % LANE == 0 and kv_rank % LANE == 0

    (w_heads, w_latent, wq, wkv, wo, wg, wv, wd, cw, cb) = _pack_weights(
        w_in, mla_w_uq, mla_w_ukv, w_o, w_gate, w_val, conv_w, conv_b, w_down, gw, q_rank, kv_rank,
        n_heads, HEAD_DIM ** -0.5 * LOG2E)
    cos, sin = _rope_tables(seq)
    slopes, moba_k_aux, moba_q_aux = _moba_constants(seq, n_heads)
    f_col_block = (q_rank + kv_rank + 2 * LANE) // LANE
    mla_scale = (HEAD_DIM + ROPE_DIM) ** -0.5 * LOG2E
    a_q, a_k, a_v, b_u, b_v, d_q, d_k, d_v = (i * n_heads for i in range(8))

    xf = x.reshape(batch * seq, d_model)
    h = _rmsnorm(xf, norm_mix_g[0], MXU_DTYPE)
    for l in range(depth):
        proj = _matmul_groups(h, w_heads, l, 1024, 1024)
        latent = _matmul_f32(h, w_latent, l, 512)
        qn, qr, kn, vv, kr = _mla_up(latent, mla_q_norm_g[l], mla_kv_norm_g[l], wq, wkv, cos, sin,
                                     l, n_heads, seq, mla_scale)
        fox_q_aux, fox_k_aux = _fox_gate(latent, fox_b_f[l], f_col_block, batch, seq, n_heads)
        y_a = _attention(proj, moba_q_aux, proj, moba_k_aux, proj, batch, seq, n_heads,
                         a_q, a_k, a_v, moba_slopes=slopes)
        y_b = _sgu(proj, sgu_ln_g[l], sgu_ln_b[l], sgu_w[l], sgu_b[l], n_heads, b_u, b_v)
        y_c = _attention(qn, qr, kn, kr, vv, batch, seq, n_heads, k2_shared=True)
        y_d = _attention(proj, fox_q_aux, proj, fox_k_aux, proj, batch, seq, n_heads, d_q, d_k, d_v)
        y = jnp.concatenate([y_a, y_b, y_c, y_d], axis=0)
        xf, h = _proj_residual_norm(y, wo, l, xf, norm_ffn_g[l], MXU_DTYPE, group_gain=group_norm_g[l])
        act = _ffn_up(h, wg, wv, cw, cb, l, seq)
        if l == depth - 1:
            h = _proj_residual_norm(act, wd, l, xf, final_norm_g, F32, keep_x=False)
        else:
            xf, h = _proj_residual_norm(act, wd, l, xf, norm_mix_g[l + 1], MXU_DTYPE)
    return h.reshape(batch, seq, d_model)
```

```python
import functools
import math

import jax
import jax.numpy as jnp
from jax import lax
from jax.experimental import pallas as pl
from jax.experimental.pallas import tpu as pltpu

F32 = jnp.float32
MXU_DTYPE = jnp.bfloat16

LANE = 128
HEAD_DIM = 128
MOBA_BLOCK = 256
MOBA_TOPK = 3
SGU_CHUNK = 128
ROPE_DIM = 64
ROPE_THETA = 10000.0
CONV_WIDTH = 3
NORM_EPS = 1e-6
NEG_INF = -1e30
LOG2E = math.log2(math.e)

ATTN_BLOCK = 256
MOBA_MAX_BLOCKS = 8
CONV_HALO = 16
PROJ_COL_CHUNK = 1024
NORM_ROW_CHUNK = 128
VMEM_LIMIT_BYTES = 56 * 1024 * 1024


def _params(semantics):
    return pltpu.CompilerParams(dimension_semantics=semantics, vmem_limit_bytes=VMEM_LIMIT_BYTES)


def _rmsnorm_rows(x, g):
    ms = jnp.mean(x * x, axis=-1, keepdims=True)
    return x * lax.rsqrt(ms + NORM_EPS) * g


def _split3(x):
    hi = x.astype(MXU_DTYPE).astype(F32)
    r = x - hi
    mid = r.astype(MXU_DTYPE).astype(F32)
    lo = (r - mid).astype(MXU_DTYPE).astype(F32)
    return hi, mid, lo


def _norm_kernel(x_ref, g_ref, o_ref):
    o_ref[...] = _rmsnorm_rows(x_ref[...], g_ref[...]).astype(o_ref.dtype)


def _rmsnorm(x, g, out_dtype):
    t, d = x.shape
    tm = min(256, t)
    return pl.pallas_call(
        _norm_kernel,
        out_shape=jax.ShapeDtypeStruct((t, d), out_dtype),
        grid=(t // tm,),
        in_specs=[pl.BlockSpec((tm, d), lambda i: (i, 0)),
                  pl.BlockSpec((1, d), lambda i: (0, 0))],
        out_specs=pl.BlockSpec((tm, d), lambda i: (i, 0)),
        compiler_params=_params(("parallel",)),
        name="rmsnorm",
    )(x, g.reshape(1, d))


def _mm_groups_kernel(a_ref, w_ref, o_ref):
    acc = jnp.dot(a_ref[...], w_ref[...], preferred_element_type=F32)
    for g in range(o_ref.shape[0]):
        o_ref[g] = acc[:, g * LANE:(g + 1) * LANE].astype(o_ref.dtype)


def _matmul_groups(a, w_stack, layer, tm, bn):
    t, k = a.shape
    n = w_stack.shape[-1]
    tm, bn = min(tm, t), min(bn, n)
    return pl.pallas_call(
        _mm_groups_kernel,
        out_shape=jax.ShapeDtypeStruct((n // LANE, t, LANE), MXU_DTYPE),
        grid=(t // tm, n // bn),
        in_specs=[pl.BlockSpec((tm, k), lambda i, j: (i, 0)),
                  pl.BlockSpec((None, k, bn), lambda i, j: (layer, 0, j))],
        out_specs=pl.BlockSpec((bn // LANE, tm, LANE), lambda i, j: (j, i, 0)),
        compiler_params=_params(("parallel", "arbitrary")),
        name="in_proj_heads",
    )(a, w_stack)


def _mm_plain_kernel(a_ref, w_ref, o_ref):
    o_ref[...] = jnp.dot(a_ref[...], w_ref[...], preferred_element_type=F32).astype(o_ref.dtype)


def _matmul_f32(a, w_stack, layer, tm):
    t, k = a.shape
    n = w_stack.shape[-1]
    tm = min(tm, t)
    return pl.pallas_call(
        _mm_plain_kernel,
        out_shape=jax.ShapeDtypeStruct((t, n), F32),
        grid=(t // tm,),
        in_specs=[pl.BlockSpec((tm, k), lambda i: (i, 0)),
                  pl.BlockSpec((None, k, n), lambda i: (layer, 0, 0))],
        out_specs=pl.BlockSpec((tm, n), lambda i: (i, 0)),
        compiler_params=_params(("parallel",)),
        name="in_proj_latent",
    )(a, w_stack)


def _mla_up_kernel(c_ref, gq_ref, gkv_ref, wq_ref, wkv_ref, cos_ref, sin_ref,
                   qn_ref, qr_ref, kn_ref, v_ref, kr_ref, *, q_rank, kv_rank, scale):
    n_heads = qn_ref.shape[0]
    hw = n_heads * HEAD_DIM
    c = c_ref[...]
    cq = _rmsnorm_rows(c[:, :q_rank], gq_ref[...]).astype(MXU_DTYPE)
    ckv = _rmsnorm_rows(c[:, q_rank:q_rank + kv_rank], gkv_ref[...]).astype(MXU_DTYPE)
    q = jnp.dot(cq, wq_ref[...], preferred_element_type=F32)
    kv = jnp.dot(ckv, wkv_ref[...], preferred_element_type=F32)
    cos, sin = cos_ref[...], sin_ref[...]
    for h in range(n_heads):
        lo, hi = h * HEAD_DIM, (h + 1) * HEAD_DIM
        qn_ref[h] = (q[:, lo:hi] * scale).astype(qn_ref.dtype)
        roped = q[:, hw + lo:hw + hi] * cos + q[:, 2 * hw + lo:2 * hw + hi] * sin
        qr_ref[h] = (roped * scale).astype(qr_ref.dtype)
        kn_ref[h] = kv[:, lo:hi].astype(kn_ref.dtype)
        v_ref[h] = kv[:, hw + lo:hw + hi].astype(v_ref.dtype)
    o = q_rank + kv_rank
    kr_ref[...] = (c[:, o:o + LANE] * cos + c[:, o + LANE:o + 2 * LANE] * sin).astype(kr_ref.dtype)


def _mla_up(c, gq, gkv, wq_stack, wkv_stack, cos, sin, layer, n_heads, seq, scale):
    t, cw = c.shape
    q_rank, kv_rank = gq.shape[-1], gkv.shape[-1]
    tm = min(512, seq)
    per_seq = seq // tm
    heads = jax.ShapeDtypeStruct((n_heads, t, LANE), MXU_DTYPE)
    head_spec = pl.BlockSpec((n_heads, tm, LANE), lambda i: (0, i, 0))
    return pl.pallas_call(
        functools.partial(_mla_up_kernel, q_rank=q_rank, kv_rank=kv_rank, scale=scale),
        out_shape=(heads, heads, heads, heads, jax.ShapeDtypeStruct((t, LANE), MXU_DTYPE)),
        grid=(t // tm,),
        in_specs=[pl.BlockSpec((tm, cw), lambda i: (i, 0)),
                  pl.BlockSpec((1, q_rank), lambda i: (0, 0)),
                  pl.BlockSpec((1, kv_rank), lambda i: (0, 0)),
                  pl.BlockSpec((None,) + wq_stack.shape[1:], lambda i: (layer, 0, 0)),
                  pl.BlockSpec((None,) + wkv_stack.shape[1:], lambda i: (layer, 0, 0)),
                  pl.BlockSpec((tm, LANE), lambda i: (i % per_seq, 0)),
                  pl.BlockSpec((tm, LANE), lambda i: (i % per_seq, 0))],
        out_specs=(head_spec, head_spec, head_spec, head_spec,
                   pl.BlockSpec((tm, LANE), lambda i: (i, 0))),
        compiler_params=_params(("parallel",)),
        name="mla_up",
    )(c, gq.reshape(1, q_rank), gkv.reshape(1, kv_rank), wq_stack, wkv_stack, cos, sin)


def _fox_gate_kernel(f_ref, b_ref, qa_ref, ka_ref):
    n_heads = qa_ref.shape[0]
    z = f_ref[...] + b_ref[...]
    x = -(jnp.maximum(-z, 0.0) + jnp.log1p(jnp.exp(-jnp.abs(z))))
    s = z.shape[0]
    rows = lax.broadcasted_iota(jnp.int32, z.shape, 0)
    shift = 1
    while shift < s:
        x = x + jnp.where(rows >= shift, pltpu.roll(x, shift, axis=0), 0.0)
        shift *= 2
    x = x * LOG2E
    lane = lax.broadcasted_iota(jnp.int32, z.shape, 1)
    ones = jnp.where(lane < 3, 1.0, 0.0)
    for h in range(n_heads):
        hi, mid, lo = _split3(jnp.broadcast_to(x[:, h:h + 1], x.shape))
        pieces = jnp.where(lane == 0, hi, jnp.where(lane == 1, mid, jnp.where(lane == 2, lo, 0.0)))
        qa_ref[h] = (pieces + pltpu.roll(ones, 3, axis=1)).astype(qa_ref.dtype)
        ka_ref[h] = (ones - pltpu.roll(pieces, 3, axis=1)).astype(ka_ref.dtype)


def _fox_gate(c, b_f, f_col_block, batch, seq, n_heads):
    b_pad = jnp.zeros((1, LANE), F32).at[0, :n_heads].set(b_f)
    aux = jax.ShapeDtypeStruct((n_heads, batch * seq, LANE), MXU_DTYPE)
    aux_spec = pl.BlockSpec((n_heads, seq, LANE), lambda b: (0, b, 0))
    return pl.pallas_call(
        _fox_gate_kernel,
        out_shape=(aux, aux),
        grid=(batch,),
        in_specs=[pl.BlockSpec((seq, LANE), lambda b: (b, f_col_block)),
                  pl.BlockSpec((1, LANE), lambda b: (0, 0))],
        out_specs=(aux_spec, aux_spec),
        compiler_params=_params(("parallel",)),
        name="fox_gate",
    )(c, b_pad)


def _moba_query_aux(q, kmean_ref, qconst, slope2, qi, blk):
    g = lax.dot_general(kmean_ref[...], q.astype(F32), (((1,), (1,)), ((), ())),
                        precision=lax.Precision.HIGHEST, preferred_element_type=F32)
    row = lax.broadcasted_iota(jnp.int32, g.shape, 0)
    bid = row & (MOBA_MAX_BLOCKS - 1)
    rank = jnp.zeros(g.shape, F32)
    for n in range(qi):
        gn = g[n:n + 1, :]
        rank = rank + jnp.where((gn > g) | ((gn == g) & (n < bid)), 1.0, 0.0)
    past = bid < qi
    chosen = (rank < MOBA_TOPK) & past
    offset = (qi - bid).astype(F32) * (-slope2 * blk)
    term = jnp.where(chosen, offset, jnp.where(past, NEG_INF, 0.0))
    hi = term.astype(MXU_DTYPE).astype(F32)
    lo = jnp.where(chosen, term - hi, 0.0)
    aux_t = jnp.where(row < MOBA_MAX_BLOCKS, hi, lo)
    aux_t = jnp.concatenate([aux_t, jnp.zeros((LANE - aux_t.shape[0], blk), F32)], axis=0)
    return (aux_t.T + qconst).astype(MXU_DTYPE)


def _attention_kernel(*refs, moba):
    if moba:
        slopes_ref, q1_ref, k1_ref, v_ref, kc_ref, qc_ref, o_ref, kk_ref, vv_ref, kmean_ref = refs
        k2_ref = kc_ref
    else:
        q1_ref, q2_ref, k1_ref, k2_ref, v_ref, o_ref, kk_ref, vv_ref = refs
    seq = k1_ref.shape[0]
    tq = ATTN_BLOCK
    kk_ref[:, :HEAD_DIM] = k1_ref[...]
    kk_ref[:, HEAD_DIM:] = k2_ref[...]
    vv_ref[:, :HEAD_DIM] = v_ref[...]
    vv_ref[:, HEAD_DIM:] = jnp.ones((seq, HEAD_DIM), vv_ref.dtype)
    if moba:
        slope2 = slopes_ref[pl.program_id(1)] * LOG2E
        qconst = qc_ref[...]
        kmean_ref[...] = jnp.zeros_like(kmean_ref)
        for n in range(seq // tq):
            mean = jnp.mean(k1_ref[n * tq:(n + 1) * tq, :].astype(F32), axis=0, keepdims=True)
            kmean_ref[n:n + 1, :] = mean
            kmean_ref[MOBA_MAX_BLOCKS + n:MOBA_MAX_BLOCKS + n + 1, :] = mean
    r = lax.broadcasted_iota(jnp.int32, (tq, tq), 0)
    c = lax.broadcasted_iota(jnp.int32, (tq, tq), 1)
    for qi in range(seq // tq):
        rows = slice(qi * tq, (qi + 1) * tq)
        n = (qi + 1) * tq
        q1 = q1_ref[rows, :]
        q2 = _moba_query_aux(q1, kmean_ref, qconst, slope2, qi, tq) if moba else q2_ref[rows, :]
        s = lax.dot_general(jnp.concatenate([q1, q2], axis=1), kk_ref[:n, :],
                            (((1,), (1,)), ((), ())), preferred_element_type=F32)
        diag = jnp.where(c <= r, s[:, n - tq:], NEG_INF)
        s = diag if qi == 0 else jnp.concatenate([s[:, :n - tq], diag], axis=1)
        m = jnp.max(s, axis=-1, keepdims=True)
        p = jnp.exp2(s - m).astype(vv_ref.dtype)
        acc = jnp.dot(p, vv_ref[:n, :], preferred_element_type=F32)
        o_ref[rows, :] = (acc[:, :HEAD_DIM] / acc[:, HEAD_DIM:]).astype(o_ref.dtype)


def _attention(q1, q2, k1, k2, v, batch, seq, n_heads, q1_group=0, k1_group=0, v_group=0,
               k2_shared=False, moba_slopes=None):
    assert seq % ATTN_BLOCK == 0
    t = batch * seq
    moba = moba_slopes is not None
    head = lambda g: pl.BlockSpec((None, seq, HEAD_DIM), lambda b, h: (g + h, b, 0))
    if moba:
        assert seq // ATTN_BLOCK <= MOBA_MAX_BLOCKS and ATTN_BLOCK == MOBA_BLOCK
        in_specs = [pl.BlockSpec(memory_space=pltpu.SMEM), head(q1_group), head(k1_group), head(v_group),
                    pl.BlockSpec((None, seq, LANE), lambda b, h: (h, 0, 0)),
                    pl.BlockSpec((None, ATTN_BLOCK, LANE), lambda b, h: (h, 0, 0))]
        args = (moba_slopes, q1, k1, v, k2, q2)
        scratch = [pltpu.VMEM((2 * MOBA_MAX_BLOCKS, HEAD_DIM), F32)]
    else:
        k2_spec = pl.BlockSpec((seq, LANE), lambda b, h: (b, 0)) if k2_shared else head(0)
        in_specs = [head(q1_group), head(0), head(k1_group), k2_spec, head(v_group)]
        args = (q1, q2, k1, k2, v)
        scratch = []
    return pl.pallas_call(
        functools.partial(_attention_kernel, moba=moba),
        out_shape=jax.ShapeDtypeStruct((n_heads, t, HEAD_DIM), MXU_DTYPE),
        grid=(batch, n_heads),
        in_specs=in_specs,
        out_specs=head(0),
        scratch_shapes=[pltpu.VMEM((seq, 2 * HEAD_DIM), MXU_DTYPE),
                        pltpu.VMEM((seq, 2 * HEAD_DIM), MXU_DTYPE)] + scratch,
        compiler_params=_params(("parallel", "parallel")),
        name="moba_attention" if moba else "causal_attention",
    )(*args)


def _moba_constants(seq, n_heads):
    slopes = jnp.exp2(-8.0 * jnp.arange(1, n_heads + 1, dtype=F32) / n_heads)
    pieces = jnp.stack(_split3(slopes * LOG2E), axis=-1)
    pos = jnp.arange(seq)
    onehot = (pos[:, None] // MOBA_BLOCK == jnp.arange(MOBA_MAX_BLOCKS)[None, :]).astype(F32)
    within = jnp.broadcast_to((pos % MOBA_BLOCK).astype(F32)[None, :, None], (n_heads, seq, 3))
    k_aux = jnp.concatenate([jnp.broadcast_to(onehot, (n_heads, seq, MOBA_MAX_BLOCKS))] * 2
                            + [jnp.broadcast_to(-pieces[:, None, :], (n_heads, seq, 3)), within], axis=-1)
    q_aux = jnp.concatenate([jnp.zeros((n_heads, MOBA_BLOCK, 2 * MOBA_MAX_BLOCKS), F32),
                             within[:, :MOBA_BLOCK], jnp.broadcast_to(pieces[:, None, :], (n_heads, MOBA_BLOCK, 3))],
                            axis=-1)
    return slopes, _pad_last(k_aux, LANE).astype(MXU_DTYPE), _pad_last(q_aux, LANE)


def _sgu_kernel(u_ref, v_ref, lg_ref, lb_ref, w_ref, b_ref, o_ref):
    tm = u_ref.shape[0]
    nc = tm // SGU_CHUNK
    v = jax.nn.gelu(v_ref[...].astype(F32))
    vc = v - jnp.mean(v, axis=-1, keepdims=True)
    var = jnp.mean(vc * vc, axis=-1, keepdims=True)
    vn = (vc * lax.rsqrt(var + NORM_EPS) * lg_ref[...] + lb_ref[...]).astype(MXU_DTYPE)
    v_side = jnp.concatenate([vn[c * SGU_CHUNK:(c + 1) * SGU_CHUNK, :] for c in range(nc)], axis=1)
    r = lax.broadcasted_iota(jnp.int32, (SGU_CHUNK, SGU_CHUNK), 0)
    c_ix = lax.broadcasted_iota(jnp.int32, (SGU_CHUNK, SGU_CHUNK), 1)
    w = jnp.where(c_ix <= r, w_ref[...], 0.0).astype(MXU_DTYPE)
    mixed = jnp.dot(w, v_side, preferred_element_type=F32) + b_ref[...]
    u = jax.nn.gelu(u_ref[...].astype(F32))
    for c in range(nc):
        rows = slice(c * SGU_CHUNK, (c + 1) * SGU_CHUNK)
        o_ref[rows, :] = (u[rows, :] * mixed[:, c * HEAD_DIM:(c + 1) * HEAD_DIM]).astype(o_ref.dtype)


def _sgu(proj, ln_g, ln_b, w_s, b_s, n_heads, u_group, v_group):
    t = proj.shape[1]
    tm = min(1024, t)
    vec = pl.BlockSpec((None, 1, HEAD_DIM), lambda g, i: (g, 0, 0))
    return pl.pallas_call(
        _sgu_kernel,
        out_shape=jax.ShapeDtypeStruct((n_heads, t, HEAD_DIM), MXU_DTYPE),
        grid=(n_heads, t // tm),
        in_specs=[pl.BlockSpec((None, tm, HEAD_DIM), lambda g, i: (u_group + g, i, 0)),
                  pl.BlockSpec((None, tm, HEAD_DIM), lambda g, i: (v_group + g, i, 0)),
                  vec, vec,
                  pl.BlockSpec((None, SGU_CHUNK, SGU_CHUNK), lambda g, i: (g, 0, 0)),
                  pl.BlockSpec((None, SGU_CHUNK, 1), lambda g, i: (g, 0, 0))],
        out_specs=pl.BlockSpec((None, tm, HEAD_DIM), lambda g, i: (g, i, 0)),
        compiler_params=_params(("parallel", "parallel")),
        name="spatial_gating",
    )(proj, proj, ln_g[:, None, :], ln_b[:, None, :], w_s, b_s[:, :, None])


def _out_proj_kernel(*refs, n_mix):
    y_refs, (gn_ref, w_ref, resid_ref, o_ref, a_ref) = refs[:n_mix], refs[n_mix:]
    per = y_refs[0].shape[0]
    width = per * LANE
    tm = a_ref.shape[0]

    @pl.when(pl.program_id(1) == 0)
    def _():
        for m, y_ref in enumerate(y_refs):
            for r0 in range(0, tm, NORM_ROW_CHUNK):
                rows = slice(r0, min(r0 + NORM_ROW_CHUNK, tm))
                parts = [y_ref[i, rows, :].astype(F32) for i in range(per)]
                ssq = sum(jnp.sum(p * p, axis=-1, keepdims=True) for p in parts)
                rstd = lax.rsqrt(ssq / width + NORM_EPS)
                for i, p in enumerate(parts):
                    lo = m * width + i * LANE
                    a_ref[rows, lo:lo + LANE] = (p * rstd * gn_ref[m:m + 1, i * LANE:(i + 1) * LANE]).astype(a_ref.dtype)

    o_ref[...] = resid_ref[...] + jnp.dot(a_ref[...], w_ref[...], preferred_element_type=F32)


def _out_proj(ys, group_gain, w_stack, layer, resid):
    t, n = resid.shape
    n_mix, width = group_gain.shape
    per = width // LANE
    tm, bn = min(1024, t), min(512, n)
    y_spec = pl.BlockSpec((per, tm, LANE), lambda i, j: (0, i, 0))
    return pl.pallas_call(
        functools.partial(_out_proj_kernel, n_mix=n_mix),
        out_shape=jax.ShapeDtypeStruct((t, n), F32),
        grid=(t // tm, n // bn),
        in_specs=[y_spec] * n_mix + [pl.BlockSpec((n_mix, width), lambda i, j: (0, 0)),
                                     pl.BlockSpec((None, n_mix * width, bn), lambda i, j: (layer, 0, j)),
                                     pl.BlockSpec((tm, bn), lambda i, j: (i, j))],
        out_specs=pl.BlockSpec((tm, bn), lambda i, j: (i, j)),
        scratch_shapes=[pltpu.VMEM((tm, n_mix * width), MXU_DTYPE)],
        compiler_params=_params(("parallel", "arbitrary")),
        name="out_proj",
    )(*ys, group_gain, w_stack, resid)


def _down_proj_norm_kernel(*refs, keep_x, k_total):
    if not keep_x:
        refs = refs + (refs[-1],)
    a_ref, w_ref, resid_ref, g_ref, ox_ref, oh_ref = refs
    k = pl.program_id(1)
    last = pl.num_programs(1) - 1
    tm, n = ox_ref.shape
    bk = a_ref.shape[1]
    k_rem = k_total % bk

    @pl.when(k == 0)
    def _():
        ox_ref[...] = resid_ref[...]

    def accumulate(depth):
        for c0 in range(0, n, PROJ_COL_CHUNK):
            cols = slice(c0, min(c0 + PROJ_COL_CHUNK, n))
            ox_ref[:, cols] += jnp.dot(a_ref[:, :depth], w_ref[:depth, cols], preferred_element_type=F32)

    if k_rem:
        pl.when(k < last)(lambda: accumulate(bk))
        pl.when(k == last)(lambda: accumulate(k_rem))
    else:
        accumulate(bk)

    @pl.when(k == last)
    def _():
        for r0 in range(0, tm, NORM_ROW_CHUNK):
            rows = slice(r0, min(r0 + NORM_ROW_CHUNK, tm))
            oh_ref[rows, :] = _rmsnorm_rows(ox_ref[rows, :], g_ref[...]).astype(oh_ref.dtype)


def _down_proj_norm(a, w_stack, layer, resid, g, h_dtype, keep_x=True):
    t, n = resid.shape
    k_total = a.shape[1]
    tm = min(512, t)
    bk = min(1024, k_total)
    row_block = pl.BlockSpec((tm, n), lambda i, k: (i, 0))
    h_shape = jax.ShapeDtypeStruct((t, n), h_dtype)
    assert keep_x or h_dtype == F32
    out_shape = (jax.ShapeDtypeStruct((t, n), F32), h_shape) if keep_x else h_shape
    return pl.pallas_call(
        functools.partial(_down_proj_norm_kernel, keep_x=keep_x, k_total=k_total),
        out_shape=out_shape,
        grid=(t // tm, pl.cdiv(k_total, bk)),
        in_specs=[pl.BlockSpec((tm, bk), lambda i, k: (i, k)),
                  pl.BlockSpec((None, bk, n), lambda i, k: (layer, k, 0)),
                  pl.BlockSpec((tm, n), lambda i, k: (i, 0), pipeline_mode=pl.Buffered(1)),
                  pl.BlockSpec((1, n), lambda i, k: (0, 0))],
        out_specs=(row_block, row_block) if keep_x else row_block,
        compiler_params=_params(("parallel", "arbitrary")),
        name="down_proj_norm",
    )(a, w_stack, resid, g.reshape(1, n))


def _ffn_up_kernel(h_ref, halo_ref, wg_ref, wv_ref, cw_ref, cb_ref, o_ref, hext_ref, *, blocks_per_seq):
    i, j = pl.program_id(0), pl.program_id(1)
    tm = h_ref.shape[0]

    @pl.when(j == 0)
    def _():
        halo = halo_ref[...]
        first = i % blocks_per_seq == 0
        hext_ref[:CONV_HALO, :] = jnp.where(first, jnp.zeros_like(halo), halo)
        hext_ref[CONV_HALO:, :] = h_ref[...]

    g = jnp.dot(hext_ref[...], wg_ref[...], preferred_element_type=F32)
    val = jnp.dot(h_ref[...], wv_ref[...], preferred_element_type=F32)
    cw = cw_ref[...]
    gate = cb_ref[...]
    for tap in range(CONV_WIDTH):
        lo = CONV_HALO - (CONV_WIDTH - 1) + tap
        gate = gate + cw[tap:tap + 1, :] * g[lo:lo + tm, :]
    o_ref[...] = (gate * jax.nn.sigmoid(gate) * val).astype(o_ref.dtype)


def _ffn_up(h, wg_stack, wv_stack, conv_w, conv_b, layer, seq):
    t, d = h.shape
    n = wg_stack.shape[-1]
    tm = min(1024, seq)
    bn = min(512, n)
    per_seq = seq // tm
    halo_blocks = tm // CONV_HALO
    return pl.pallas_call(
        functools.partial(_ffn_up_kernel, blocks_per_seq=per_seq),
        out_shape=jax.ShapeDtypeStruct((t, n), MXU_DTYPE),
        grid=(t // tm, pl.cdiv(n, bn)),
        in_specs=[pl.BlockSpec((tm, d), lambda i, j: (i, 0)),
                  pl.BlockSpec((CONV_HALO, d), lambda i, j: (jnp.maximum(i * halo_blocks - 1, 0), 0)),
                  pl.BlockSpec((None, d, bn), lambda i, j: (layer, 0, j)),
                  pl.BlockSpec((None, d, bn), lambda i, j: (layer, 0, j)),
                  pl.BlockSpec((None, CONV_WIDTH, bn), lambda i, j: (layer, 0, j)),
                  pl.BlockSpec((None, 1, bn), lambda i, j: (layer, 0, j))],
        out_specs=pl.BlockSpec((tm, bn), lambda i, j: (i, j)),
        scratch_shapes=[pltpu.VMEM((tm + CONV_HALO, d), MXU_DTYPE)],
        compiler_params=_params(("parallel", "arbitrary")),
        name="ffn_up_conv",
    )(h, h, wg_stack, wv_stack, conv_w, conv_b)


def _pad_last(w, width):
    return jnp.pad(w, [(0, 0)] * (w.ndim - 1) + [(0, width - w.shape[-1])])


def _rotate_half_columns(w):
    half = ROPE_DIM // 2
    return jnp.concatenate([-w[..., half:], w[..., :half]], axis=-1)


def _pack_weights(w_in, mla_w_uq, mla_w_ukv, w_o, w_gate, w_val, conv_w, conv_b, w_down,
                  group_width, q_rank, kv_rank, n_heads, score_scale):
    gw = group_width
    a_end = 3 * gw
    b_end = a_end + 2 * gw
    cq_end = b_end + q_rank
    ckv_end = cq_end + kv_rank
    kr_end = ckv_end + ROPE_DIM
    d_end = kr_end + 3 * gw
    w_heads = jnp.concatenate(
        [w_in[..., :gw] * score_scale, w_in[..., gw:b_end],
         w_in[..., kr_end:kr_end + gw] * score_scale, w_in[..., kr_end + gw:d_end]], axis=-1).astype(MXU_DTYPE)
    kr = w_in[..., ckv_end:kr_end]
    w_latent = jnp.concatenate(
        [w_in[..., b_end:ckv_end], _pad_last(kr, LANE), _pad_last(_rotate_half_columns(kr), LANE),
         _pad_last(w_in[..., d_end:], LANE)], axis=-1).astype(MXU_DTYPE)

    layers, qr, _ = mla_w_uq.shape
    uq = mla_w_uq.reshape(layers, qr, n_heads, HEAD_DIM + ROPE_DIM)
    nope, rope = uq[..., :HEAD_DIM], uq[..., HEAD_DIM:]
    flat = lambda w: w.reshape(layers, w.shape[1], n_heads * HEAD_DIM)
    wq = jnp.concatenate([flat(nope), flat(_pad_last(rope, HEAD_DIM)),
                          flat(_pad_last(_rotate_half_columns(rope), HEAD_DIM))], axis=-1).astype(MXU_DTYPE)
    ukv = mla_w_ukv.reshape(layers, kv_rank, n_heads, 2 * HEAD_DIM)
    wkv = jnp.concatenate([flat(ukv[..., :HEAD_DIM]), flat(ukv[..., HEAD_DIM:])], axis=-1).astype(MXU_DTYPE)

    return (w_heads, w_latent, wq, wkv, w_o.astype(MXU_DTYPE), w_gate.astype(MXU_DTYPE),
            w_val.astype(MXU_DTYPE), w_down.astype(MXU_DTYPE), conv_b[:, None, :])


def _rope_tables(seq):
    half = ROPE_DIM // 2
    inv_freq = ROPE_THETA ** (-jnp.arange(half, dtype=F32) / half)
    ang = jnp.arange(seq, dtype=F32)[:, None] * inv_freq[None, :]
    cos = jnp.concatenate([jnp.cos(ang), jnp.cos(ang)], axis=-1)
    sin = jnp.concatenate([jnp.sin(ang), jnp.sin(ang)], axis=-1)
    return _pad_last(cos, LANE), _pad_last(sin, LANE)


def kernel(x, norm_mix_g, w_in, sgu_ln_g, sgu_ln_b, sgu_w, sgu_b, mla_q_norm_g, mla_kv_norm_g,
           mla_w_uq, mla_w_ukv, fox_b_f, group_norm_g, w_o, norm_ffn_g, w_gate, w_val, conv_w,
           conv_b, w_down, final_norm_g):
    batch, seq, d_model = x.shape
    depth = w_in.shape[0]
    n_heads = fox_b_f.shape[-1]
    gw = group_norm_g.shape[-1]
    q_rank, kv_rank = mla_q_norm_g.shape[-1], mla_kv_norm_g.shape[-1]
    assert gw == n_heads * HEAD_DIM and seq % MOBA_BLOCK == 0
    assert q_rank % LANE == 0 and kv_rank % LANE == 0

    (w_heads, w_latent, wq, wkv, wo, wg, wv, wd, cb) = _pack_weights(
        w_in, mla_w_uq, mla_w_ukv, w_o, w_gate, w_val, conv_w, conv_b, w_down, gw, q_rank, kv_rank,
        n_heads, HEAD_DIM ** -0.5 * LOG2E)
    cos, sin = _rope_tables(seq)
    slopes, moba_k_aux, moba_q_aux = _moba_constants(seq, n_heads)
    f_col_block = (q_rank + kv_rank + 2 * LANE) // LANE
    mla_scale = (HEAD_DIM + ROPE_DIM) ** -0.5 * LOG2E
    a_q, a_k, a_v, b_u, b_v, d_q, d_k, d_v = (i * n_heads for i in range(8))

    xf = x.reshape(batch * seq, d_model)
    h = _rmsnorm(xf, norm_mix_g[0], MXU_DTYPE)
    for l in range(depth):
        proj = _matmul_groups(h, w_heads, l, 1024, 1024)
        latent = _matmul_f32(h, w_latent, l, 512)
        qn, qr, kn, vv, kr = _mla_up(latent, mla_q_norm_g[l], mla_kv_norm_g[l], wq, wkv, cos, sin,
                                     l, n_heads, seq, mla_scale)
        fox_q_aux, fox_k_aux = _fox_gate(latent, fox_b_f[l], f_col_block, batch, seq, n_heads)
        y_a = _attention(proj, moba_q_aux, proj, moba_k_aux, proj, batch, seq, n_heads,
                         a_q, a_k, a_v, moba_slopes=slopes)
        y_b = _sgu(proj, sgu_ln_g[l], sgu_ln_b[l], sgu_w[l], sgu_b[l], n_heads, b_u, b_v)
        y_c = _attention(qn, qr, kn, kr, vv, batch, seq, n_heads, k2_shared=True)
        y_d = _attention(proj, fox_q_aux, proj, fox_k_aux, proj, batch, seq, n_heads, d_q, d_k, d_v)
        xf = _out_proj((y_a, y_b, y_c, y_d), group_norm_g[l], wo, l, xf)
        h = _rmsnorm(xf, norm_ffn_g[l], MXU_DTYPE)
        act = _ffn_up(h, wg, wv, conv_w, cb, l, seq)
        if l == depth - 1:
            h = _down_proj_norm(act, wd, l, xf, final_norm_g, F32, keep_x=False)
        else:
            xf, h = _down_proj_norm(act, wd, l, xf, norm_mix_g[l + 1], MXU_DTYPE)
    return h.reshape(batch, seq, d_model)
```

```python
import functools
import math

import jax
import jax.numpy as jnp
from jax import lax
from jax.experimental import pallas as pl
from jax.experimental.pallas import tpu as pltpu

F32 = jnp.float32
MXU_DTYPE = jnp.bfloat16

LANE = 128
HEAD_DIM = 128
MOBA_BLOCK = 256
MOBA_TOPK = 3
SGU_CHUNK = 128
ROPE_DIM = 64
ROPE_THETA = 10000.0
CONV_WIDTH = 3
NORM_EPS = 1e-6
NEG_INF = -1e30
LOG2E = math.log2(math.e)

ATTN_BLOCK = 256
MOBA_MAX_BLOCKS = 8
CONV_HALO = 16
PROJ_COL_CHUNK = 1024
NORM_ROW_CHUNK = 128
VMEM_LIMIT_BYTES = 56 * 1024 * 1024


def _params(semantics):
    return pltpu.CompilerParams(dimension_semantics=semantics, vmem_limit_bytes=VMEM_LIMIT_BYTES)


def _rmsnorm_rows(x, g):
    ms = jnp.mean(x * x, axis=-1, keepdims=True)
    return x * lax.rsqrt(ms + NORM_EPS) * g


def _split3(x):
    hi = x.astype(MXU_DTYPE).astype(F32)
    r = x - hi
    mid = r.astype(MXU_DTYPE).astype(F32)
    lo = (r - mid).astype(MXU_DTYPE).astype(F32)
    return hi, mid, lo


def _norm_kernel(x_ref, g_ref, o_ref):
    o_ref[...] = _rmsnorm_rows(x_ref[...], g_ref[...]).astype(o_ref.dtype)


def _rmsnorm(x, g, out_dtype):
    t, d = x.shape
    tm = min(256, t)
    return pl.pallas_call(
        _norm_kernel,
        out_shape=jax.ShapeDtypeStruct((t, d), out_dtype),
        grid=(t // tm,),
        in_specs=[pl.BlockSpec((tm, d), lambda i: (i, 0)),
                  pl.BlockSpec((1, d), lambda i: (0, 0))],
        out_specs=pl.BlockSpec((tm, d), lambda i: (i, 0)),
        compiler_params=_params(("parallel",)),
        name="rmsnorm",
    )(x, g.reshape(1, d))


def _dot_nt(a, w_t):
    return lax.dot_general(a, w_t, (((1,), (1,)), ((), ())), preferred_element_type=F32)


def _mm_groups_kernel(a_ref, w_ref, o_ref):
    acc = _dot_nt(a_ref[...], w_ref[...])
    for g in range(o_ref.shape[0]):
        o_ref[g] = acc[:, g * LANE:(g + 1) * LANE].astype(o_ref.dtype)


def _matmul_groups(a, w_stack, layer, tm, bn):
    t, k = a.shape
    n = w_stack.shape[1]
    tm, bn = min(tm, t), min(bn, n)
    return pl.pallas_call(
        _mm_groups_kernel,
        out_shape=jax.ShapeDtypeStruct((n // LANE, t, LANE), MXU_DTYPE),
        grid=(t // tm, n // bn),
        in_specs=[pl.BlockSpec((tm, k), lambda i, j: (i, 0)),
                  pl.BlockSpec((None, bn, k), lambda i, j: (layer, j, 0))],
        out_specs=pl.BlockSpec((bn // LANE, tm, LANE), lambda i, j: (j, i, 0)),
        compiler_params=_params(("parallel", "arbitrary")),
        name="in_proj_heads",
    )(a, w_stack)


def _mm_plain_kernel(a_ref, w_ref, o_ref):
    o_ref[...] = _dot_nt(a_ref[...], w_ref[...]).astype(o_ref.dtype)


def _matmul_f32(a, w_stack, layer, tm):
    t, k = a.shape
    n = w_stack.shape[1]
    tm = min(tm, t)
    return pl.pallas_call(
        _mm_plain_kernel,
        out_shape=jax.ShapeDtypeStruct((t, n), F32),
        grid=(t // tm,),
        in_specs=[pl.BlockSpec((tm, k), lambda i: (i, 0)),
                  pl.BlockSpec((None, n, k), lambda i: (layer, 0, 0))],
        out_specs=pl.BlockSpec((tm, n), lambda i: (i, 0)),
        compiler_params=_params(("parallel",)),
        name="in_proj_latent",
    )(a, w_stack)


def _mla_up_kernel(c_ref, gq_ref, gkv_ref, wq_ref, wkv_ref, cos_ref, sin_ref,
                   qn_ref, qr_ref, kn_ref, v_ref, kr_ref, *, q_rank, kv_rank, scale):
    n_heads = qn_ref.shape[0]
    hw = n_heads * HEAD_DIM
    c = c_ref[...]
    cq = _rmsnorm_rows(c[:, :q_rank], gq_ref[...]).astype(MXU_DTYPE)
    ckv = _rmsnorm_rows(c[:, q_rank:q_rank + kv_rank], gkv_ref[...]).astype(MXU_DTYPE)
    q = jnp.dot(cq, wq_ref[...], preferred_element_type=F32)
    kv = jnp.dot(ckv, wkv_ref[...], preferred_element_type=F32)
    cos, sin = cos_ref[...], sin_ref[...]
    for h in range(n_heads):
        lo, hi = h * HEAD_DIM, (h + 1) * HEAD_DIM
        qn_ref[h] = (q[:, lo:hi] * scale).astype(qn_ref.dtype)
        roped = q[:, hw + lo:hw + hi] * cos + q[:, 2 * hw + lo:2 * hw + hi] * sin
        qr_ref[h] = (roped * scale).astype(qr_ref.dtype)
        kn_ref[h] = kv[:, lo:hi].astype(kn_ref.dtype)
        v_ref[h] = kv[:, hw + lo:hw + hi].astype(v_ref.dtype)
    o = q_rank + kv_rank
    kr_ref[...] = (c[:, o:o + LANE] * cos + c[:, o + LANE:o + 2 * LANE] * sin).astype(kr_ref.dtype)


def _mla_up(c, gq, gkv, wq_stack, wkv_stack, cos, sin, layer, n_heads, seq, scale):
    t, cw = c.shape
    q_rank, kv_rank = gq.shape[-1], gkv.shape[-1]
    tm = min(512, seq)
    per_seq = seq // tm
    heads = jax.ShapeDtypeStruct((n_heads, t, LANE), MXU_DTYPE)
    head_spec = pl.BlockSpec((n_heads, tm, LANE), lambda i: (0, i, 0))
    return pl.pallas_call(
        functools.partial(_mla_up_kernel, q_rank=q_rank, kv_rank=kv_rank, scale=scale),
        out_shape=(heads, heads, heads, heads, jax.ShapeDtypeStruct((t, LANE), MXU_DTYPE)),
        grid=(t // tm,),
        in_specs=[pl.BlockSpec((tm, cw), lambda i: (i, 0)),
                  pl.BlockSpec((1, q_rank), lambda i: (0, 0)),
                  pl.BlockSpec((1, kv_rank), lambda i: (0, 0)),
                  pl.BlockSpec((None,) + wq_stack.shape[1:], lambda i: (layer, 0, 0)),
                  pl.BlockSpec((None,) + wkv_stack.shape[1:], lambda i: (layer, 0, 0)),
                  pl.BlockSpec((tm, LANE), lambda i: (i % per_seq, 0)),
                  pl.BlockSpec((tm, LANE), lambda i: (i % per_seq, 0))],
        out_specs=(head_spec, head_spec, head_spec, head_spec,
                   pl.BlockSpec((tm, LANE), lambda i: (i, 0))),
        compiler_params=_params(("parallel",)),
        name="mla_up",
    )(c, gq.reshape(1, q_rank), gkv.reshape(1, kv_rank), wq_stack, wkv_stack, cos, sin)


def _fox_gate_kernel(f_ref, b_ref, qa_ref, ka_ref):
    n_heads = qa_ref.shape[0]
    z = f_ref[...] + b_ref[...]
    x = -(jnp.maximum(-z, 0.0) + jnp.log1p(jnp.exp(-jnp.abs(z))))
    s = z.shape[0]
    rows = lax.broadcasted_iota(jnp.int32, z.shape, 0)
    shift = 1
    while shift < s:
        x = x + jnp.where(rows >= shift, pltpu.roll(x, shift, axis=0), 0.0)
        shift *= 2
    x = x * LOG2E
    lane = lax.broadcasted_iota(jnp.int32, z.shape, 1)
    ones = jnp.where(lane < 3, 1.0, 0.0)
    for h in range(n_heads):
        hi, mid, lo = _split3(jnp.broadcast_to(x[:, h:h + 1], x.shape))
        pieces = jnp.where(lane == 0, hi, jnp.where(lane == 1, mid, jnp.where(lane == 2, lo, 0.0)))
        qa_ref[h] = (pieces + pltpu.roll(ones, 3, axis=1)).astype(qa_ref.dtype)
        ka_ref[h] = (ones - pltpu.roll(pieces, 3, axis=1)).astype(ka_ref.dtype)


def _fox_gate(c, b_f, f_col_block, batch, seq, n_heads):
    b_pad = jnp.zeros((1, LANE), F32).at[0, :n_heads].set(b_f)
    aux = jax.ShapeDtypeStruct((n_heads, batch * seq, LANE), MXU_DTYPE)
    aux_spec = pl.BlockSpec((n_heads, seq, LANE), lambda b: (0, b, 0))
    return pl.pallas_call(
        _fox_gate_kernel,
        out_shape=(aux, aux),
        grid=(batch,),
        in_specs=[pl.BlockSpec((seq, LANE), lambda b: (b, f_col_block)),
                  pl.BlockSpec((1, LANE), lambda b: (0, 0))],
        out_specs=(aux_spec, aux_spec),
        compiler_params=_params(("parallel",)),
        name="fox_gate",
    )(c, b_pad)


def _moba_query_aux(q, kmean_ref, qconst, slope2, blk):
    seq = q.shape[0]
    g = lax.dot_general(kmean_ref[...], q.astype(F32), (((1,), (1,)), ((), ())),
                        precision=lax.Precision.HIGHEST, preferred_element_type=F32)
    row = lax.broadcasted_iota(jnp.int32, g.shape, 0)
    bid = row & (MOBA_MAX_BLOCKS - 1)
    qblk = lax.broadcasted_iota(jnp.int32, g.shape, 1) // blk
    rank = jnp.zeros(g.shape, F32)
    for n in range(seq // blk - 1):
        gn = g[n:n + 1, :]
        beats = ((gn > g) | ((gn == g) & (n < bid))) & (n < qblk)
        rank = rank + jnp.where(beats, 1.0, 0.0)
    past = bid < qblk
    chosen = (rank < MOBA_TOPK) & past
    offset = (qblk - bid).astype(F32) * (-slope2 * blk)
    term = jnp.where(chosen, offset, jnp.where(past, NEG_INF, 0.0))
    hi = term.astype(MXU_DTYPE).astype(F32)
    lo = jnp.where(chosen, term - hi, 0.0)
    aux_t = jnp.where(row < MOBA_MAX_BLOCKS, hi, lo)
    aux = jnp.concatenate([aux_t, jnp.zeros((LANE - aux_t.shape[0], seq), F32)], axis=0).T
    return [(aux[r0:r0 + blk, :] + qconst).astype(MXU_DTYPE) for r0 in range(0, seq, blk)]


def _attention_kernel(*refs, moba):
    if moba:
        slopes_ref, q1_ref, k1_ref, v_ref, kc_ref, qc_ref, o_ref, kk_ref, vv_ref, kmean_ref = refs
        k2_ref = kc_ref
    else:
        q1_ref, q2_ref, k1_ref, k2_ref, v_ref, o_ref, kk_ref, vv_ref = refs
    seq = k1_ref.shape[0]
    tq = ATTN_BLOCK
    kk_ref[:, :HEAD_DIM] = k1_ref[...]
    kk_ref[:, HEAD_DIM:] = k2_ref[...]
    vv_ref[:, :HEAD_DIM] = v_ref[...]
    vv_ref[:, HEAD_DIM:] = jnp.ones((seq, HEAD_DIM), vv_ref.dtype)
    if moba:
        slope2 = slopes_ref[pl.program_id(1)] * LOG2E
        qconst = qc_ref[...]
        kmean_ref[...] = jnp.zeros_like(kmean_ref)
        for n in range(seq // tq):
            mean = jnp.mean(k1_ref[n * tq:(n + 1) * tq, :].astype(F32), axis=0, keepdims=True)
            kmean_ref[n:n + 1, :] = mean
            kmean_ref[MOBA_MAX_BLOCKS + n:MOBA_MAX_BLOCKS + n + 1, :] = mean
        q2_tiles = _moba_query_aux(q1_ref[...], kmean_ref, qconst, slope2, tq)
    r = lax.broadcasted_iota(jnp.int32, (tq, tq), 0)
    c = lax.broadcasted_iota(jnp.int32, (tq, tq), 1)
    for qi in reversed(range(seq // tq)):
        rows = slice(qi * tq, (qi + 1) * tq)
        n = (qi + 1) * tq
        q1 = q1_ref[rows, :]
        q2 = q2_tiles[qi] if moba else q2_ref[rows, :]
        s = lax.dot_general(jnp.concatenate([q1, q2], axis=1), kk_ref[:n, :],
                            (((1,), (1,)), ((), ())), preferred_element_type=F32)
        diag = jnp.where(c <= r, s[:, n - tq:], NEG_INF)
        s = diag if qi == 0 else jnp.concatenate([s[:, :n - tq], diag], axis=1)
        m = jnp.max(s, axis=-1, keepdims=True)
        p = jnp.exp2(s - m).astype(vv_ref.dtype)
        acc = jnp.dot(p, vv_ref[:n, :], preferred_element_type=F32)
        o_ref[rows, :] = (acc[:, :HEAD_DIM] / acc[:, HEAD_DIM:]).astype(o_ref.dtype)


def _attention(q1, q2, k1, k2, v, batch, seq, n_heads, q1_group=0, k1_group=0, v_group=0,
               k2_shared=False, moba_slopes=None):
    assert seq % ATTN_BLOCK == 0
    t = batch * seq
    moba = moba_slopes is not None
    head = lambda g: pl.BlockSpec((None, seq, HEAD_DIM), lambda b, h: (g + h, b, 0))
    if moba:
        assert seq // ATTN_BLOCK <= MOBA_MAX_BLOCKS and ATTN_BLOCK == MOBA_BLOCK
        in_specs = [pl.BlockSpec(memory_space=pltpu.SMEM), head(q1_group), head(k1_group), head(v_group),
                    pl.BlockSpec((None, seq, LANE), lambda b, h: (h, 0, 0)),
                    pl.BlockSpec((None, ATTN_BLOCK, LANE), lambda b, h: (h, 0, 0))]
        args = (moba_slopes, q1, k1, v, k2, q2)
        scratch = [pltpu.VMEM((2 * MOBA_MAX_BLOCKS, HEAD_DIM), F32)]
    else:
        k2_spec = pl.BlockSpec((seq, LANE), lambda b, h: (b, 0)) if k2_shared else head(0)
        in_specs = [head(q1_group), head(0), head(k1_group), k2_spec, head(v_group)]
        args = (q1, q2, k1, k2, v)
        scratch = []
    return pl.pallas_call(
        functools.partial(_attention_kernel, moba=moba),
        out_shape=jax.ShapeDtypeStruct((n_heads, t, HEAD_DIM), MXU_DTYPE),
        grid=(batch, n_heads),
        in_specs=in_specs,
        out_specs=head(0),
        scratch_shapes=[pltpu.VMEM((seq, 2 * HEAD_DIM), MXU_DTYPE),
                        pltpu.VMEM((seq, 2 * HEAD_DIM), MXU_DTYPE)] + scratch,
        compiler_params=_params(("parallel", "parallel")),
        name="moba_attention" if moba else "causal_attention",
    )(*args)


def _moba_constants(seq, n_heads):
    slopes = jnp.exp2(-8.0 * jnp.arange(1, n_heads + 1, dtype=F32) / n_heads)
    pieces = jnp.stack(_split3(slopes * LOG2E), axis=-1)
    pos = jnp.arange(seq)
    onehot = (pos[:, None] // MOBA_BLOCK == jnp.arange(MOBA_MAX_BLOCKS)[None, :]).astype(F32)
    within = jnp.broadcast_to((pos % MOBA_BLOCK).astype(F32)[None, :, None], (n_heads, seq, 3))
    k_aux = jnp.concatenate([jnp.broadcast_to(onehot, (n_heads, seq, MOBA_MAX_BLOCKS))] * 2
                            + [jnp.broadcast_to(-pieces[:, None, :], (n_heads, seq, 3)), within], axis=-1)
    q_aux = jnp.concatenate([jnp.zeros((n_heads, MOBA_BLOCK, 2 * MOBA_MAX_BLOCKS), F32),
                             within[:, :MOBA_BLOCK], jnp.broadcast_to(pieces[:, None, :], (n_heads, MOBA_BLOCK, 3))],
                            axis=-1)
    return slopes, _pad_last(k_aux, LANE).astype(MXU_DTYPE), _pad_last(q_aux, LANE)


def _sgu_kernel(u_ref, v_ref, lg_ref, lb_ref, w_ref, b_ref, o_ref):
    tm = u_ref.shape[0]
    nc = tm // SGU_CHUNK
    v = jax.nn.gelu(v_ref[...].astype(F32))
    vc = v - jnp.mean(v, axis=-1, keepdims=True)
    var = jnp.mean(vc * vc, axis=-1, keepdims=True)
    vn = (vc * lax.rsqrt(var + NORM_EPS) * lg_ref[...] + lb_ref[...]).astype(MXU_DTYPE)
    v_side = jnp.concatenate([vn[c * SGU_CHUNK:(c + 1) * SGU_CHUNK, :] for c in range(nc)], axis=1)
    r = lax.broadcasted_iota(jnp.int32, (SGU_CHUNK, SGU_CHUNK), 0)
    c_ix = lax.broadcasted_iota(jnp.int32, (SGU_CHUNK, SGU_CHUNK), 1)
    w = jnp.where(c_ix <= r, w_ref[...], 0.0).astype(MXU_DTYPE)
    mixed = jnp.dot(w, v_side, preferred_element_type=F32) + b_ref[...]
    u = jax.nn.gelu(u_ref[...].astype(F32))
    for c in range(nc):
        rows = slice(c * SGU_CHUNK, (c + 1) * SGU_CHUNK)
        o_ref[rows, :] = (u[rows, :] * mixed[:, c * HEAD_DIM:(c + 1) * HEAD_DIM]).astype(o_ref.dtype)


def _sgu(proj, ln_g, ln_b, w_s, b_s, n_heads, u_group, v_group):
    t = proj.shape[1]
    tm = min(1024, t)
    vec = pl.BlockSpec((None, 1, HEAD_DIM), lambda g, i: (g, 0, 0))
    return pl.pallas_call(
        _sgu_kernel,
        out_shape=jax.ShapeDtypeStruct((n_heads, t, HEAD_DIM), MXU_DTYPE),
        grid=(n_heads, t // tm),
        in_specs=[pl.BlockSpec((None, tm, HEAD_DIM), lambda g, i: (u_group + g, i, 0)),
                  pl.BlockSpec((None, tm, HEAD_DIM), lambda g, i: (v_group + g, i, 0)),
                  vec, vec,
                  pl.BlockSpec((None, SGU_CHUNK, SGU_CHUNK), lambda g, i: (g, 0, 0)),
                  pl.BlockSpec((None, SGU_CHUNK, 1), lambda g, i: (g, 0, 0))],
        out_specs=pl.BlockSpec((None, tm, HEAD_DIM), lambda g, i: (g, i, 0)),
        compiler_params=_params(("parallel", "parallel")),
        name="spatial_gating",
    )(proj, proj, ln_g[:, None, :], ln_b[:, None, :], w_s, b_s[:, :, None])


def _out_proj_kernel(*refs, n_mix):
    y_refs, (gn_ref, w_ref, resid_ref, o_ref, a_ref) = refs[:n_mix], refs[n_mix:]
    per = y_refs[0].shape[0]
    width = per * LANE
    tm = a_ref.shape[0]

    @pl.when(pl.program_id(1) == 0)
    def _():
        for m, y_ref in enumerate(y_refs):
            for r0 in range(0, tm, NORM_ROW_CHUNK):
                rows = slice(r0, min(r0 + NORM_ROW_CHUNK, tm))
                parts = [y_ref[i, rows, :].astype(F32) for i in range(per)]
                ssq = sum(jnp.sum(p * p, axis=-1, keepdims=True) for p in parts)
                rstd = lax.rsqrt(ssq / width + NORM_EPS)
                for i, p in enumerate(parts):
                    lo = m * width + i * LANE
                    a_ref[rows, lo:lo + LANE] = (p * rstd * gn_ref[m:m + 1, i * LANE:(i + 1) * LANE]).astype(a_ref.dtype)

    o_ref[...] = resid_ref[...] + jnp.dot(a_ref[...], w_ref[...], preferred_element_type=F32)


def _out_proj(ys, group_gain, w_stack, layer, resid):
    t, n = resid.shape
    n_mix, width = group_gain.shape
    per = width // LANE
    tm, bn = min(1024, t), min(512, n)
    y_spec = pl.BlockSpec((per, tm, LANE), lambda i, j: (0, i, 0))
    return pl.pallas_call(
        functools.partial(_out_proj_kernel, n_mix=n_mix),
        out_shape=jax.ShapeDtypeStruct((t, n), F32),
        grid=(t // tm, n // bn),
        in_specs=[y_spec] * n_mix + [pl.BlockSpec((n_mix, width), lambda i, j: (0, 0)),
                                     pl.BlockSpec((None, n_mix * width, bn), lambda i, j: (layer, 0, j)),
                                     pl.BlockSpec((tm, bn), lambda i, j: (i, j))],
        out_specs=pl.BlockSpec((tm, bn), lambda i, j: (i, j)),
        scratch_shapes=[pltpu.VMEM((tm, n_mix * width), MXU_DTYPE)],
        compiler_params=_params(("parallel", "arbitrary")),
        name="out_proj",
    )(*ys, group_gain, w_stack, resid)


def _down_proj_norm_kernel(*refs, keep_x, k_total):
    if not keep_x:
        refs = refs + (refs[-1],)
    a_ref, w_ref, resid_ref, g_ref, ox_ref, oh_ref = refs
    k = pl.program_id(1)
    last = pl.num_programs(1) - 1
    tm, n = ox_ref.shape
    bk = a_ref.shape[1]
    k_rem = k_total % bk

    @pl.when(k == 0)
    def _():
        ox_ref[...] = resid_ref[...]

    def accumulate(depth):
        for c0 in range(0, n, PROJ_COL_CHUNK):
            cols = slice(c0, min(c0 + PROJ_COL_CHUNK, n))
            ox_ref[:, cols] += jnp.dot(a_ref[:, :depth], w_ref[:depth, cols], preferred_element_type=F32)

    if k_rem:
        pl.when(k < last)(lambda: accumulate(bk))
        pl.when(k == last)(lambda: accumulate(k_rem))
    else:
        accumulate(bk)

    @pl.when(k == last)
    def _():
        for r0 in range(0, tm, NORM_ROW_CHUNK):
            rows = slice(r0, min(r0 + NORM_ROW_CHUNK, tm))
            oh_ref[rows, :] = _rmsnorm_rows(ox_ref[rows, :], g_ref[...]).astype(oh_ref.dtype)


def _down_proj_norm(a, w_stack, layer, resid, g, h_dtype, keep_x=True):
    t, n = resid.shape
    k_total = a.shape[1]
    tm = min(512, t)
    bk = min(1024, k_total)
    row_block = pl.BlockSpec((tm, n), lambda i, k: (i, 0))
    h_shape = jax.ShapeDtypeStruct((t, n), h_dtype)
    assert keep_x or h_dtype == F32
    out_shape = (jax.ShapeDtypeStruct((t, n), F32), h_shape) if keep_x else h_shape
    return pl.pallas_call(
        functools.partial(_down_proj_norm_kernel, keep_x=keep_x, k_total=k_total),
        out_shape=out_shape,
        grid=(t // tm, pl.cdiv(k_total, bk)),
        in_specs=[pl.BlockSpec((tm, bk), lambda i, k: (i, k)),
                  pl.BlockSpec((None, bk, n), lambda i, k: (layer, k, 0)),
                  pl.BlockSpec((tm, n), lambda i, k: (i, 0), pipeline_mode=pl.Buffered(1)),
                  pl.BlockSpec((1, n), lambda i, k: (0, 0))],
        out_specs=(row_block, row_block) if keep_x else row_block,
        compiler_params=_params(("parallel", "arbitrary")),
        name="down_proj_norm",
    )(a, w_stack, resid, g.reshape(1, n))


def _ffn_up_kernel(h_ref, halo_ref, wg_ref, wv_ref, cw_ref, cb_ref, o_ref, hext_ref, *, blocks_per_seq):
    i, j = pl.program_id(0), pl.program_id(1)
    tm = h_ref.shape[0]

    @pl.when(j == 0)
    def _():
        halo = halo_ref[...]
        first = i % blocks_per_seq == 0
        hext_ref[:CONV_HALO, :] = jnp.where(first, jnp.zeros_like(halo), halo)
        hext_ref[CONV_HALO:, :] = h_ref[...]

    g = jnp.dot(hext_ref[...], wg_ref[...], preferred_element_type=F32)
    val = jnp.dot(h_ref[...], wv_ref[...], preferred_element_type=F32)
    cw = cw_ref[...]
    gate = cb_ref[...]
    for tap in range(CONV_WIDTH):
        lo = CONV_HALO - (CONV_WIDTH - 1) + tap
        gate = gate + cw[tap:tap + 1, :] * g[lo:lo + tm, :]
    o_ref[...] = (gate * jax.nn.sigmoid(gate) * val).astype(o_ref.dtype)


def _ffn_up(h, wg_stack, wv_stack, conv_w, conv_b, layer, seq):
    t, d = h.shape
    n = wg_stack.shape[-1]
    tm = min(1024, seq)
    bn = min(512, n)
    per_seq = seq // tm
    halo_blocks = tm // CONV_HALO
    return pl.pallas_call(
        functools.partial(_ffn_up_kernel, blocks_per_seq=per_seq),
        out_shape=jax.ShapeDtypeStruct((t, n), MXU_DTYPE),
        grid=(t // tm, pl.cdiv(n, bn)),
        in_specs=[pl.BlockSpec((tm, d), lambda i, j: (i, 0)),
                  pl.BlockSpec((CONV_HALO, d), lambda i, j: (jnp.maximum(i * halo_blocks - 1, 0), 0)),
                  pl.BlockSpec((None, d, bn), lambda i, j: (layer, 0, j)),
                  pl.BlockSpec((None, d, bn), lambda i, j: (layer, 0, j)),
                  pl.BlockSpec((None, CONV_WIDTH, bn), lambda i, j: (layer, 0, j)),
                  pl.BlockSpec((None, 1, bn), lambda i, j: (layer, 0, j))],
        out_specs=pl.BlockSpec((tm, bn), lambda i, j: (i, j)),
        scratch_shapes=[pltpu.VMEM((tm + CONV_HALO, d), MXU_DTYPE)],
        compiler_params=_params(("parallel", "arbitrary")),
        name="ffn_up_conv",
    )(h, h, wg_stack, wv_stack, conv_w, conv_b)


def _pad_last(w, width):
    return jnp.pad(w, [(0, 0)] * (w.ndim - 1) + [(0, width - w.shape[-1])])


def _rotate_half_columns(w):
    half = ROPE_DIM // 2
    return jnp.concatenate([-w[..., half:], w[..., :half]], axis=-1)


def _pack_weights(w_in, mla_w_uq, mla_w_ukv, w_o, w_gate, w_val, conv_w, conv_b, w_down,
                  group_width, q_rank, kv_rank, n_heads, score_scale):
    gw = group_width
    a_end = 3 * gw
    b_end = a_end + 2 * gw
    cq_end = b_end + q_rank
    ckv_end = cq_end + kv_rank
    kr_end = ckv_end + ROPE_DIM
    d_end = kr_end + 3 * gw
    wt = jnp.swapaxes(w_in, 1, 2)
    w_heads = jnp.concatenate(
        [wt[:, :gw] * score_scale, wt[:, gw:b_end],
         wt[:, kr_end:kr_end + gw] * score_scale, wt[:, kr_end + gw:d_end]], axis=1).astype(MXU_DTYPE)
    kr = w_in[..., ckv_end:kr_end]
    pad_t = lambda w: jnp.swapaxes(_pad_last(w, LANE), 1, 2)
    w_latent = jnp.concatenate(
        [wt[:, b_end:ckv_end], pad_t(kr), pad_t(_rotate_half_columns(kr)), pad_t(w_in[..., d_end:])],
        axis=1).astype(MXU_DTYPE)

    layers, qr, _ = mla_w_uq.shape
    uq = mla_w_uq.reshape(layers, qr, n_heads, HEAD_DIM + ROPE_DIM)
    nope, rope = uq[..., :HEAD_DIM], uq[..., HEAD_DIM:]
    flat = lambda w: w.reshape(layers, w.shape[1], n_heads * HEAD_DIM)
    wq = jnp.concatenate([flat(nope), flat(_pad_last(rope, HEAD_DIM)),
                          flat(_pad_last(_rotate_half_columns(rope), HEAD_DIM))], axis=-1).astype(MXU_DTYPE)
    ukv = mla_w_ukv.reshape(layers, kv_rank, n_heads, 2 * HEAD_DIM)
    wkv = jnp.concatenate([flat(ukv[..., :HEAD_DIM]), flat(ukv[..., HEAD_DIM:])], axis=-1).astype(MXU_DTYPE)

    return (w_heads, w_latent, wq, wkv, w_o.astype(MXU_DTYPE), w_gate.astype(MXU_DTYPE),
            w_val.astype(MXU_DTYPE), w_down.astype(MXU_DTYPE), conv_b[:, None, :])


def _rope_tables(seq):
    half = ROPE_DIM // 2
    inv_freq = ROPE_THETA ** (-jnp.arange(half, dtype=F32) / half)
    ang = jnp.arange(seq, dtype=F32)[:, None] * inv_freq[None, :]
    cos = jnp.concatenate([jnp.cos(ang), jnp.cos(ang)], axis=-1)
    sin = jnp.concatenate([jnp.sin(ang), jnp.sin(ang)], axis=-1)
    return _pad_last(cos, LANE), _pad_last(sin, LANE)


def kernel(x, norm_mix_g, w_in, sgu_ln_g, sgu_ln_b, sgu_w, sgu_b, mla_q_norm_g, mla_kv_norm_g,
           mla_w_uq, mla_w_ukv, fox_b_f, group_norm_g, w_o, norm_ffn_g, w_gate, w_val, conv_w,
           conv_b, w_down, final_norm_g):
    batch, seq, d_model = x.shape
    depth = w_in.shape[0]
    n_heads = fox_b_f.shape[-1]
    gw = group_norm_g.shape[-1]
    q_rank, kv_rank = mla_q_norm_g.shape[-1], mla_kv_norm_g.shape[-1]
    assert gw == n_heads * HEAD_DIM and seq % MOBA_BLOCK == 0
    assert q_rank % LANE == 0 and kv_rank % LANE == 0

    (w_heads, w_latent, wq, wkv, wo, wg, wv, wd, cb) = _pack_weights(
        w_in, mla_w_uq, mla_w_ukv, w_o, w_gate, w_val, conv_w, conv_b, w_down, gw, q_rank, kv_rank,
        n_heads, HEAD_DIM ** -0.5 * LOG2E)
    cos, sin = _rope_tables(seq)
    slopes, moba_k_aux, moba_q_aux = _moba_constants(seq, n_heads)
    f_col_block = (q_rank + kv_rank + 2 * LANE) // LANE
    mla_scale = (HEAD_DIM + ROPE_DIM) ** -0.5 * LOG2E
    a_q, a_k, a_v, b_u, b_v, d_q, d_k, d_v = (i * n_heads for i in range(8))

    xf = x.reshape(batch * seq, d_model)
    h = _rmsnorm(xf, norm_mix_g[0], MXU_DTYPE)
    for l in range(depth):
        proj = _matmul_groups(h, w_heads, l, 1024, 1024)
        latent = _matmul_f32(h, w_latent, l, 512)
        qn, qr, kn, vv, kr = _mla_up(latent, mla_q_norm_g[l], mla_kv_norm_g[l], wq, wkv, cos, sin,
                                     l, n_heads, seq, mla_scale)
        fox_q_aux, fox_k_aux = _fox_gate(latent, fox_b_f[l], f_col_block, batch, seq, n_heads)
        y_a = _attention(proj, moba_q_aux, proj, moba_k_aux, proj, batch, seq, n_heads,
                         a_q, a_k, a_v, moba_slopes=slopes)
        y_b = _sgu(proj, sgu_ln_g[l], sgu_ln_b[l], sgu_w[l], sgu_b[l], n_heads, b_u, b_v)
        y_c = _attention(qn, qr, kn, kr, vv, batch, seq, n_heads, k2_shared=True)
        y_d = _attention(proj, fox_q_aux, proj, fox_k_aux, proj, batch, seq, n_heads, d_q, d_k, d_v)
        xf = _out_proj((y_a, y_b, y_c, y_d), group_norm_g[l], wo, l, xf)
        h = _rmsnorm(xf, norm_ffn_g[l], MXU_DTYPE)
        act = _ffn_up(h, wg, wv, conv_w, cb, l, seq)
        if l == depth - 1:
            h = _down_proj_norm(act, wd, l, xf, final_norm_g, F32, keep_x=False)
        else:
            xf, h = _down_proj_norm(act, wd, l, xf, norm_mix_g[l + 1], MXU_DTYPE)
    return h.reshape(batch, seq, d_model)
```

```python
import functools
import math

import jax
import jax.numpy as jnp
from jax import lax
from jax.experimental import pallas as pl
from jax.experimental.pallas import tpu as pltpu

F32 = jnp.float32
MXU_DTYPE = jnp.bfloat16

LANE = 128
HEAD_DIM = 128
MOBA_BLOCK = 256
MOBA_TOPK = 3
SGU_CHUNK = 128
ROPE_DIM = 64
ROPE_THETA = 10000.0
CONV_WIDTH = 3
NORM_EPS = 1e-6
NEG_INF = -1e30
LOG2E = math.log2(math.e)

ATTN_BLOCK = 256
MOBA_MAX_BLOCKS = 8
CONV_HALO = 16
PROJ_COL_CHUNK = 1024
NORM_ROW_CHUNK = 128
VMEM_LIMIT_BYTES = 56 * 1024 * 1024


def _params(semantics):
    return pltpu.CompilerParams(dimension_semantics=semantics, vmem_limit_bytes=VMEM_LIMIT_BYTES)


def _rmsnorm_rows(x, g):
    ms = jnp.mean(x * x, axis=-1, keepdims=True)
    return x * lax.rsqrt(ms + NORM_EPS) * g


def _split3(x):
    hi = x.astype(MXU_DTYPE).astype(F32)
    r = x - hi
    mid = r.astype(MXU_DTYPE).astype(F32)
    lo = (r - mid).astype(MXU_DTYPE).astype(F32)
    return hi, mid, lo


def _norm_kernel(x_ref, g_ref, o_ref):
    o_ref[...] = _rmsnorm_rows(x_ref[...], g_ref[...]).astype(o_ref.dtype)


def _rmsnorm(x, g, out_dtype):
    t, d = x.shape
    tm = min(256, t)
    return pl.pallas_call(
        _norm_kernel,
        out_shape=jax.ShapeDtypeStruct((t, d), out_dtype),
        grid=(t // tm,),
        in_specs=[pl.BlockSpec((tm, d), lambda i: (i, 0)),
                  pl.BlockSpec((1, d), lambda i: (0, 0))],
        out_specs=pl.BlockSpec((tm, d), lambda i: (i, 0)),
        compiler_params=_params(("parallel",)),
        name="rmsnorm",
    )(x, g.reshape(1, d))


def _dot_nt(a, w_t):
    return lax.dot_general(a, w_t, (((1,), (1,)), ((), ())), preferred_element_type=F32)


def _mm_groups_kernel(a_ref, w_ref, o_ref, *, scaled_blocks, scale):
    acc = _dot_nt(a_ref[...], w_ref[...])
    j = pl.program_id(1)
    is_scaled = functools.reduce(jnp.logical_or, [j == b for b in scaled_blocks])
    acc = acc * jnp.where(is_scaled, scale, 1.0)
    for g in range(o_ref.shape[0]):
        o_ref[g] = acc[:, g * LANE:(g + 1) * LANE].astype(o_ref.dtype)


def _matmul_groups(a, w_t, tm, bn, segments, scaled_starts, scale):
    t, k = a.shape
    tm = min(tm, t)
    assert all(rows % bn == 0 for _, rows in segments) and all(s0 % bn == 0 for s0 in scaled_starts)
    n = sum(rows for _, rows in segments)

    def row_start(j):
        r = j * bn
        shift, done = 0, 0
        for r0, rows in segments:
            r = r + jnp.where(j >= done // bn, (r0 - done) - shift, 0)
            shift, done = r0 - done, done + rows
        return pl.multiple_of(r, math.gcd(bn, *(r0 for r0, _ in segments)))

    return pl.pallas_call(
        functools.partial(_mm_groups_kernel, scaled_blocks=[s0 // bn for s0 in scaled_starts], scale=scale),
        out_shape=jax.ShapeDtypeStruct((n // LANE, t, LANE), MXU_DTYPE),
        grid=(t // tm, n // bn),
        in_specs=[pl.BlockSpec((tm, k), lambda i, j: (i, 0)),
                  pl.BlockSpec((pl.Element(bn), pl.Element(k)), lambda i, j: (row_start(j), 0))],
        out_specs=pl.BlockSpec((bn // LANE, tm, LANE), lambda i, j: (j, i, 0)),
        compiler_params=_params(("parallel", "arbitrary")),
        name="in_proj_heads",
    )(a, w_t)


def _mm_plain_kernel(a_ref, w_ref, o_ref):
    o_ref[...] = _dot_nt(a_ref[...], w_ref[...]).astype(o_ref.dtype)


def _matmul_f32(a, w_stack, layer, tm):
    t, k = a.shape
    n = w_stack.shape[1]
    tm = min(tm, t)
    return pl.pallas_call(
        _mm_plain_kernel,
        out_shape=jax.ShapeDtypeStruct((t, n), F32),
        grid=(t // tm,),
        in_specs=[pl.BlockSpec((tm, k), lambda i: (i, 0)),
                  pl.BlockSpec((None, n, k), lambda i: (layer, 0, 0))],
        out_specs=pl.BlockSpec((tm, n), lambda i: (i, 0)),
        compiler_params=_params(("parallel",)),
        name="in_proj_latent",
    )(a, w_stack)


def _mla_up_kernel(c_ref, gq_ref, gkv_ref, wq_ref, wkv_ref, cos_ref, sin_ref,
                   qn_ref, qr_ref, kn_ref, v_ref, kr_ref, *, q_rank, kv_rank, scale):
    n_heads = qn_ref.shape[0]
    hw = n_heads * HEAD_DIM
    c = c_ref[...]
    cq = _rmsnorm_rows(c[:, :q_rank], gq_ref[...]).astype(MXU_DTYPE)
    ckv = _rmsnorm_rows(c[:, q_rank:q_rank + kv_rank], gkv_ref[...]).astype(MXU_DTYPE)
    q = jnp.dot(cq, wq_ref[...], preferred_element_type=F32)
    kv = jnp.dot(ckv, wkv_ref[...], preferred_element_type=F32)
    cos, sin = cos_ref[...], sin_ref[...]
    for h in range(n_heads):
        lo, hi = h * HEAD_DIM, (h + 1) * HEAD_DIM
        qn_ref[h] = (q[:, lo:hi] * scale).astype(qn_ref.dtype)
        roped = q[:, hw + lo:hw + hi] * cos + q[:, 2 * hw + lo:2 * hw + hi] * sin
        qr_ref[h] = (roped * scale).astype(qr_ref.dtype)
        kn_ref[h] = kv[:, lo:hi].astype(kn_ref.dtype)
        v_ref[h] = kv[:, hw + lo:hw + hi].astype(v_ref.dtype)
    o = q_rank + kv_rank
    kr_ref[...] = (c[:, o:o + LANE] * cos + c[:, o + LANE:o + 2 * LANE] * sin).astype(kr_ref.dtype)


def _mla_up(c, gq, gkv, wq_stack, wkv_stack, cos, sin, layer, n_heads, seq, scale):
    t, cw = c.shape
    q_rank, kv_rank = gq.shape[-1], gkv.shape[-1]
    tm = min(512, seq)
    per_seq = seq // tm
    heads = jax.ShapeDtypeStruct((n_heads, t, LANE), MXU_DTYPE)
    head_spec = pl.BlockSpec((n_heads, tm, LANE), lambda i: (0, i, 0))
    return pl.pallas_call(
        functools.partial(_mla_up_kernel, q_rank=q_rank, kv_rank=kv_rank, scale=scale),
        out_shape=(heads, heads, heads, heads, jax.ShapeDtypeStruct((t, LANE), MXU_DTYPE)),
        grid=(t // tm,),
        in_specs=[pl.BlockSpec((tm, cw), lambda i: (i, 0)),
                  pl.BlockSpec((1, q_rank), lambda i: (0, 0)),
                  pl.BlockSpec((1, kv_rank), lambda i: (0, 0)),
                  pl.BlockSpec((None,) + wq_stack.shape[1:], lambda i: (layer, 0, 0)),
                  pl.BlockSpec((None,) + wkv_stack.shape[1:], lambda i: (layer, 0, 0)),
                  pl.BlockSpec((tm, LANE), lambda i: (i % per_seq, 0)),
                  pl.BlockSpec((tm, LANE), lambda i: (i % per_seq, 0))],
        out_specs=(head_spec, head_spec, head_spec, head_spec,
                   pl.BlockSpec((tm, LANE), lambda i: (i, 0))),
        compiler_params=_params(("parallel",)),
        name="mla_up",
    )(c, gq.reshape(1, q_rank), gkv.reshape(1, kv_rank), wq_stack, wkv_stack, cos, sin)


def _fox_gate_kernel(f_ref, b_ref, qa_ref, ka_ref):
    n_heads = qa_ref.shape[0]
    z = f_ref[...] + b_ref[...]
    x = -(jnp.maximum(-z, 0.0) + jnp.log1p(jnp.exp(-jnp.abs(z))))
    s = z.shape[0]
    rows = lax.broadcasted_iota(jnp.int32, z.shape, 0)
    shift = 1
    while shift < s:
        x = x + jnp.where(rows >= shift, pltpu.roll(x, shift, axis=0), 0.0)
        shift *= 2
    x = x * LOG2E
    lane = lax.broadcasted_iota(jnp.int32, z.shape, 1)
    ones = jnp.where(lane < 3, 1.0, 0.0)
    for h in range(n_heads):
        hi, mid, lo = _split3(jnp.broadcast_to(x[:, h:h + 1], x.shape))
        pieces = jnp.where(lane == 0, hi, jnp.where(lane == 1, mid, jnp.where(lane == 2, lo, 0.0)))
        qa_ref[h] = (pieces + pltpu.roll(ones, 3, axis=1)).astype(qa_ref.dtype)
        ka_ref[h] = (ones - pltpu.roll(pieces, 3, axis=1)).astype(ka_ref.dtype)


def _fox_gate(c, b_f, f_col_block, batch, seq, n_heads):
    b_pad = jnp.zeros((1, LANE), F32).at[0, :n_heads].set(b_f)
    aux = jax.ShapeDtypeStruct((n_heads, batch * seq, LANE), MXU_DTYPE)
    aux_spec = pl.BlockSpec((n_heads, seq, LANE), lambda b: (0, b, 0))
    return pl.pallas_call(
        _fox_gate_kernel,
        out_shape=(aux, aux),
        grid=(batch,),
        in_specs=[pl.BlockSpec((seq, LANE), lambda b: (b, f_col_block)),
                  pl.BlockSpec((1, LANE), lambda b: (0, 0))],
        out_specs=(aux_spec, aux_spec),
        compiler_params=_params(("parallel",)),
        name="fox_gate",
    )(c, b_pad)


def _moba_query_aux(q, kmean_ref, qconst, slope2, blk):
    seq = q.shape[0]
    g = lax.dot_general(kmean_ref[...], q.astype(F32), (((1,), (1,)), ((), ())),
                        precision=lax.Precision.HIGHEST, preferred_element_type=F32)
    row = lax.broadcasted_iota(jnp.int32, g.shape, 0)
    bid = row & (MOBA_MAX_BLOCKS - 1)
    qblk = lax.broadcasted_iota(jnp.int32, g.shape, 1) // blk
    rank = jnp.zeros(g.shape, F32)
    for n in range(seq // blk - 1):
        gn = g[n:n + 1, :]
        beats = ((gn > g) | ((gn == g) & (n < bid))) & (n < qblk)
        rank = rank + jnp.where(beats, 1.0, 0.0)
    past = bid < qblk
    chosen = (rank < MOBA_TOPK) & past
    offset = (qblk - bid).astype(F32) * (-slope2 * blk)
    term = jnp.where(chosen, offset, jnp.where(past, NEG_INF, 0.0))
    hi = term.astype(MXU_DTYPE).astype(F32)
    lo = jnp.where(chosen, term - hi, 0.0)
    aux_t = jnp.where(row < MOBA_MAX_BLOCKS, hi, lo)
    aux = jnp.concatenate([aux_t, jnp.zeros((LANE - aux_t.shape[0], seq), F32)], axis=0).T
    return [(aux[r0:r0 + blk, :] + qconst).astype(MXU_DTYPE) for r0 in range(0, seq, blk)]


def _attention_kernel(*refs, moba):
    if moba:
        slopes_ref, q1_ref, k1_ref, v_ref, kc_ref, qc_ref, o_ref, kk_ref, vv_ref, kmean_ref = refs
        k2_ref = kc_ref
    else:
        q1_ref, q2_ref, k1_ref, k2_ref, v_ref, o_ref, kk_ref, vv_ref = refs
    seq = k1_ref.shape[0]
    tq = ATTN_BLOCK
    kk_ref[:, :HEAD_DIM] = k1_ref[...]
    kk_ref[:, HEAD_DIM:] = k2_ref[...]
    vv_ref[:, :HEAD_DIM] = v_ref[...]
    vv_ref[:, HEAD_DIM:] = jnp.ones((seq, HEAD_DIM), vv_ref.dtype)
    if moba:
        slope2 = slopes_ref[pl.program_id(1)] * LOG2E
        qconst = qc_ref[...]
        kmean_ref[...] = jnp.zeros_like(kmean_ref)
        for n in range(seq // tq):
            mean = jnp.mean(k1_ref[n * tq:(n + 1) * tq, :].astype(F32), axis=0, keepdims=True)
            kmean_ref[n:n + 1, :] = mean
            kmean_ref[MOBA_MAX_BLOCKS + n:MOBA_MAX_BLOCKS + n + 1, :] = mean
        q2_tiles = _moba_query_aux(q1_ref[...], kmean_ref, qconst, slope2, tq)
    r = lax.broadcasted_iota(jnp.int32, (tq, tq), 0)
    c = lax.broadcasted_iota(jnp.int32, (tq, tq), 1)
    for qi in reversed(range(seq // tq)):
        rows = slice(qi * tq, (qi + 1) * tq)
        n = (qi + 1) * tq
        q1 = q1_ref[rows, :]
        q2 = q2_tiles[qi] if moba else q2_ref[rows, :]
        s = lax.dot_general(jnp.concatenate([q1, q2], axis=1), kk_ref[:n, :],
                            (((1,), (1,)), ((), ())), preferred_element_type=F32)
        diag = jnp.where(c <= r, s[:, n - tq:], NEG_INF)
        s = diag if qi == 0 else jnp.concatenate([s[:, :n - tq], diag], axis=1)
        m = jnp.max(s, axis=-1, keepdims=True)
        p = jnp.exp2(s - m).astype(vv_ref.dtype)
        acc = jnp.dot(p, vv_ref[:n, :], preferred_element_type=F32)
        o_ref[rows, :] = (acc[:, :HEAD_DIM] / acc[:, HEAD_DIM:]).astype(o_ref.dtype)


def _attention(q1, q2, k1, k2, v, batch, seq, n_heads, q1_group=0, k1_group=0, v_group=0,
               k2_shared=False, moba_slopes=None):
    assert seq % ATTN_BLOCK == 0
    t = batch * seq
    moba = moba_slopes is not None
    head = lambda g: pl.BlockSpec((None, seq, HEAD_DIM), lambda b, h: (g + h, b, 0))
    if moba:
        assert seq // ATTN_BLOCK <= MOBA_MAX_BLOCKS and ATTN_BLOCK == MOBA_BLOCK
        in_specs = [pl.BlockSpec(memory_space=pltpu.SMEM), head(q1_group), head(k1_group), head(v_group),
                    pl.BlockSpec((None, seq, LANE), lambda b, h: (h, 0, 0)),
                    pl.BlockSpec((None, ATTN_BLOCK, LANE), lambda b, h: (h, 0, 0))]
        args = (moba_slopes, q1, k1, v, k2, q2)
        scratch = [pltpu.VMEM((2 * MOBA_MAX_BLOCKS, HEAD_DIM), F32)]
    else:
        k2_spec = pl.BlockSpec((seq, LANE), lambda b, h: (b, 0)) if k2_shared else head(0)
        in_specs = [head(q1_group), head(0), head(k1_group), k2_spec, head(v_group)]
        args = (q1, q2, k1, k2, v)
        scratch = []
    return pl.pallas_call(
        functools.partial(_attention_kernel, moba=moba),
        out_shape=jax.ShapeDtypeStruct((n_heads, t, HEAD_DIM), MXU_DTYPE),
        grid=(batch, n_heads),
        in_specs=in_specs,
        out_specs=head(0),
        scratch_shapes=[pltpu.VMEM((seq, 2 * HEAD_DIM), MXU_DTYPE),
                        pltpu.VMEM((seq, 2 * HEAD_DIM), MXU_DTYPE)] + scratch,
        compiler_params=_params(("parallel", "parallel")),
        name="moba_attention" if moba else "causal_attention",
    )(*args)


def _moba_constants(seq, n_heads):
    slopes = jnp.exp2(-8.0 * jnp.arange(1, n_heads + 1, dtype=F32) / n_heads)
    pieces = jnp.stack(_split3(slopes * LOG2E), axis=-1)
    pos = jnp.arange(seq)
    onehot = (pos[:, None] // MOBA_BLOCK == jnp.arange(MOBA_MAX_BLOCKS)[None, :]).astype(F32)
    within = jnp.broadcast_to((pos % MOBA_BLOCK).astype(F32)[None, :, None], (n_heads, seq, 3))
    k_aux = jnp.concatenate([jnp.broadcast_to(onehot, (n_heads, seq, MOBA_MAX_BLOCKS))] * 2
                            + [jnp.broadcast_to(-pieces[:, None, :], (n_heads, seq, 3)), within], axis=-1)
    q_aux = jnp.concatenate([jnp.zeros((n_heads, MOBA_BLOCK, 2 * MOBA_MAX_BLOCKS), F32),
                             within[:, :MOBA_BLOCK], jnp.broadcast_to(pieces[:, None, :], (n_heads, MOBA_BLOCK, 3))],
                            axis=-1)
    return slopes, _pad_last(k_aux, LANE).astype(MXU_DTYPE), _pad_last(q_aux, LANE)


def _sgu_kernel(u_ref, v_ref, lg_ref, lb_ref, w_ref, b_ref, o_ref):
    tm = u_ref.shape[0]
    nc = tm // SGU_CHUNK
    v = jax.nn.gelu(v_ref[...].astype(F32))
    vc = v - jnp.mean(v, axis=-1, keepdims=True)
    var = jnp.mean(vc * vc, axis=-1, keepdims=True)
    vn = (vc * lax.rsqrt(var + NORM_EPS) * lg_ref[...] + lb_ref[...]).astype(MXU_DTYPE)
    v_side = jnp.concatenate([vn[c * SGU_CHUNK:(c + 1) * SGU_CHUNK, :] for c in range(nc)], axis=1)
    r = lax.broadcasted_iota(jnp.int32, (SGU_CHUNK, SGU_CHUNK), 0)
    c_ix = lax.broadcasted_iota(jnp.int32, (SGU_CHUNK, SGU_CHUNK), 1)
    w = jnp.where(c_ix <= r, w_ref[...], 0.0).astype(MXU_DTYPE)
    mixed = jnp.dot(w, v_side, preferred_element_type=F32) + b_ref[...]
    u = jax.nn.gelu(u_ref[...].astype(F32))
    for c in range(nc):
        rows = slice(c * SGU_CHUNK, (c + 1) * SGU_CHUNK)
        o_ref[rows, :] = (u[rows, :] * mixed[:, c * HEAD_DIM:(c + 1) * HEAD_DIM]).astype(o_ref.dtype)


def _sgu(proj, ln_g, ln_b, w_s, b_s, n_heads, u_group, v_group):
    t = proj.shape[1]
    tm = min(1024, t)
    vec = pl.BlockSpec((None, 1, HEAD_DIM), lambda g, i: (g, 0, 0))
    return pl.pallas_call(
        _sgu_kernel,
        out_shape=jax.ShapeDtypeStruct((n_heads, t, HEAD_DIM), MXU_DTYPE),
        grid=(n_heads, t // tm),
        in_specs=[pl.BlockSpec((None, tm, HEAD_DIM), lambda g, i: (u_group + g, i, 0)),
                  pl.BlockSpec((None, tm, HEAD_DIM), lambda g, i: (v_group + g, i, 0)),
                  vec, vec,
                  pl.BlockSpec((None, SGU_CHUNK, SGU_CHUNK), lambda g, i: (g, 0, 0)),
                  pl.BlockSpec((None, SGU_CHUNK, 1), lambda g, i: (g, 0, 0))],
        out_specs=pl.BlockSpec((None, tm, HEAD_DIM), lambda g, i: (g, i, 0)),
        compiler_params=_params(("parallel", "parallel")),
        name="spatial_gating",
    )(proj, proj, ln_g[:, None, :], ln_b[:, None, :], w_s, b_s[:, :, None])


def _out_proj_kernel(*refs, n_mix):
    y_refs, (gn_ref, w_ref, resid_ref, o_ref, ob_ref, op_ref, a_ref) = refs[:n_mix], refs[n_mix:]
    per = y_refs[0].shape[0]
    width = per * LANE
    tm = a_ref.shape[0]

    @pl.when(pl.program_id(1) == 0)
    def _():
        for m, y_ref in enumerate(y_refs):
            for r0 in range(0, tm, NORM_ROW_CHUNK):
                rows = slice(r0, min(r0 + NORM_ROW_CHUNK, tm))
                parts = [y_ref[i, rows, :].astype(F32) for i in range(per)]
                ssq = sum(jnp.sum(p * p, axis=-1, keepdims=True) for p in parts)
                rstd = lax.rsqrt(ssq / width + NORM_EPS)
                for i, p in enumerate(parts):
                    lo = m * width + i * LANE
                    a_ref[rows, lo:lo + LANE] = (p * rstd * gn_ref[m:m + 1, i * LANE:(i + 1) * LANE]).astype(a_ref.dtype)

    x_new = resid_ref[...] + jnp.dot(a_ref[...], w_ref[...], preferred_element_type=F32)
    o_ref[...] = x_new
    ob_ref[...] = x_new.astype(ob_ref.dtype)
    sq = x_new * x_new
    op_ref[...] = sum(sq[:, c0:c0 + LANE] for c0 in range(0, sq.shape[1], LANE))


def _out_proj(ys, group_gain, w_stack, layer, resid):
    t, n = resid.shape
    n_mix, width = group_gain.shape
    per = width // LANE
    tm, bn = min(1024, t), min(512, n)
    y_spec = pl.BlockSpec((per, tm, LANE), lambda i, j: (0, i, 0))
    tile = pl.BlockSpec((tm, bn), lambda i, j: (i, j))
    return pl.pallas_call(
        functools.partial(_out_proj_kernel, n_mix=n_mix),
        out_shape=(jax.ShapeDtypeStruct((t, n), F32), jax.ShapeDtypeStruct((t, n), MXU_DTYPE),
                   jax.ShapeDtypeStruct((t, n // bn * LANE), F32)),
        grid=(t // tm, n // bn),
        in_specs=[y_spec] * n_mix + [pl.BlockSpec((n_mix, width), lambda i, j: (0, 0)),
                                     pl.BlockSpec((None, n_mix * width, bn), lambda i, j: (layer, 0, j)),
                                     tile],
        out_specs=(tile, tile, pl.BlockSpec((tm, LANE), lambda i, j: (i, j))),
        scratch_shapes=[pltpu.VMEM((tm, n_mix * width), MXU_DTYPE)],
        compiler_params=_params(("parallel", "arbitrary")),
        name="out_proj",
    )(*ys, group_gain, w_stack, resid)


def _down_proj_norm_kernel(*refs, keep_x, k_total):
    if not keep_x:
        refs = refs + (refs[-1],)
    a_ref, w_ref, resid_ref, g_ref, ox_ref, oh_ref = refs
    k = pl.program_id(1)
    last = pl.num_programs(1) - 1
    tm, n = ox_ref.shape
    bk = a_ref.shape[1]
    k_rem = k_total % bk

    @pl.when(k == 0)
    def _():
        ox_ref[...] = resid_ref[...]

    def accumulate(depth):
        for c0 in range(0, n, PROJ_COL_CHUNK):
            cols = slice(c0, min(c0 + PROJ_COL_CHUNK, n))
            ox_ref[:, cols] += jnp.dot(a_ref[:, :depth], w_ref[:depth, cols], preferred_element_type=F32)

    if k_rem:
        pl.when(k < last)(lambda: accumulate(bk))
        pl.when(k == last)(lambda: accumulate(k_rem))
    else:
        accumulate(bk)

    @pl.when(k == last)
    def _():
        for r0 in range(0, tm, NORM_ROW_CHUNK):
            rows = slice(r0, min(r0 + NORM_ROW_CHUNK, tm))
            oh_ref[rows, :] = _rmsnorm_rows(ox_ref[rows, :], g_ref[...]).astype(oh_ref.dtype)


def _down_proj_norm(a, w_stack, layer, resid, g, h_dtype, keep_x=True):
    t, n = resid.shape
    k_total = a.shape[1]
    tm = min(512, t)
    bk = min(1024, k_total)
    row_block = pl.BlockSpec((tm, n), lambda i, k: (i, 0))
    h_shape = jax.ShapeDtypeStruct((t, n), h_dtype)
    assert keep_x or h_dtype == F32
    out_shape = (jax.ShapeDtypeStruct((t, n), F32), h_shape) if keep_x else h_shape
    return pl.pallas_call(
        functools.partial(_down_proj_norm_kernel, keep_x=keep_x, k_total=k_total),
        out_shape=out_shape,
        grid=(t // tm, pl.cdiv(k_total, bk)),
        in_specs=[pl.BlockSpec((tm, bk), lambda i, k: (i, k)),
                  pl.BlockSpec((None, bk, n), lambda i, k: (layer, k, 0)),
                  pl.BlockSpec((tm, n), lambda i, k: (i, 0), pipeline_mode=pl.Buffered(1)),
                  pl.BlockSpec((1, n), lambda i, k: (0, 0))],
        out_specs=(row_block, row_block) if keep_x else row_block,
        compiler_params=_params(("parallel", "arbitrary")),
        name="down_proj_norm",
    )(a, w_stack, resid, g.reshape(1, n))


def _ffn_up_kernel(h_ref, halo_ref, p_ref, phalo_ref, wg_ref, wv_ref, cw_ref, cb_ref, o_ref, hext_ref,
                   rstd_ref, *, blocks_per_seq):
    i, j = pl.program_id(0), pl.program_id(1)
    tm, d = h_ref.shape

    def inv_rms(p):
        r = lax.rsqrt(jnp.sum(p, axis=-1, keepdims=True) / d + NORM_EPS)
        return jnp.broadcast_to(r, (p.shape[0], LANE))

    @pl.when(j == 0)
    def _():
        halo = halo_ref[...]
        first = i % blocks_per_seq == 0
        hext_ref[:CONV_HALO, :] = jnp.where(first, jnp.zeros_like(halo), halo)
        hext_ref[CONV_HALO:, :] = h_ref[...]
        rstd_ref[:CONV_HALO, :] = inv_rms(phalo_ref[...])
        rstd_ref[CONV_HALO:, :] = inv_rms(p_ref[...])

    def scale_rows(m, r):
        return jnp.concatenate([m[:, c0:c0 + LANE] * r for c0 in range(0, m.shape[1], LANE)], axis=1)

    g = jnp.dot(hext_ref[...], wg_ref[...], preferred_element_type=F32)
    g = scale_rows(g, rstd_ref[...])
    val = scale_rows(jnp.dot(h_ref[...], wv_ref[...], preferred_element_type=F32), rstd_ref[CONV_HALO:, :])
    cw = cw_ref[...]
    gate = cb_ref[...]
    for tap in range(CONV_WIDTH):
        lo = CONV_HALO - (CONV_WIDTH - 1) + tap
        gate = gate + cw[tap:tap + 1, :] * g[lo:lo + tm, :]
    o_ref[...] = (gate * jax.nn.sigmoid(gate) * val).astype(o_ref.dtype)


def _ffn_up(h, ssq, wg_stack, wv_stack, conv_w, conv_b, layer, seq):
    t, d = h.shape
    n_p = ssq.shape[1]
    n = wg_stack.shape[-1]
    tm = min(1024, seq)
    bn = min(512, n)
    per_seq = seq // tm
    halo_blocks = tm // CONV_HALO
    return pl.pallas_call(
        functools.partial(_ffn_up_kernel, blocks_per_seq=per_seq),
        out_shape=jax.ShapeDtypeStruct((t, n), MXU_DTYPE),
        grid=(t // tm, pl.cdiv(n, bn)),
        in_specs=[pl.BlockSpec((tm, d), lambda i, j: (i, 0)),
                  pl.BlockSpec((CONV_HALO, d), lambda i, j: (jnp.maximum(i * halo_blocks - 1, 0), 0)),
                  pl.BlockSpec((tm, n_p), lambda i, j: (i, 0)),
                  pl.BlockSpec((CONV_HALO, n_p), lambda i, j: (jnp.maximum(i * halo_blocks - 1, 0), 0)),
                  pl.BlockSpec((None, d, bn), lambda i, j: (layer, 0, j)),
                  pl.BlockSpec((None, d, bn), lambda i, j: (layer, 0, j)),
                  pl.BlockSpec((None, CONV_WIDTH, bn), lambda i, j: (layer, 0, j)),
                  pl.BlockSpec((None, 1, bn), lambda i, j: (layer, 0, j))],
        out_specs=pl.BlockSpec((tm, bn), lambda i, j: (i, j)),
        scratch_shapes=[pltpu.VMEM((tm + CONV_HALO, d), MXU_DTYPE), pltpu.VMEM((tm + CONV_HALO, LANE), F32)],
        compiler_params=_params(("parallel", "arbitrary")),
        name="ffn_up_conv",
    )(h, h, ssq, ssq, wg_stack, wv_stack, conv_w, conv_b)


def _pad_last(w, width):
    return jnp.pad(w, [(0, 0)] * (w.ndim - 1) + [(0, width - w.shape[-1])])


def _rotate_half_columns(w):
    half = ROPE_DIM // 2
    return jnp.concatenate([-w[..., half:], w[..., :half]], axis=-1)


def _pack_weights(w_in, mla_w_uq, mla_w_ukv, w_o, norm_ffn_g, w_gate, w_val, conv_b, w_down,
                  group_width, q_rank, kv_rank, n_heads):
    gw = group_width
    a_end = 3 * gw
    b_end = a_end + 2 * gw
    cq_end = b_end + q_rank
    ckv_end = cq_end + kv_rank
    kr_end = ckv_end + ROPE_DIM
    d_end = kr_end + 3 * gw
    wt = jnp.swapaxes(w_in, 1, 2).astype(MXU_DTYPE)
    head_segments = ((0, b_end), (kr_end, 3 * gw))
    kr = wt[:, ckv_end:kr_end]
    half = ROPE_DIM // 2
    pad_rows = lambda w: jnp.pad(w, ((0, 0), (0, LANE - w.shape[1]), (0, 0)))
    w_latent = jnp.concatenate(
        [wt[:, b_end:ckv_end], pad_rows(kr), pad_rows(jnp.concatenate([-kr[:, half:], kr[:, :half]], axis=1)),
         pad_rows(wt[:, d_end:])], axis=1)

    layers, qr, _ = mla_w_uq.shape
    uq = mla_w_uq.reshape(layers, qr, n_heads, HEAD_DIM + ROPE_DIM)
    nope, rope = uq[..., :HEAD_DIM], uq[..., HEAD_DIM:]
    flat = lambda w: w.reshape(layers, w.shape[1], n_heads * HEAD_DIM)
    wq = jnp.concatenate([flat(nope), flat(_pad_last(rope, HEAD_DIM)),
                          flat(_pad_last(_rotate_half_columns(rope), HEAD_DIM))], axis=-1).astype(MXU_DTYPE)
    ukv = mla_w_ukv.reshape(layers, kv_rank, n_heads, 2 * HEAD_DIM)
    wkv = jnp.concatenate([flat(ukv[..., :HEAD_DIM]), flat(ukv[..., HEAD_DIM:])], axis=-1).astype(MXU_DTYPE)

    ffn_gain = norm_ffn_g[:, :, None]
    return ([wt[l] for l in range(wt.shape[0])], head_segments, w_latent, wq, wkv, w_o.astype(MXU_DTYPE),
            (w_gate * ffn_gain).astype(MXU_DTYPE), (w_val * ffn_gain).astype(MXU_DTYPE),
            w_down.astype(MXU_DTYPE), conv_b[:, None, :])


def _rope_tables(seq):
    half = ROPE_DIM // 2
    inv_freq = ROPE_THETA ** (-jnp.arange(half, dtype=F32) / half)
    ang = jnp.arange(seq, dtype=F32)[:, None] * inv_freq[None, :]
    cos = jnp.concatenate([jnp.cos(ang), jnp.cos(ang)], axis=-1)
    sin = jnp.concatenate([jnp.sin(ang), jnp.sin(ang)], axis=-1)
    return _pad_last(cos, LANE), _pad_last(sin, LANE)


def kernel(x, norm_mix_g, w_in, sgu_ln_g, sgu_ln_b, sgu_w, sgu_b, mla_q_norm_g, mla_kv_norm_g,
           mla_w_uq, mla_w_ukv, fox_b_f, group_norm_g, w_o, norm_ffn_g, w_gate, w_val, conv_w,
           conv_b, w_down, final_norm_g):
    batch, seq, d_model = x.shape
    depth = w_in.shape[0]
    n_heads = fox_b_f.shape[-1]
    gw = group_norm_g.shape[-1]
    q_rank, kv_rank = mla_q_norm_g.shape[-1], mla_kv_norm_g.shape[-1]
    assert gw == n_heads * HEAD_DIM and seq % MOBA_BLOCK == 0
    assert q_rank % LANE == 0 and kv_rank % LANE == 0

    (w_in_t, head_segments, w_latent, wq, wkv, wo, wg, wv, wd, cb) = _pack_weights(
        w_in, mla_w_uq, mla_w_ukv, w_o, norm_ffn_g, w_gate, w_val, conv_b, w_down, gw, q_rank, kv_rank, n_heads)
    cos, sin = _rope_tables(seq)
    slopes, moba_k_aux, moba_q_aux = _moba_constants(seq, n_heads)
    f_col_block = (q_rank + kv_rank + 2 * LANE) // LANE
    mla_scale = (HEAD_DIM + ROPE_DIM) ** -0.5 * LOG2E
    a_q, a_k, a_v, b_u, b_v, d_q, d_k, d_v = (i * n_heads for i in range(8))
    q_scale, q_starts = HEAD_DIM ** -0.5 * LOG2E, (a_q * HEAD_DIM, d_q * HEAD_DIM)

    xf = x.reshape(batch * seq, d_model)
    h = _rmsnorm(xf, norm_mix_g[0], MXU_DTYPE)
    for l in range(depth):
        proj = _matmul_groups(h, w_in_t[l], 1024, min(1024, gw), head_segments, q_starts, q_scale)
        latent = _matmul_f32(h, w_latent, l, 512)
        qn, qr, kn, vv, kr = _mla_up(latent, mla_q_norm_g[l], mla_kv_norm_g[l], wq, wkv, cos, sin,
                                     l, n_heads, seq, mla_scale)
        fox_q_aux, fox_k_aux = _fox_gate(latent, fox_b_f[l], f_col_block, batch, seq, n_heads)
        y_a = _attention(proj, moba_q_aux, proj, moba_k_aux, proj, batch, seq, n_heads,
                         a_q, a_k, a_v, moba_slopes=slopes)
        y_b = _sgu(proj, sgu_ln_g[l], sgu_ln_b[l], sgu_w[l], sgu_b[l], n_heads, b_u, b_v)
        y_c = _attention(qn, qr, kn, kr, vv, batch, seq, n_heads, k2_shared=True)
        y_d = _attention(proj, fox_q_aux, proj, fox_k_aux, proj, batch, seq, n_heads, d_q, d_k, d_v)
        xf, xb, ssq = _out_proj((y_a, y_b, y_c, y_d), group_norm_g[l], wo, l, xf)
        act = _ffn_up(xb, ssq, wg, wv, conv_w, cb, l, seq)
        if l == depth - 1:
            h = _down_proj_norm(act, wd, l, xf, final_norm_g, F32, keep_x=False)
        else:
            xf, h = _down_proj_norm(act, wd, l, xf, norm_mix_g[l + 1], MXU_DTYPE)
    return h.reshape(batch, seq, d_model)
```

```python
import functools
import math

import jax
import jax.numpy as jnp
from jax import lax
from jax.experimental import pallas as pl
from jax.experimental.pallas import tpu as pltpu

F32 = jnp.float32
MXU_DTYPE = jnp.bfloat16

LANE = 128
HEAD_DIM = 128
MOBA_BLOCK = 256
MOBA_TOPK = 3
SGU_CHUNK = 128
ROPE_DIM = 64
ROPE_THETA = 10000.0
CONV_WIDTH = 3
NORM_EPS = 1e-6
NEG_INF = -1e30
LOG2E = math.log2(math.e)

ATTN_BLOCK = 256
MOBA_MAX_BLOCKS = 8
CONV_HALO = 16
PROJ_COL_CHUNK = 1024
NORM_ROW_CHUNK = 128
VMEM_LIMIT_BYTES = 56 * 1024 * 1024


def _params(semantics):
    return pltpu.CompilerParams(dimension_semantics=semantics, vmem_limit_bytes=VMEM_LIMIT_BYTES)


def _rmsnorm_rows(x, g):
    ms = jnp.mean(x * x, axis=-1, keepdims=True)
    return x * lax.rsqrt(ms + NORM_EPS) * g


def _split3(x):
    hi = x.astype(MXU_DTYPE).astype(F32)
    r = x - hi
    mid = r.astype(MXU_DTYPE).astype(F32)
    lo = (r - mid).astype(MXU_DTYPE).astype(F32)
    return hi, mid, lo


def _norm_kernel(x_ref, g_ref, o_ref):
    o_ref[...] = _rmsnorm_rows(x_ref[...], g_ref[...]).astype(o_ref.dtype)


def _rmsnorm(x, g, out_dtype):
    t, d = x.shape
    tm = min(256, t)
    return pl.pallas_call(
        _norm_kernel,
        out_shape=jax.ShapeDtypeStruct((t, d), out_dtype),
        grid=(t // tm,),
        in_specs=[pl.BlockSpec((tm, d), lambda i: (i, 0)),
                  pl.BlockSpec((1, d), lambda i: (0, 0))],
        out_specs=pl.BlockSpec((tm, d), lambda i: (i, 0)),
        compiler_params=_params(("parallel",)),
        name="rmsnorm",
    )(x, g.reshape(1, d))


def _dot_nt(a, w_t):
    return lax.dot_general(a, w_t, (((1,), (1,)), ((), ())), preferred_element_type=F32)


def _mm_groups_kernel(a_ref, w_ref, o_ref, *, scaled_blocks, scale):
    acc = _dot_nt(a_ref[...], w_ref[...])
    j = pl.program_id(1)
    is_scaled = functools.reduce(jnp.logical_or, [j == b for b in scaled_blocks])
    acc = acc * jnp.where(is_scaled, scale, 1.0)
    for g in range(o_ref.shape[0]):
        o_ref[g] = acc[:, g * LANE:(g + 1) * LANE].astype(o_ref.dtype)


def _matmul_groups(a, w_t, tm, bn, segments, scaled_starts, scale):
    t, k = a.shape
    tm = min(tm, t)
    assert all(rows % bn == 0 for _, rows in segments) and all(s0 % bn == 0 for s0 in scaled_starts)
    n = sum(rows for _, rows in segments)

    def row_start(j):
        r = j * bn
        shift, done = 0, 0
        for r0, rows in segments:
            r = r + jnp.where(j >= done // bn, (r0 - done) - shift, 0)
            shift, done = r0 - done, done + rows
        return pl.multiple_of(r, math.gcd(bn, *(r0 for r0, _ in segments)))

    return pl.pallas_call(
        functools.partial(_mm_groups_kernel, scaled_blocks=[s0 // bn for s0 in scaled_starts], scale=scale),
        out_shape=jax.ShapeDtypeStruct((n // LANE, t, LANE), MXU_DTYPE),
        grid=(t // tm, n // bn),
        in_specs=[pl.BlockSpec((tm, k), lambda i, j: (i, 0)),
                  pl.BlockSpec((pl.Element(bn), pl.Element(k)), lambda i, j: (row_start(j), 0))],
        out_specs=pl.BlockSpec((bn // LANE, tm, LANE), lambda i, j: (j, i, 0)),
        compiler_params=_params(("parallel", "arbitrary")),
        name="in_proj_heads",
    )(a, w_t)


def _mm_plain_kernel(a_ref, w_ref, o_ref):
    o_ref[...] = _dot_nt(a_ref[...], w_ref[...]).astype(o_ref.dtype)


def _matmul_f32(a, w_t, tm):
    t, k = a.shape
    n = w_t.shape[0]
    tm = min(tm, t)
    return pl.pallas_call(
        _mm_plain_kernel,
        out_shape=jax.ShapeDtypeStruct((t, n), F32),
        grid=(t // tm,),
        in_specs=[pl.BlockSpec((tm, k), lambda i: (i, 0)),
                  pl.BlockSpec((n, k), lambda i: (0, 0))],
        out_specs=pl.BlockSpec((tm, n), lambda i: (i, 0)),
        compiler_params=_params(("parallel",)),
        name="in_proj_latent",
    )(a, w_t)


def _mla_up_kernel(c_ref, gq_ref, gkv_ref, wq_ref, wkv_ref, cos_ref, sin_ref,
                   qn_ref, qr_ref, kn_ref, v_ref, kr_ref, *, q_rank, kv_rank, scale):
    n_heads = qn_ref.shape[0]
    hw = n_heads * HEAD_DIM
    c = c_ref[...]
    cq = _rmsnorm_rows(c[:, :q_rank], gq_ref[...]).astype(MXU_DTYPE)
    ckv = _rmsnorm_rows(c[:, q_rank:q_rank + kv_rank], gkv_ref[...]).astype(MXU_DTYPE)
    q = jnp.dot(cq, wq_ref[...], preferred_element_type=F32)
    kv = jnp.dot(ckv, wkv_ref[...], preferred_element_type=F32)
    cos, sin = cos_ref[...], sin_ref[...]

    def rope(pair):
        return pair * cos + pltpu.roll(pair, ROPE_DIM, axis=1) * sin

    for h in range(n_heads):
        lo, hi = h * HEAD_DIM, (h + 1) * HEAD_DIM
        qn_ref[h] = (q[:, lo:hi] * scale).astype(qn_ref.dtype)
        qr_ref[h] = (rope(q[:, hw + lo:hw + hi]) * scale).astype(qr_ref.dtype)
        kn_ref[h] = kv[:, lo:hi].astype(kn_ref.dtype)
        v_ref[h] = kv[:, hw + lo:hw + hi].astype(v_ref.dtype)
    o = q_rank + kv_rank
    kr_ref[...] = rope(c[:, o:o + LANE]).astype(kr_ref.dtype)


def _mla_up(c, gq, gkv, wq_stack, wkv_stack, cos, sin, layer, n_heads, seq, scale):
    t, cw = c.shape
    q_rank, kv_rank = gq.shape[-1], gkv.shape[-1]
    tm = min(512, seq)
    per_seq = seq // tm
    heads = jax.ShapeDtypeStruct((n_heads, t, LANE), MXU_DTYPE)
    head_spec = pl.BlockSpec((n_heads, tm, LANE), lambda i: (0, i, 0))
    return pl.pallas_call(
        functools.partial(_mla_up_kernel, q_rank=q_rank, kv_rank=kv_rank, scale=scale),
        out_shape=(heads, heads, heads, heads, jax.ShapeDtypeStruct((t, LANE), MXU_DTYPE)),
        grid=(t // tm,),
        in_specs=[pl.BlockSpec((tm, cw), lambda i: (i, 0)),
                  pl.BlockSpec((1, q_rank), lambda i: (0, 0)),
                  pl.BlockSpec((1, kv_rank), lambda i: (0, 0)),
                  pl.BlockSpec((None,) + wq_stack.shape[1:], lambda i: (layer, 0, 0)),
                  pl.BlockSpec((None,) + wkv_stack.shape[1:], lambda i: (layer, 0, 0)),
                  pl.BlockSpec((tm, LANE), lambda i: (i % per_seq, 0)),
                  pl.BlockSpec((tm, LANE), lambda i: (i % per_seq, 0))],
        out_specs=(head_spec, head_spec, head_spec, head_spec,
                   pl.BlockSpec((tm, LANE), lambda i: (i, 0))),
        compiler_params=_params(("parallel",)),
        name="mla_up",
    )(c, gq.reshape(1, q_rank), gkv.reshape(1, kv_rank), wq_stack, wkv_stack, cos, sin)


def _fox_gate_kernel(f_ref, b_ref, qa_ref, ka_ref):
    n_heads = qa_ref.shape[0]
    z = f_ref[...] + b_ref[...]
    x = -(jnp.maximum(-z, 0.0) + jnp.log1p(jnp.exp(-jnp.abs(z))))
    s = z.shape[0]
    rows = lax.broadcasted_iota(jnp.int32, z.shape, 0)
    shift = 1
    while shift < s:
        x = x + jnp.where(rows >= shift, pltpu.roll(x, shift, axis=0), 0.0)
        shift *= 2
    x = x * LOG2E
    lane = lax.broadcasted_iota(jnp.int32, z.shape, 1)
    ones = jnp.where(lane < 3, 1.0, 0.0)
    for h in range(n_heads):
        hi, mid, lo = _split3(jnp.broadcast_to(x[:, h:h + 1], x.shape))
        pieces = jnp.where(lane == 0, hi, jnp.where(lane == 1, mid, jnp.where(lane == 2, lo, 0.0)))
        qa_ref[h] = (pieces + pltpu.roll(ones, 3, axis=1)).astype(qa_ref.dtype)
        ka_ref[h] = (ones - pltpu.roll(pieces, 3, axis=1)).astype(ka_ref.dtype)


def _fox_gate(c, b_f, f_col_block, batch, seq, n_heads):
    b_pad = jnp.zeros((1, LANE), F32).at[0, :n_heads].set(b_f)
    aux = jax.ShapeDtypeStruct((n_heads, batch * seq, LANE), MXU_DTYPE)
    aux_spec = pl.BlockSpec((n_heads, seq, LANE), lambda b: (0, b, 0))
    return pl.pallas_call(
        _fox_gate_kernel,
        out_shape=(aux, aux),
        grid=(batch,),
        in_specs=[pl.BlockSpec((seq, LANE), lambda b: (b, f_col_block)),
                  pl.BlockSpec((1, LANE), lambda b: (0, 0))],
        out_specs=(aux_spec, aux_spec),
        compiler_params=_params(("parallel",)),
        name="fox_gate",
    )(c, b_pad)


def _moba_query_aux(q, kmean_ref, qconst, slope2, blk):
    seq = q.shape[0]
    g = lax.dot_general(kmean_ref[...], q.astype(F32), (((1,), (1,)), ((), ())),
                        precision=lax.Precision.HIGHEST, preferred_element_type=F32)
    row = lax.broadcasted_iota(jnp.int32, g.shape, 0)
    bid = row & (MOBA_MAX_BLOCKS - 1)
    qblk = lax.broadcasted_iota(jnp.int32, g.shape, 1) // blk
    rank = jnp.zeros(g.shape, F32)
    for n in range(seq // blk - 1):
        gn = g[n:n + 1, :]
        beats = ((gn > g) | ((gn == g) & (n < bid))) & (n < qblk)
        rank = rank + jnp.where(beats, 1.0, 0.0)
    past = bid < qblk
    chosen = (rank < MOBA_TOPK) & past
    offset = (qblk - bid).astype(F32) * (-slope2 * blk)
    term = jnp.where(chosen, offset, jnp.where(past, NEG_INF, 0.0))
    hi = term.astype(MXU_DTYPE).astype(F32)
    lo = jnp.where(chosen, term - hi, 0.0)
    aux_t = jnp.where(row < MOBA_MAX_BLOCKS, hi, lo)
    aux = jnp.concatenate([aux_t, jnp.zeros((LANE - aux_t.shape[0], seq), F32)], axis=0).T
    return [(aux[r0:r0 + blk, :] + qconst).astype(MXU_DTYPE) for r0 in range(0, seq, blk)]


def _attention_kernel(*refs, moba):
    if moba:
        slopes_ref, q1_ref, k1_ref, v_ref, kc_ref, qc_ref, o_ref, kk_ref, vv_ref, kmean_ref = refs
        k2_ref = kc_ref
    else:
        q1_ref, q2_ref, k1_ref, k2_ref, v_ref, o_ref, kk_ref, vv_ref = refs
    seq = k1_ref.shape[0]
    tq = ATTN_BLOCK
    kk_ref[:, :HEAD_DIM] = k1_ref[...]
    kk_ref[:, HEAD_DIM:] = k2_ref[...]
    vv_ref[:, :HEAD_DIM] = v_ref[...]
    vv_ref[:, HEAD_DIM:] = jnp.ones((seq, HEAD_DIM), vv_ref.dtype)
    if moba:
        slope2 = slopes_ref[pl.program_id(1)] * LOG2E
        qconst = qc_ref[...]
        kmean_ref[...] = jnp.zeros_like(kmean_ref)
        for n in range(seq // tq):
            mean = jnp.mean(k1_ref[n * tq:(n + 1) * tq, :].astype(F32), axis=0, keepdims=True)
            kmean_ref[n:n + 1, :] = mean
            kmean_ref[MOBA_MAX_BLOCKS + n:MOBA_MAX_BLOCKS + n + 1, :] = mean
        q2_tiles = _moba_query_aux(q1_ref[...], kmean_ref, qconst, slope2, tq)
    r = lax.broadcasted_iota(jnp.int32, (tq, tq), 0)
    c = lax.broadcasted_iota(jnp.int32, (tq, tq), 1)
    for qi in reversed(range(seq // tq)):
        rows = slice(qi * tq, (qi + 1) * tq)
        n = (qi + 1) * tq
        q1 = q1_ref[rows, :]
        q2 = q2_tiles[qi] if moba else q2_ref[rows, :]
        s = lax.dot_general(jnp.concatenate([q1, q2], axis=1), kk_ref[:n, :],
                            (((1,), (1,)), ((), ())), preferred_element_type=F32)
        diag = jnp.where(c <= r, s[:, n - tq:], NEG_INF)
        s = diag if qi == 0 else jnp.concatenate([s[:, :n - tq], diag], axis=1)
        m = jnp.max(s, axis=-1, keepdims=True)
        p = jnp.exp2(s - m).astype(vv_ref.dtype)
        acc = jnp.dot(p, vv_ref[:n, :], preferred_element_type=F32)
        o_ref[rows, :] = (acc[:, :HEAD_DIM] / acc[:, HEAD_DIM:]).astype(o_ref.dtype)


def _attention(q1, q2, k1, k2, v, batch, seq, n_heads, q1_group=0, k1_group=0, v_group=0,
               k2_shared=False, moba_slopes=None):
    assert seq % ATTN_BLOCK == 0
    t = batch * seq
    moba = moba_slopes is not None
    head = lambda g: pl.BlockSpec((None, seq, HEAD_DIM), lambda b, h: (g + h, b, 0))
    if moba:
        assert seq // ATTN_BLOCK <= MOBA_MAX_BLOCKS and ATTN_BLOCK == MOBA_BLOCK
        in_specs = [pl.BlockSpec(memory_space=pltpu.SMEM), head(q1_group), head(k1_group), head(v_group),
                    pl.BlockSpec((None, seq, LANE), lambda b, h: (h, 0, 0)),
                    pl.BlockSpec((None, ATTN_BLOCK, LANE), lambda b, h: (h, 0, 0))]
        args = (moba_slopes, q1, k1, v, k2, q2)
        scratch = [pltpu.VMEM((2 * MOBA_MAX_BLOCKS, HEAD_DIM), F32)]
    else:
        k2_spec = pl.BlockSpec((seq, LANE), lambda b, h: (b, 0)) if k2_shared else head(0)
        in_specs = [head(q1_group), head(0), head(k1_group), k2_spec, head(v_group)]
        args = (q1, q2, k1, k2, v)
        scratch = []
    return pl.pallas_call(
        functools.partial(_attention_kernel, moba=moba),
        out_shape=jax.ShapeDtypeStruct((n_heads, t, HEAD_DIM), MXU_DTYPE),
        grid=(batch, n_heads),
        in_specs=in_specs,
        out_specs=head(0),
        scratch_shapes=[pltpu.VMEM((seq, 2 * HEAD_DIM), MXU_DTYPE),
                        pltpu.VMEM((seq, 2 * HEAD_DIM), MXU_DTYPE)] + scratch,
        compiler_params=_params(("parallel", "parallel")),
        name="moba_attention" if moba else "causal_attention",
    )(*args)


def _moba_constants(seq, n_heads):
    slopes = jnp.exp2(-8.0 * jnp.arange(1, n_heads + 1, dtype=F32) / n_heads)
    pieces = jnp.stack(_split3(slopes * LOG2E), axis=-1)
    pos = jnp.arange(seq)
    onehot = (pos[:, None] // MOBA_BLOCK == jnp.arange(MOBA_MAX_BLOCKS)[None, :]).astype(F32)
    within = jnp.broadcast_to((pos % MOBA_BLOCK).astype(F32)[None, :, None], (n_heads, seq, 3))
    k_aux = jnp.concatenate([jnp.broadcast_to(onehot, (n_heads, seq, MOBA_MAX_BLOCKS))] * 2
                            + [jnp.broadcast_to(-pieces[:, None, :], (n_heads, seq, 3)), within], axis=-1)
    q_aux = jnp.concatenate([jnp.zeros((n_heads, MOBA_BLOCK, 2 * MOBA_MAX_BLOCKS), F32),
                             within[:, :MOBA_BLOCK], jnp.broadcast_to(pieces[:, None, :], (n_heads, MOBA_BLOCK, 3))],
                            axis=-1)
    return slopes, _pad_last(k_aux, LANE).astype(MXU_DTYPE), _pad_last(q_aux, LANE)


def _sgu_kernel(u_ref, v_ref, lg_ref, lb_ref, w_ref, b_ref, o_ref):
    tm = u_ref.shape[0]
    nc = tm // SGU_CHUNK
    v = jax.nn.gelu(v_ref[...].astype(F32))
    vc = v - jnp.mean(v, axis=-1, keepdims=True)
    var = jnp.mean(vc * vc, axis=-1, keepdims=True)
    vn = (vc * lax.rsqrt(var + NORM_EPS) * lg_ref[...] + lb_ref[...]).astype(MXU_DTYPE)
    v_side = jnp.concatenate([vn[c * SGU_CHUNK:(c + 1) * SGU_CHUNK, :] for c in range(nc)], axis=1)
    r = lax.broadcasted_iota(jnp.int32, (SGU_CHUNK, SGU_CHUNK), 0)
    c_ix = lax.broadcasted_iota(jnp.int32, (SGU_CHUNK, SGU_CHUNK), 1)
    w = jnp.where(c_ix <= r, w_ref[...], 0.0).astype(MXU_DTYPE)
    mixed = jnp.dot(w, v_side, preferred_element_type=F32) + b_ref[...]
    u = jax.nn.gelu(u_ref[...].astype(F32))
    for c in range(nc):
        rows = slice(c * SGU_CHUNK, (c + 1) * SGU_CHUNK)
        o_ref[rows, :] = (u[rows, :] * mixed[:, c * HEAD_DIM:(c + 1) * HEAD_DIM]).astype(o_ref.dtype)


def _sgu(proj, ln_g, ln_b, w_s, b_s, n_heads, u_group, v_group):
    t = proj.shape[1]
    tm = min(1024, t)
    vec = pl.BlockSpec((None, 1, HEAD_DIM), lambda g, i: (g, 0, 0))
    return pl.pallas_call(
        _sgu_kernel,
        out_shape=jax.ShapeDtypeStruct((n_heads, t, HEAD_DIM), MXU_DTYPE),
        grid=(n_heads, t // tm),
        in_specs=[pl.BlockSpec((None, tm, HEAD_DIM), lambda g, i: (u_group + g, i, 0)),
                  pl.BlockSpec((None, tm, HEAD_DIM), lambda g, i: (v_group + g, i, 0)),
                  vec, vec,
                  pl.BlockSpec((None, SGU_CHUNK, SGU_CHUNK), lambda g, i: (g, 0, 0)),
                  pl.BlockSpec((None, SGU_CHUNK, 1), lambda g, i: (g, 0, 0))],
        out_specs=pl.BlockSpec((None, tm, HEAD_DIM), lambda g, i: (g, i, 0)),
        compiler_params=_params(("parallel", "parallel")),
        name="spatial_gating",
    )(proj, proj, ln_g[:, None, :], ln_b[:, None, :], w_s, b_s[:, :, None])


def _out_proj_kernel(*refs, n_mix):
    y_refs, (gn_ref, w_ref, resid_ref, o_ref, ob_ref, op_ref, a_ref) = refs[:n_mix], refs[n_mix:]
    per = y_refs[0].shape[0]
    width = per * LANE
    tm = a_ref.shape[0]

    @pl.when(pl.program_id(1) == 0)
    def _():
        for m, y_ref in enumerate(y_refs):
            for r0 in range(0, tm, NORM_ROW_CHUNK):
                rows = slice(r0, min(r0 + NORM_ROW_CHUNK, tm))
                parts = [y_ref[i, rows, :].astype(F32) for i in range(per)]
                ssq = sum(jnp.sum(p * p, axis=-1, keepdims=True) for p in parts)
                rstd = lax.rsqrt(ssq / width + NORM_EPS)
                for i, p in enumerate(parts):
                    lo = m * width + i * LANE
                    a_ref[rows, lo:lo + LANE] = (p * rstd * gn_ref[m:m + 1, i * LANE:(i + 1) * LANE]).astype(a_ref.dtype)

    x_new = resid_ref[...] + jnp.dot(a_ref[...], w_ref[...], preferred_element_type=F32)
    o_ref[...] = x_new
    ob_ref[...] = x_new.astype(ob_ref.dtype)
    sq = x_new * x_new
    op_ref[...] = sum(sq[:, c0:c0 + LANE] for c0 in range(0, sq.shape[1], LANE))


def _out_proj(ys, group_gain, w_stack, layer, resid):
    t, n = resid.shape
    n_mix, width = group_gain.shape
    per = width // LANE
    tm, bn = min(1024, t), min(512, n)
    y_spec = pl.BlockSpec((per, tm, LANE), lambda i, j: (0, i, 0))
    tile = pl.BlockSpec((tm, bn), lambda i, j: (i, j))
    return pl.pallas_call(
        functools.partial(_out_proj_kernel, n_mix=n_mix),
        out_shape=(jax.ShapeDtypeStruct((t, n), F32), jax.ShapeDtypeStruct((t, n), MXU_DTYPE),
                   jax.ShapeDtypeStruct((t, n // bn * LANE), F32)),
        grid=(t // tm, n // bn),
        in_specs=[y_spec] * n_mix + [pl.BlockSpec((n_mix, width), lambda i, j: (0, 0)),
                                     pl.BlockSpec((None, n_mix * width, bn), lambda i, j: (layer, 0, j)),
                                     tile],
        out_specs=(tile, tile, pl.BlockSpec((tm, LANE), lambda i, j: (i, j))),
        scratch_shapes=[pltpu.VMEM((tm, n_mix * width), MXU_DTYPE)],
        compiler_params=_params(("parallel", "arbitrary")),
        name="out_proj",
    )(*ys, group_gain, w_stack, resid)


def _down_proj_norm_kernel(*refs, keep_x, k_total):
    if not keep_x:
        refs = refs + (refs[-1],)
    a_ref, w_ref, resid_ref, g_ref, ox_ref, oh_ref = refs
    k = pl.program_id(1)
    last = pl.num_programs(1) - 1
    tm, n = ox_ref.shape
    bk = a_ref.shape[1]
    k_rem = k_total % bk

    @pl.when(k == 0)
    def _():
        ox_ref[...] = resid_ref[...]

    def accumulate(depth):
        for c0 in range(0, n, PROJ_COL_CHUNK):
            cols = slice(c0, min(c0 + PROJ_COL_CHUNK, n))
            ox_ref[:, cols] += jnp.dot(a_ref[:, :depth], w_ref[:depth, cols], preferred_element_type=F32)

    if k_rem:
        pl.when(k < last)(lambda: accumulate(bk))
        pl.when(k == last)(lambda: accumulate(k_rem))
    else:
        accumulate(bk)

    @pl.when(k == last)
    def _():
        for r0 in range(0, tm, NORM_ROW_CHUNK):
            rows = slice(r0, min(r0 + NORM_ROW_CHUNK, tm))
            oh_ref[rows, :] = _rmsnorm_rows(ox_ref[rows, :], g_ref[...]).astype(oh_ref.dtype)


def _down_proj_norm(a, w_stack, layer, resid, g, h_dtype, keep_x=True):
    t, n = resid.shape
    k_total = a.shape[1]
    tm = min(512, t)
    bk = min(1024, k_total)
    row_block = pl.BlockSpec((tm, n), lambda i, k: (i, 0))
    h_shape = jax.ShapeDtypeStruct((t, n), h_dtype)
    assert keep_x or h_dtype == F32
    out_shape = (jax.ShapeDtypeStruct((t, n), F32), h_shape) if keep_x else h_shape
    return pl.pallas_call(
        functools.partial(_down_proj_norm_kernel, keep_x=keep_x, k_total=k_total),
        out_shape=out_shape,
        grid=(t // tm, pl.cdiv(k_total, bk)),
        in_specs=[pl.BlockSpec((tm, bk), lambda i, k: (i, k)),
                  pl.BlockSpec((None, bk, n), lambda i, k: (layer, k, 0)),
                  pl.BlockSpec((tm, n), lambda i, k: (i, 0), pipeline_mode=pl.Buffered(1)),
                  pl.BlockSpec((1, n), lambda i, k: (0, 0))],
        out_specs=(row_block, row_block) if keep_x else row_block,
        compiler_params=_params(("parallel", "arbitrary")),
        name="down_proj_norm",
    )(a, w_stack, resid, g.reshape(1, n))


def _ffn_up_kernel(h_ref, halo_ref, p_ref, phalo_ref, wg_ref, wv_ref, cw_ref, cb_ref, o_ref, hext_ref,
                   rstd_ref, *, blocks_per_seq):
    i, j = pl.program_id(0), pl.program_id(1)
    tm, d = h_ref.shape

    def inv_rms(p):
        r = lax.rsqrt(jnp.sum(p, axis=-1, keepdims=True) / d + NORM_EPS)
        return jnp.broadcast_to(r, (p.shape[0], LANE))

    @pl.when(j == 0)
    def _():
        halo = halo_ref[...]
        first = i % blocks_per_seq == 0
        hext_ref[:CONV_HALO, :] = jnp.where(first, jnp.zeros_like(halo), halo)
        hext_ref[CONV_HALO:, :] = h_ref[...]
        rstd_ref[:CONV_HALO, :] = inv_rms(phalo_ref[...])
        rstd_ref[CONV_HALO:, :] = inv_rms(p_ref[...])

    def scale_rows(m, r):
        return jnp.concatenate([m[:, c0:c0 + LANE] * r for c0 in range(0, m.shape[1], LANE)], axis=1)

    g = jnp.dot(hext_ref[...], wg_ref[...], preferred_element_type=F32)
    g = scale_rows(g, rstd_ref[...])
    val = scale_rows(jnp.dot(h_ref[...], wv_ref[...], preferred_element_type=F32), rstd_ref[CONV_HALO:, :])
    cw = cw_ref[...]
    gate = cb_ref[...]
    for tap in range(CONV_WIDTH):
        lo = CONV_HALO - (CONV_WIDTH - 1) + tap
        gate = gate + cw[tap:tap + 1, :] * g[lo:lo + tm, :]
    o_ref[...] = (gate * jax.nn.sigmoid(gate) * val).astype(o_ref.dtype)


def _ffn_up(h, ssq, wg_stack, wv_stack, conv_w, conv_b, layer, seq):
    t, d = h.shape
    n_p = ssq.shape[1]
    n = wg_stack.shape[-1]
    tm = min(1024, seq)
    bn = min(512, n)
    per_seq = seq // tm
    halo_blocks = tm // CONV_HALO
    return pl.pallas_call(
        functools.partial(_ffn_up_kernel, blocks_per_seq=per_seq),
        out_shape=jax.ShapeDtypeStruct((t, n), MXU_DTYPE),
        grid=(t // tm, pl.cdiv(n, bn)),
        in_specs=[pl.BlockSpec((tm, d), lambda i, j: (i, 0)),
                  pl.BlockSpec((CONV_HALO, d), lambda i, j: (jnp.maximum(i * halo_blocks - 1, 0), 0)),
                  pl.BlockSpec((tm, n_p), lambda i, j: (i, 0)),
                  pl.BlockSpec((CONV_HALO, n_p), lambda i, j: (jnp.maximum(i * halo_blocks - 1, 0), 0)),
                  pl.BlockSpec((None, d, bn), lambda i, j: (layer, 0, j)),
                  pl.BlockSpec((None, d, bn), lambda i, j: (layer, 0, j)),
                  pl.BlockSpec((None, CONV_WIDTH, bn), lambda i, j: (layer, 0, j)),
                  pl.BlockSpec((None, 1, bn), lambda i, j: (layer, 0, j))],
        out_specs=pl.BlockSpec((tm, bn), lambda i, j: (i, j)),
        scratch_shapes=[pltpu.VMEM((tm + CONV_HALO, d), MXU_DTYPE), pltpu.VMEM((tm + CONV_HALO, LANE), F32)],
        compiler_params=_params(("parallel", "arbitrary")),
        name="ffn_up_conv",
    )(h, h, ssq, ssq, wg_stack, wv_stack, conv_w, conv_b)


def _pad_last(w, width):
    return jnp.pad(w, [(0, 0)] * (w.ndim - 1) + [(0, width - w.shape[-1])])


def _rotate_half_columns(w):
    half = ROPE_DIM // 2
    return jnp.concatenate([-w[..., half:], w[..., :half]], axis=-1)


def _pack_weights(w_in, mla_w_uq, mla_w_ukv, w_o, norm_ffn_g, w_gate, w_val, conv_b, w_down,
                  group_width, q_rank, kv_rank, n_heads):
    gw = group_width
    a_end = 3 * gw
    b_end = a_end + 2 * gw
    cq_end = b_end + q_rank
    ckv_end = cq_end + kv_rank
    kr_end = ckv_end + ROPE_DIM
    d_end = kr_end + 3 * gw
    layers, qr, _ = mla_w_uq.shape
    head_segments = ((0, b_end), (kr_end, 3 * gw))
    half = ROPE_DIM // 2
    w_in_t, w_latent = [], []
    for l in range(layers):
        wt = jnp.swapaxes(w_in[l], 0, 1).astype(MXU_DTYPE)
        kr = wt[ckv_end:kr_end]
        forget = wt[d_end:]
        w_in_t.append(wt)
        w_latent.append(jnp.concatenate(
            [wt[b_end:ckv_end], kr, -kr[half:], kr[:half],
             jnp.pad(forget, ((0, LANE - forget.shape[0]), (0, 0)))], axis=0))

    uq = mla_w_uq.reshape(layers, qr, n_heads, HEAD_DIM + ROPE_DIM)
    nope, rope = uq[..., :HEAD_DIM], uq[..., HEAD_DIM:]
    flat = lambda w: w.reshape(layers, w.shape[1], n_heads * HEAD_DIM)
    wq = jnp.concatenate([flat(nope), flat(jnp.concatenate([rope, _rotate_half_columns(rope)], axis=-1))],
                         axis=-1).astype(MXU_DTYPE)
    ukv = mla_w_ukv.reshape(layers, kv_rank, n_heads, 2 * HEAD_DIM)
    wkv = jnp.concatenate([flat(ukv[..., :HEAD_DIM]), flat(ukv[..., HEAD_DIM:])], axis=-1).astype(MXU_DTYPE)

    ffn_gain = norm_ffn_g[:, :, None]
    return (w_in_t, head_segments, w_latent, wq, wkv, w_o.astype(MXU_DTYPE),
            (w_gate * ffn_gain).astype(MXU_DTYPE), (w_val * ffn_gain).astype(MXU_DTYPE),
            w_down.astype(MXU_DTYPE), conv_b[:, None, :])


def _rope_tables(seq):
    half = ROPE_DIM // 2
    inv_freq = ROPE_THETA ** (-jnp.arange(half, dtype=F32) / half)
    ang = jnp.arange(seq, dtype=F32)[:, None] * inv_freq[None, :]
    cos = jnp.concatenate([jnp.cos(ang), jnp.cos(ang)], axis=-1)
    sin = jnp.concatenate([jnp.sin(ang), jnp.sin(ang)], axis=-1)
    return _pad_last(cos, LANE), _pad_last(sin, LANE)


def kernel(x, norm_mix_g, w_in, sgu_ln_g, sgu_ln_b, sgu_w, sgu_b, mla_q_norm_g, mla_kv_norm_g,
           mla_w_uq, mla_w_ukv, fox_b_f, group_norm_g, w_o, norm_ffn_g, w_gate, w_val, conv_w,
           conv_b, w_down, final_norm_g):
    batch, seq, d_model = x.shape
    depth = w_in.shape[0]
    n_heads = fox_b_f.shape[-1]
    gw = group_norm_g.shape[-1]
    q_rank, kv_rank = mla_q_norm_g.shape[-1], mla_kv_norm_g.shape[-1]
    assert gw == n_heads * HEAD_DIM and seq % MOBA_BLOCK == 0
    assert q_rank % LANE == 0 and kv_rank % LANE == 0

    (w_in_t, head_segments, w_latent, wq, wkv, wo, wg, wv, wd, cb) = _pack_weights(
        w_in, mla_w_uq, mla_w_ukv, w_o, norm_ffn_g, w_gate, w_val, conv_b, w_down, gw, q_rank, kv_rank, n_heads)
    cos, sin = _rope_tables(seq)
    slopes, moba_k_aux, moba_q_aux = _moba_constants(seq, n_heads)
    f_col_block = (q_rank + kv_rank + LANE) // LANE
    mla_scale = (HEAD_DIM + ROPE_DIM) ** -0.5 * LOG2E
    a_q, a_k, a_v, b_u, b_v, d_q, d_k, d_v = (i * n_heads for i in range(8))
    q_scale, q_starts = HEAD_DIM ** -0.5 * LOG2E, (a_q * HEAD_DIM, d_q * HEAD_DIM)

    xf = x.reshape(batch * seq, d_model)
    h = _rmsnorm(xf, norm_mix_g[0], MXU_DTYPE)
    for l in range(depth):
        proj = _matmul_groups(h, w_in_t[l], 1024, min(1024, gw), head_segments, q_starts, q_scale)
        latent = _matmul_f32(h, w_latent[l], 512)
        qn, qr, kn, vv, kr = _mla_up(latent, mla_q_norm_g[l], mla_kv_norm_g[l], wq, wkv, cos, sin,
                                     l, n_heads, seq, mla_scale)
        fox_q_aux, fox_k_aux = _fox_gate(latent, fox_b_f[l], f_col_block, batch, seq, n_heads)
        y_a = _attention(proj, moba_q_aux, proj, moba_k_aux, proj, batch, seq, n_heads,
                         a_q, a_k, a_v, moba_slopes=slopes)
        y_b = _sgu(proj, sgu_ln_g[l], sgu_ln_b[l], sgu_w[l], sgu_b[l], n_heads, b_u, b_v)
        y_c = _attention(qn, qr, kn, kr, vv, batch, seq, n_heads, k2_shared=True)
        y_d = _attention(proj, fox_q_aux, proj, fox_k_aux, proj, batch, seq, n_heads, d_q, d_k, d_v)
        xf, xb, ssq = _out_proj((y_a, y_b, y_c, y_d), group_norm_g[l], wo, l, xf)
        act = _ffn_up(xb, ssq, wg, wv, conv_w, cb, l, seq)
        if l == depth - 1:
            h = _down_proj_norm(act, wd, l, xf, final_norm_g, F32, keep_x=False)
        else:
            xf, h = _down_proj_norm(act, wd, l, xf, norm_mix_g[l + 1], MXU_DTYPE)
    return h.reshape(batch, seq, d_model)
```

```python
import functools
import math

import jax
import jax.numpy as jnp
from jax import lax
from jax.experimental import pallas as pl
from jax.experimental.pallas import tpu as pltpu

F32 = jnp.float32
MXU_DTYPE = jnp.bfloat16

LANE = 128
HEAD_DIM = 128
MOBA_BLOCK = 256
MOBA_TOPK = 3
SGU_CHUNK = 128
ROPE_DIM = 64
ROPE_THETA = 10000.0
CONV_WIDTH = 3
NORM_EPS = 1e-6
NEG_INF = -1e30
LOG2E = math.log2(math.e)

ATTN_BLOCK = 256
MOBA_MAX_BLOCKS = 8
ATTN_HEADS_PER_STEP = 4
CONV_HALO = 16
PROJ_COL_CHUNK = 1024
NORM_ROW_CHUNK = 128
VMEM_LIMIT_BYTES = 56 * 1024 * 1024


def _params(semantics):
    return pltpu.CompilerParams(dimension_semantics=semantics, vmem_limit_bytes=VMEM_LIMIT_BYTES)


def _rmsnorm_rows(x, g):
    ms = jnp.mean(x * x, axis=-1, keepdims=True)
    return x * lax.rsqrt(ms + NORM_EPS) * g


def _split3(x):
    hi = x.astype(MXU_DTYPE).astype(F32)
    r = x - hi
    mid = r.astype(MXU_DTYPE).astype(F32)
    lo = (r - mid).astype(MXU_DTYPE).astype(F32)
    return hi, mid, lo


def _norm_kernel(x_ref, g_ref, o_ref):
    o_ref[...] = _rmsnorm_rows(x_ref[...], g_ref[...]).astype(o_ref.dtype)


def _rmsnorm(x, g, out_dtype):
    t, d = x.shape
    tm = min(256, t)
    return pl.pallas_call(
        _norm_kernel,
        out_shape=jax.ShapeDtypeStruct((t, d), out_dtype),
        grid=(t // tm,),
        in_specs=[pl.BlockSpec((tm, d), lambda i: (i, 0)),
                  pl.BlockSpec((1, d), lambda i: (0, 0))],
        out_specs=pl.BlockSpec((tm, d), lambda i: (i, 0)),
        compiler_params=_params(("parallel",)),
        name="rmsnorm",
    )(x, g.reshape(1, d))


def _dot_nt(a, w_t):
    return lax.dot_general(a, w_t, (((1,), (1,)), ((), ())), preferred_element_type=F32)


def _mm_groups_kernel(a_ref, w_ref, o_ref, *, scaled_blocks, scale):
    acc = _dot_nt(a_ref[...], w_ref[...])
    j = pl.program_id(1)
    is_scaled = functools.reduce(jnp.logical_or, [j == b for b in scaled_blocks])
    acc = acc * jnp.where(is_scaled, scale, 1.0)
    for g in range(o_ref.shape[0]):
        o_ref[g] = acc[:, g * LANE:(g + 1) * LANE].astype(o_ref.dtype)


def _matmul_groups(a, w_t, tm, bn, segments, scaled_starts, scale):
    t, k = a.shape
    tm = min(tm, t)
    assert all(rows % bn == 0 for _, rows in segments) and all(s0 % bn == 0 for s0 in scaled_starts)
    n = sum(rows for _, rows in segments)

    def row_start(j):
        r = j * bn
        shift, done = 0, 0
        for r0, rows in segments:
            r = r + jnp.where(j >= done // bn, (r0 - done) - shift, 0)
            shift, done = r0 - done, done + rows
        return pl.multiple_of(r, math.gcd(bn, *(r0 for r0, _ in segments)))

    return pl.pallas_call(
        functools.partial(_mm_groups_kernel, scaled_blocks=[s0 // bn for s0 in scaled_starts], scale=scale),
        out_shape=jax.ShapeDtypeStruct((n // LANE, t, LANE), MXU_DTYPE),
        grid=(t // tm, n // bn),
        in_specs=[pl.BlockSpec((tm, k), lambda i, j: (i, 0)),
                  pl.BlockSpec((pl.Element(bn), pl.Element(k)), lambda i, j: (row_start(j), 0))],
        out_specs=pl.BlockSpec((bn // LANE, tm, LANE), lambda i, j: (j, i, 0)),
        compiler_params=_params(("parallel", "arbitrary")),
        name="in_proj_heads",
    )(a, w_t)


def _mm_plain_kernel(a_ref, w_ref, o_ref):
    o_ref[...] = _dot_nt(a_ref[...], w_ref[...]).astype(o_ref.dtype)


def _matmul_f32(a, w_t, tm):
    t, k = a.shape
    n = w_t.shape[0]
    tm = min(tm, t)
    return pl.pallas_call(
        _mm_plain_kernel,
        out_shape=jax.ShapeDtypeStruct((t, n), F32),
        grid=(t // tm,),
        in_specs=[pl.BlockSpec((tm, k), lambda i: (i, 0)),
                  pl.BlockSpec((n, k), lambda i: (0, 0))],
        out_specs=pl.BlockSpec((tm, n), lambda i: (i, 0)),
        compiler_params=_params(("parallel",)),
        name="in_proj_latent",
    )(a, w_t)


def _mla_up_kernel(c_ref, gq_ref, gkv_ref, wq_ref, wkv_ref, cos_ref, sin_ref,
                   qn_ref, qr_ref, kn_ref, v_ref, kr_ref, *, q_rank, kv_rank, scale):
    n_heads = qn_ref.shape[0]
    hw = n_heads * HEAD_DIM
    c = c_ref[...]
    cq = _rmsnorm_rows(c[:, :q_rank], gq_ref[...]).astype(MXU_DTYPE)
    ckv = _rmsnorm_rows(c[:, q_rank:q_rank + kv_rank], gkv_ref[...]).astype(MXU_DTYPE)
    q = jnp.dot(cq, wq_ref[...], preferred_element_type=F32)
    kv = jnp.dot(ckv, wkv_ref[...], preferred_element_type=F32)
    cos, sin = cos_ref[...], sin_ref[...]

    def rope(pair):
        return pair * cos + pltpu.roll(pair, ROPE_DIM, axis=1) * sin

    for h in range(n_heads):
        lo, hi = h * HEAD_DIM, (h + 1) * HEAD_DIM
        qn_ref[h] = (q[:, lo:hi] * scale).astype(qn_ref.dtype)
        qr_ref[h] = (rope(q[:, hw + lo:hw + hi]) * scale).astype(qr_ref.dtype)
        kn_ref[h] = kv[:, lo:hi].astype(kn_ref.dtype)
        v_ref[h] = kv[:, hw + lo:hw + hi].astype(v_ref.dtype)
    o = q_rank + kv_rank
    kr_ref[...] = rope(c[:, o:o + LANE]).astype(kr_ref.dtype)


def _mla_up(c, gq, gkv, wq_stack, wkv_stack, cos, sin, layer, n_heads, seq, scale):
    t, cw = c.shape
    q_rank, kv_rank = gq.shape[-1], gkv.shape[-1]
    tm = min(512, seq)
    per_seq = seq // tm
    heads = jax.ShapeDtypeStruct((n_heads, t, LANE), MXU_DTYPE)
    head_spec = pl.BlockSpec((n_heads, tm, LANE), lambda i: (0, i, 0))
    return pl.pallas_call(
        functools.partial(_mla_up_kernel, q_rank=q_rank, kv_rank=kv_rank, scale=scale),
        out_shape=(heads, heads, heads, heads, jax.ShapeDtypeStruct((t, LANE), MXU_DTYPE)),
        grid=(t // tm,),
        in_specs=[pl.BlockSpec((tm, cw), lambda i: (i, 0)),
                  pl.BlockSpec((1, q_rank), lambda i: (0, 0)),
                  pl.BlockSpec((1, kv_rank), lambda i: (0, 0)),
                  pl.BlockSpec((None,) + wq_stack.shape[1:], lambda i: (layer, 0, 0)),
                  pl.BlockSpec((None,) + wkv_stack.shape[1:], lambda i: (layer, 0, 0)),
                  pl.BlockSpec((tm, LANE), lambda i: (i % per_seq, 0)),
                  pl.BlockSpec((tm, LANE), lambda i: (i % per_seq, 0))],
        out_specs=(head_spec, head_spec, head_spec, head_spec,
                   pl.BlockSpec((tm, LANE), lambda i: (i, 0))),
        compiler_params=_params(("parallel",)),
        name="mla_up",
    )(c, gq.reshape(1, q_rank), gkv.reshape(1, kv_rank), wq_stack, wkv_stack, cos, sin)


def _fox_gate_kernel(f_ref, b_ref, qa_ref, ka_ref):
    n_heads = qa_ref.shape[0]
    z = f_ref[...] + b_ref[...]
    x = -(jnp.maximum(-z, 0.0) + jnp.log1p(jnp.exp(-jnp.abs(z))))
    s = z.shape[0]
    rows = lax.broadcasted_iota(jnp.int32, z.shape, 0)
    shift = 1
    while shift < s:
        x = x + jnp.where(rows >= shift, pltpu.roll(x, shift, axis=0), 0.0)
        shift *= 2
    x = x * LOG2E
    lane = lax.broadcasted_iota(jnp.int32, z.shape, 1)
    ones = jnp.where(lane < 3, 1.0, 0.0)
    for h in range(n_heads):
        hi, mid, lo = _split3(jnp.broadcast_to(x[:, h:h + 1], x.shape))
        pieces = jnp.where(lane == 0, hi, jnp.where(lane == 1, mid, jnp.where(lane == 2, lo, 0.0)))
        qa_ref[h] = (pieces + pltpu.roll(ones, 3, axis=1)).astype(qa_ref.dtype)
        ka_ref[h] = (ones - pltpu.roll(pieces, 3, axis=1)).astype(ka_ref.dtype)


def _fox_gate(c, b_f, f_col_block, batch, seq, n_heads):
    b_pad = jnp.zeros((1, LANE), F32).at[0, :n_heads].set(b_f)
    aux = jax.ShapeDtypeStruct((n_heads, batch * seq, LANE), MXU_DTYPE)
    aux_spec = pl.BlockSpec((n_heads, seq, LANE), lambda b: (0, b, 0))
    return pl.pallas_call(
        _fox_gate_kernel,
        out_shape=(aux, aux),
        grid=(batch,),
        in_specs=[pl.BlockSpec((seq, LANE), lambda b: (b, f_col_block)),
                  pl.BlockSpec((1, LANE), lambda b: (0, 0))],
        out_specs=(aux_spec, aux_spec),
        compiler_params=_params(("parallel",)),
        name="fox_gate",
    )(c, b_pad)


def _moba_query_aux(q, kmean_ref, qconst, slope2, blk):
    seq = q.shape[0]
    g = lax.dot_general(kmean_ref[...], q.astype(F32), (((1,), (1,)), ((), ())),
                        precision=lax.Precision.HIGHEST, preferred_element_type=F32)
    row = lax.broadcasted_iota(jnp.int32, g.shape, 0)
    bid = row & (MOBA_MAX_BLOCKS - 1)
    qblk = lax.broadcasted_iota(jnp.int32, g.shape, 1) // blk
    rank = jnp.zeros(g.shape, F32)
    for n in range(seq // blk - 1):
        gn = g[n:n + 1, :]
        beats = ((gn > g) | ((gn == g) & (n < bid))) & (n < qblk)
        rank = rank + jnp.where(beats, 1.0, 0.0)
    past = bid < qblk
    chosen = (rank < MOBA_TOPK) & past
    offset = (qblk - bid).astype(F32) * (-slope2 * blk)
    term = jnp.where(chosen, offset, jnp.where(past, NEG_INF, 0.0))
    hi = term.astype(MXU_DTYPE).astype(F32)
    lo = jnp.where(chosen, term - hi, 0.0)
    aux_t = jnp.where(row < MOBA_MAX_BLOCKS, hi, lo)
    aux = jnp.concatenate([aux_t, jnp.zeros((LANE - aux_t.shape[0], seq), F32)], axis=0).T
    return [(aux[r0:r0 + blk, :] + qconst).astype(MXU_DTYPE) for r0 in range(0, seq, blk)]


def _attention_kernel(*refs, moba, k2_shared):
    if moba:
        slopes_ref, q1_ref, k1_ref, v_ref, k2_ref, qc_ref, o_ref, kk_ref, vv_ref, kmean_ref = refs
    else:
        q1_ref, q2_ref, k1_ref, k2_ref, v_ref, o_ref, kk_ref, vv_ref = refs
    heads, seq, _ = k1_ref.shape
    tq = ATTN_BLOCK
    q2_tiles = []
    for hh in range(heads):
        kk_ref[hh, :, :HEAD_DIM] = k1_ref[hh]
        kk_ref[hh, :, HEAD_DIM:] = k2_ref[...] if k2_shared else k2_ref[hh]
        vv_ref[hh, :, :HEAD_DIM] = v_ref[hh]
        vv_ref[hh, :, HEAD_DIM:] = jnp.ones((seq, HEAD_DIM), vv_ref.dtype)
        if moba:
            slope2 = slopes_ref[pl.program_id(1) * heads + hh] * LOG2E
            kmean_ref[hh] = jnp.zeros(kmean_ref.shape[1:], F32)
            for n in range(seq // tq):
                mean = jnp.mean(k1_ref[hh, n * tq:(n + 1) * tq, :].astype(F32), axis=0, keepdims=True)
                kmean_ref[hh, n:n + 1, :] = mean
                kmean_ref[hh, MOBA_MAX_BLOCKS + n:MOBA_MAX_BLOCKS + n + 1, :] = mean
            q2_tiles.append(_moba_query_aux(q1_ref[hh], kmean_ref.at[hh], qc_ref[hh], slope2, tq))
    r = lax.broadcasted_iota(jnp.int32, (tq, tq), 0)
    c = lax.broadcasted_iota(jnp.int32, (tq, tq), 1)
    for qi in reversed(range(seq // tq)):
        rows = slice(qi * tq, (qi + 1) * tq)
        n = (qi + 1) * tq
        for hh in range(heads):
            q1 = q1_ref[hh, rows, :]
            q2 = q2_tiles[hh][qi] if moba else q2_ref[hh, rows, :]
            s = lax.dot_general(jnp.concatenate([q1, q2], axis=1), kk_ref[hh, :n, :],
                                (((1,), (1,)), ((), ())), preferred_element_type=F32)
            diag = jnp.where(c <= r, s[:, n - tq:], NEG_INF)
            s = diag if qi == 0 else jnp.concatenate([s[:, :n - tq], diag], axis=1)
            m = jnp.max(s, axis=-1, keepdims=True)
            p = jnp.exp2(s - m).astype(vv_ref.dtype)
            acc = jnp.dot(p, vv_ref[hh, :n, :], preferred_element_type=F32)
            o_ref[hh, rows, :] = (acc[:, :HEAD_DIM] / acc[:, HEAD_DIM:]).astype(o_ref.dtype)


def _attention(q1, q2, k1, k2, v, batch, seq, n_heads, q1_group=0, k1_group=0, v_group=0,
               k2_shared=False, moba_slopes=None):
    assert seq % ATTN_BLOCK == 0
    t = batch * seq
    moba = moba_slopes is not None
    hp = math.gcd(n_heads, ATTN_HEADS_PER_STEP)
    assert all(g % hp == 0 for g in (q1_group, k1_group, v_group))
    head = lambda g: pl.BlockSpec((hp, seq, HEAD_DIM), lambda b, h: (g // hp + h, b, 0))
    if moba:
        assert seq // ATTN_BLOCK <= MOBA_MAX_BLOCKS and ATTN_BLOCK == MOBA_BLOCK
        in_specs = [pl.BlockSpec(memory_space=pltpu.SMEM), head(q1_group), head(k1_group), head(v_group),
                    pl.BlockSpec((hp, seq, LANE), lambda b, h: (h, 0, 0)),
                    pl.BlockSpec((hp, ATTN_BLOCK, LANE), lambda b, h: (h, 0, 0))]
        args = (moba_slopes, q1, k1, v, k2, q2)
        scratch = [pltpu.VMEM((hp, 2 * MOBA_MAX_BLOCKS, HEAD_DIM), F32)]
    else:
        k2_spec = pl.BlockSpec((seq, LANE), lambda b, h: (b, 0)) if k2_shared else head(0)
        in_specs = [head(q1_group), head(0), head(k1_group), k2_spec, head(v_group)]
        args = (q1, q2, k1, k2, v)
        scratch = []
    return pl.pallas_call(
        functools.partial(_attention_kernel, moba=moba, k2_shared=k2_shared),
        out_shape=jax.ShapeDtypeStruct((n_heads, t, HEAD_DIM), MXU_DTYPE),
        grid=(batch, n_heads // hp),
        in_specs=in_specs,
        out_specs=head(0),
        scratch_shapes=[pltpu.VMEM((hp, seq, 2 * HEAD_DIM), MXU_DTYPE),
                        pltpu.VMEM((hp, seq, 2 * HEAD_DIM), MXU_DTYPE)] + scratch,
        compiler_params=_params(("parallel", "parallel")),
        name="moba_attention" if moba else "causal_attention",
    )(*args)


def _moba_constants(seq, n_heads):
    slopes = jnp.exp2(-8.0 * jnp.arange(1, n_heads + 1, dtype=F32) / n_heads)
    pieces = jnp.stack(_split3(slopes * LOG2E), axis=-1)
    pos = jnp.arange(seq)
    onehot = (pos[:, None] // MOBA_BLOCK == jnp.arange(MOBA_MAX_BLOCKS)[None, :]).astype(F32)
    within = jnp.broadcast_to((pos % MOBA_BLOCK).astype(F32)[None, :, None], (n_heads, seq, 3))
    k_aux = jnp.concatenate([jnp.broadcast_to(onehot, (n_heads, seq, MOBA_MAX_BLOCKS))] * 2
                            + [jnp.broadcast_to(-pieces[:, None, :], (n_heads, seq, 3)), within], axis=-1)
    q_aux = jnp.concatenate([jnp.zeros((n_heads, MOBA_BLOCK, 2 * MOBA_MAX_BLOCKS), F32),
                             within[:, :MOBA_BLOCK], jnp.broadcast_to(pieces[:, None, :], (n_heads, MOBA_BLOCK, 3))],
                            axis=-1)
    return slopes, _pad_last(k_aux, LANE).astype(MXU_DTYPE), _pad_last(q_aux, LANE)


def _sgu_kernel(u_ref, v_ref, lg_ref, lb_ref, w_ref, b_ref, o_ref):
    tm = u_ref.shape[0]
    nc = tm // SGU_CHUNK
    v = jax.nn.gelu(v_ref[...].astype(F32))
    vc = v - jnp.mean(v, axis=-1, keepdims=True)
    var = jnp.mean(vc * vc, axis=-1, keepdims=True)
    vn = (vc * lax.rsqrt(var + NORM_EPS) * lg_ref[...] + lb_ref[...]).astype(MXU_DTYPE)
    v_side = jnp.concatenate([vn[c * SGU_CHUNK:(c + 1) * SGU_CHUNK, :] for c in range(nc)], axis=1)
    r = lax.broadcasted_iota(jnp.int32, (SGU_CHUNK, SGU_CHUNK), 0)
    c_ix = lax.broadcasted_iota(jnp.int32, (SGU_CHUNK, SGU_CHUNK), 1)
    w = jnp.where(c_ix <= r, w_ref[...], 0.0).astype(MXU_DTYPE)
    mixed = jnp.dot(w, v_side, preferred_element_type=F32) + b_ref[...]
    u = jax.nn.gelu(u_ref[...].astype(F32))
    for c in range(nc):
        rows = slice(c * SGU_CHUNK, (c + 1) * SGU_CHUNK)
        o_ref[rows, :] = (u[rows, :] * mixed[:, c * HEAD_DIM:(c + 1) * HEAD_DIM]).astype(o_ref.dtype)


def _sgu(proj, ln_g, ln_b, w_s, b_s, n_heads, u_group, v_group):
    t = proj.shape[1]
    tm = min(2048, t)
    vec = pl.BlockSpec((None, 1, HEAD_DIM), lambda g, i: (g, 0, 0))
    return pl.pallas_call(
        _sgu_kernel,
        out_shape=jax.ShapeDtypeStruct((n_heads, t, HEAD_DIM), MXU_DTYPE),
        grid=(n_heads, t // tm),
        in_specs=[pl.BlockSpec((None, tm, HEAD_DIM), lambda g, i: (u_group + g, i, 0)),
                  pl.BlockSpec((None, tm, HEAD_DIM), lambda g, i: (v_group + g, i, 0)),
                  vec, vec,
                  pl.BlockSpec((None, SGU_CHUNK, SGU_CHUNK), lambda g, i: (g, 0, 0)),
                  pl.BlockSpec((None, SGU_CHUNK, 1), lambda g, i: (g, 0, 0))],
        out_specs=pl.BlockSpec((None, tm, HEAD_DIM), lambda g, i: (g, i, 0)),
        compiler_params=_params(("parallel", "parallel")),
        name="spatial_gating",
    )(proj, proj, ln_g[:, None, :], ln_b[:, None, :], w_s, b_s[:, :, None])


def _out_proj_kernel(*refs, n_mix):
    y_refs, (gn_ref, w_ref, resid_ref, o_ref, ob_ref, op_ref, a_ref) = refs[:n_mix], refs[n_mix:]
    per = y_refs[0].shape[0]
    width = per * LANE
    tm = a_ref.shape[0]

    @pl.when(pl.program_id(1) == 0)
    def _():
        for m, y_ref in enumerate(y_refs):
            for r0 in range(0, tm, NORM_ROW_CHUNK):
                rows = slice(r0, min(r0 + NORM_ROW_CHUNK, tm))
                parts = [y_ref[i, rows, :].astype(F32) for i in range(per)]
                ssq = sum(jnp.sum(p * p, axis=-1, keepdims=True) for p in parts)
                rstd = lax.rsqrt(ssq / width + NORM_EPS)
                for i, p in enumerate(parts):
                    lo = m * width + i * LANE
                    a_ref[rows, lo:lo + LANE] = (p * rstd * gn_ref[m:m + 1, i * LANE:(i + 1) * LANE]).astype(a_ref.dtype)

    x_new = resid_ref[...] + jnp.dot(a_ref[...], w_ref[...], preferred_element_type=F32)
    o_ref[...] = x_new
    ob_ref[...] = x_new.astype(ob_ref.dtype)
    sq = x_new * x_new
    op_ref[...] = sum(sq[:, c0:c0 + LANE] for c0 in range(0, sq.shape[1], LANE))


def _out_proj(ys, group_gain, w_stack, layer, resid):
    t, n = resid.shape
    n_mix, width = group_gain.shape
    per = width // LANE
    tm, bn = min(1024, t), min(512, n)
    y_spec = pl.BlockSpec((per, tm, LANE), lambda i, j: (0, i, 0))
    tile = pl.BlockSpec((tm, bn), lambda i, j: (i, j))
    return pl.pallas_call(
        functools.partial(_out_proj_kernel, n_mix=n_mix),
        out_shape=(jax.ShapeDtypeStruct((t, n), F32), jax.ShapeDtypeStruct((t, n), MXU_DTYPE),
                   jax.ShapeDtypeStruct((t, n // bn * LANE), F32)),
        grid=(t // tm, n // bn),
        in_specs=[y_spec] * n_mix + [pl.BlockSpec((n_mix, width), lambda i, j: (0, 0)),
                                     pl.BlockSpec((None, n_mix * width, bn), lambda i, j: (layer, 0, j)),
                                     tile],
        out_specs=(tile, tile, pl.BlockSpec((tm, LANE), lambda i, j: (i, j))),
        scratch_shapes=[pltpu.VMEM((tm, n_mix * width), MXU_DTYPE)],
        compiler_params=_params(("parallel", "arbitrary")),
        name="out_proj",
    )(*ys, group_gain, w_stack, resid)


def _down_proj_norm_kernel(*refs, keep_x, k_total):
    if not keep_x:
        refs = refs + (refs[-1],)
    a_ref, w_ref, resid_ref, g_ref, ox_ref, oh_ref = refs
    k = pl.program_id(1)
    last = pl.num_programs(1) - 1
    tm, n = ox_ref.shape
    bk = a_ref.shape[1]
    k_rem = k_total % bk

    @pl.when(k == 0)
    def _():
        ox_ref[...] = resid_ref[...]

    def accumulate(depth):
        for c0 in range(0, n, PROJ_COL_CHUNK):
            cols = slice(c0, min(c0 + PROJ_COL_CHUNK, n))
            ox_ref[:, cols] += jnp.dot(a_ref[:, :depth], w_ref[:depth, cols], preferred_element_type=F32)

    if k_rem:
        pl.when(k < last)(lambda: accumulate(bk))
        pl.when(k == last)(lambda: accumulate(k_rem))
    else:
        accumulate(bk)

    @pl.when(k == last)
    def _():
        for r0 in range(0, tm, NORM_ROW_CHUNK):
            rows = slice(r0, min(r0 + NORM_ROW_CHUNK, tm))
            oh_ref[rows, :] = _rmsnorm_rows(ox_ref[rows, :], g_ref[...]).astype(oh_ref.dtype)


def _down_proj_norm(a, w_stack, layer, resid, g, h_dtype, keep_x=True):
    t, n = resid.shape
    k_total = a.shape[1]
    tm = min(512, t)
    bk = min(1024, k_total)
    row_block = pl.BlockSpec((tm, n), lambda i, k: (i, 0))
    h_shape = jax.ShapeDtypeStruct((t, n), h_dtype)
    assert keep_x or h_dtype == F32
    out_shape = (jax.ShapeDtypeStruct((t, n), F32), h_shape) if keep_x else h_shape
    return pl.pallas_call(
        functools.partial(_down_proj_norm_kernel, keep_x=keep_x, k_total=k_total),
        out_shape=out_shape,
        grid=(t // tm, pl.cdiv(k_total, bk)),
        in_specs=[pl.BlockSpec((tm, bk), lambda i, k: (i, k)),
                  pl.BlockSpec((None, bk, n), lambda i, k: (layer, k, 0)),
                  pl.BlockSpec((tm, n), lambda i, k: (i, 0), pipeline_mode=pl.Buffered(1)),
                  pl.BlockSpec((1, n), lambda i, k: (0, 0))],
        out_specs=(row_block, row_block) if keep_x else row_block,
        compiler_params=_params(("parallel", "arbitrary")),
        name="down_proj_norm",
    )(a, w_stack, resid, g.reshape(1, n))


def _ffn_up_kernel(h_ref, halo_ref, p_ref, phalo_ref, wg_ref, wv_ref, cw_ref, cb_ref, o_ref, hext_ref,
                   rstd_ref, *, blocks_per_seq):
    i, j = pl.program_id(0), pl.program_id(1)
    tm, d = h_ref.shape

    def inv_rms(p):
        r = lax.rsqrt(jnp.sum(p, axis=-1, keepdims=True) / d + NORM_EPS)
        return jnp.broadcast_to(r, (p.shape[0], LANE))

    @pl.when(j == 0)
    def _():
        halo = halo_ref[...]
        first = i % blocks_per_seq == 0
        hext_ref[:CONV_HALO, :] = jnp.where(first, jnp.zeros_like(halo), halo)
        hext_ref[CONV_HALO:, :] = h_ref[...]
        rstd_ref[:CONV_HALO, :] = inv_rms(phalo_ref[...])
        rstd_ref[CONV_HALO:, :] = inv_rms(p_ref[...])

    def scale_rows(m, r):
        return jnp.concatenate([m[:, c0:c0 + LANE] * r for c0 in range(0, m.shape[1], LANE)], axis=1)

    g = jnp.dot(hext_ref[...], wg_ref[...], preferred_element_type=F32)
    g = scale_rows(g, rstd_ref[...])
    val = scale_rows(jnp.dot(h_ref[...], wv_ref[...], preferred_element_type=F32), rstd_ref[CONV_HALO:, :])
    cw = cw_ref[...]
    gate = cb_ref[...]
    for tap in range(CONV_WIDTH):
        lo = CONV_HALO - (CONV_WIDTH - 1) + tap
        gate = gate + cw[tap:tap + 1, :] * g[lo:lo + tm, :]
    o_ref[...] = (gate * jax.nn.sigmoid(gate) * val).astype(o_ref.dtype)


def _ffn_up(h, ssq, wg_stack, wv_stack, conv_w, conv_b, layer, seq):
    t, d = h.shape
    n_p = ssq.shape[1]
    n = wg_stack.shape[-1]
    tm = min(1024, seq)
    bn = min(512, n)
    per_seq = seq // tm
    halo_blocks = tm // CONV_HALO
    return pl.pallas_call(
        functools.partial(_ffn_up_kernel, blocks_per_seq=per_seq),
        out_shape=jax.ShapeDtypeStruct((t, n), MXU_DTYPE),
        grid=(t // tm, pl.cdiv(n, bn)),
        in_specs=[pl.BlockSpec((tm, d), lambda i, j: (i, 0)),
                  pl.BlockSpec((CONV_HALO, d), lambda i, j: (jnp.maximum(i * halo_blocks - 1, 0), 0)),
                  pl.BlockSpec((tm, n_p), lambda i, j: (i, 0)),
                  pl.BlockSpec((CONV_HALO, n_p), lambda i, j: (jnp.maximum(i * halo_blocks - 1, 0), 0)),
                  pl.BlockSpec((None, d, bn), lambda i, j: (layer, 0, j)),
                  pl.BlockSpec((None, d, bn), lambda i, j: (layer, 0, j)),
                  pl.BlockSpec((None, CONV_WIDTH, bn), lambda i, j: (layer, 0, j)),
                  pl.BlockSpec((None, 1, bn), lambda i, j: (layer, 0, j))],
        out_specs=pl.BlockSpec((tm, bn), lambda i, j: (i, j)),
        scratch_shapes=[pltpu.VMEM((tm + CONV_HALO, d), MXU_DTYPE), pltpu.VMEM((tm + CONV_HALO, LANE), F32)],
        compiler_params=_params(("parallel", "arbitrary")),
        name="ffn_up_conv",
    )(h, h, ssq, ssq, wg_stack, wv_stack, conv_w, conv_b)


def _pad_last(w, width):
    return jnp.pad(w, [(0, 0)] * (w.ndim - 1) + [(0, width - w.shape[-1])])


def _rotate_half_columns(w):
    half = ROPE_DIM // 2
    return jnp.concatenate([-w[..., half:], w[..., :half]], axis=-1)


def _pack_weights(w_in, mla_w_uq, mla_w_ukv, w_o, norm_ffn_g, w_gate, w_val, conv_b, w_down,
                  group_width, q_rank, kv_rank, n_heads):
    gw = group_width
    a_end = 3 * gw
    b_end = a_end + 2 * gw
    cq_end = b_end + q_rank
    ckv_end = cq_end + kv_rank
    kr_end = ckv_end + ROPE_DIM
    d_end = kr_end + 3 * gw
    layers, qr, _ = mla_w_uq.shape
    head_segments = ((0, b_end), (kr_end, 3 * gw))
    half = ROPE_DIM // 2
    w_in_t, w_latent = [], []
    for l in range(layers):
        wt = jnp.swapaxes(w_in[l], 0, 1).astype(MXU_DTYPE)
        kr = wt[ckv_end:kr_end]
        forget = wt[d_end:]
        w_in_t.append(wt)
        w_latent.append(jnp.concatenate(
            [wt[b_end:ckv_end], kr, -kr[half:], kr[:half],
             jnp.pad(forget, ((0, LANE - forget.shape[0]), (0, 0)))], axis=0))

    uq = mla_w_uq.reshape(layers, qr, n_heads, HEAD_DIM + ROPE_DIM)
    nope, rope = uq[..., :HEAD_DIM], uq[..., HEAD_DIM:]
    flat = lambda w: w.reshape(layers, w.shape[1], n_heads * HEAD_DIM)
    wq = jnp.concatenate([flat(nope), flat(jnp.concatenate([rope, _rotate_half_columns(rope)], axis=-1))],
                         axis=-1).astype(MXU_DTYPE)
    ukv = mla_w_ukv.reshape(layers, kv_rank, n_heads, 2 * HEAD_DIM)
    wkv = jnp.concatenate([flat(ukv[..., :HEAD_DIM]), flat(ukv[..., HEAD_DIM:])], axis=-1).astype(MXU_DTYPE)

    ffn_gain = norm_ffn_g[:, :, None]
    return (w_in_t, head_segments, w_latent, wq, wkv, w_o.astype(MXU_DTYPE),
            (w_gate * ffn_gain).astype(MXU_DTYPE), (w_val * ffn_gain).astype(MXU_DTYPE),
            w_down.astype(MXU_DTYPE), conv_b[:, None, :])


def _rope_tables(seq):
    half = ROPE_DIM // 2
    inv_freq = ROPE_THETA ** (-jnp.arange(half, dtype=F32) / half)
    ang = jnp.arange(seq, dtype=F32)[:, None] * inv_freq[None, :]
    cos = jnp.concatenate([jnp.cos(ang), jnp.cos(ang)], axis=-1)
    sin = jnp.concatenate([jnp.sin(ang), jnp.sin(ang)], axis=-1)
    return _pad_last(cos, LANE), _pad_last(sin, LANE)


def kernel(x, norm_mix_g, w_in, sgu_ln_g, sgu_ln_b, sgu_w, sgu_b, mla_q_norm_g, mla_kv_norm_g,
           mla_w_uq, mla_w_ukv, fox_b_f, group_norm_g, w_o, norm_ffn_g, w_gate, w_val, conv_w,
           conv_b, w_down, final_norm_g):
    batch, seq, d_model = x.shape
    depth = w_in.shape[0]
    n_heads = fox_b_f.shape[-1]
    gw = group_norm_g.shape[-1]
    q_rank, kv_rank = mla_q_norm_g.shape[-1], mla_kv_norm_g.shape[-1]
    assert gw == n_heads * HEAD_DIM and seq % MOBA_BLOCK == 0
    assert q_rank % LANE == 0 and kv_rank % LANE == 0

    (w_in_t, head_segments, w_latent, wq, wkv, wo, wg, wv, wd, cb) = _pack_weights(
        w_in, mla_w_uq, mla_w_ukv, w_o, norm_ffn_g, w_gate, w_val, conv_b, w_down, gw, q_rank, kv_rank, n_heads)
    cos, sin = _rope_tables(seq)
    slopes, moba_k_aux, moba_q_aux = _moba_constants(seq, n_heads)
    f_col_block = (q_rank + kv_rank + LANE) // LANE
    mla_scale = (HEAD_DIM + ROPE_DIM) ** -0.5 * LOG2E
    a_q, a_k, a_v, b_u, b_v, d_q, d_k, d_v = (i * n_heads for i in range(8))
    q_scale, q_starts = HEAD_DIM ** -0.5 * LOG2E, (a_q * HEAD_DIM, d_q * HEAD_DIM)

    xf = x.reshape(batch * seq, d_model)
    h = _rmsnorm(xf, norm_mix_g[0], MXU_DTYPE)
    for l in range(depth):
        proj = _matmul_groups(h, w_in_t[l], 1024, min(1024, gw), head_segments, q_starts, q_scale)
        latent = _matmul_f32(h, w_latent[l], 512)
        qn, qr, kn, vv, kr = _mla_up(latent, mla_q_norm_g[l], mla_kv_norm_g[l], wq, wkv, cos, sin,
                                     l, n_heads, seq, mla_scale)
        fox_q_aux, fox_k_aux = _fox_gate(latent, fox_b_f[l], f_col_block, batch, seq, n_heads)
        y_a = _attention(proj, moba_q_aux, proj, moba_k_aux, proj, batch, seq, n_heads,
                         a_q, a_k, a_v, moba_slopes=slopes)
        y_b = _sgu(proj, sgu_ln_g[l], sgu_ln_b[l], sgu_w[l], sgu_b[l], n_heads, b_u, b_v)
        y_c = _attention(qn, qr, kn, kr, vv, batch, seq, n_heads, k2_shared=True)
        y_d = _attention(proj, fox_q_aux, proj, fox_k_aux, proj, batch, seq, n_heads, d_q, d_k, d_v)
        xf, xb, ssq = _out_proj((y_a, y_b, y_c, y_d), group_norm_g[l], wo, l, xf)
        act = _ffn_up(xb, ssq, wg, wv, conv_w, cb, l, seq)
        if l == depth - 1:
            h = _down_proj_norm(act, wd, l, xf, final_norm_g, F32, keep_x=False)
        else:
            xf, h = _down_proj_norm(act, wd, l, xf, norm_mix_g[l + 1], MXU_DTYPE)
    return h.reshape(batch, seq, d_model)
```

```python
import functools
import math

import jax
import jax.numpy as jnp
from jax import lax
from jax.experimental import pallas as pl
from jax.experimental.pallas import tpu as pltpu

F32 = jnp.float32
MXU_DTYPE = jnp.bfloat16

LANE = 128
HEAD_DIM = 128
MOBA_BLOCK = 256
MOBA_TOPK = 3
SGU_CHUNK = 128
ROPE_DIM = 64
ROPE_THETA = 10000.0
CONV_WIDTH = 3
NORM_EPS = 1e-6
NEG_INF = -1e30
LOG2E = math.log2(math.e)

ATTN_BLOCK = 256
MOBA_MAX_BLOCKS = 8
ATTN_HEADS_PER_STEP = 4
CONV_HALO = 16
PROJ_COL_CHUNK = 512
NORM_ROW_CHUNK = 128
VMEM_LIMIT_BYTES = 56 * 1024 * 1024


def _params(semantics):
    return pltpu.CompilerParams(dimension_semantics=semantics, vmem_limit_bytes=VMEM_LIMIT_BYTES)


def _rmsnorm_rows(x, g):
    ms = jnp.mean(x * x, axis=-1, keepdims=True)
    return x * lax.rsqrt(ms + NORM_EPS) * g


def _split3(x):
    hi = x.astype(MXU_DTYPE).astype(F32)
    r = x - hi
    mid = r.astype(MXU_DTYPE).astype(F32)
    lo = (r - mid).astype(MXU_DTYPE).astype(F32)
    return hi, mid, lo


def _norm_kernel(x_ref, g_ref, o_ref):
    o_ref[...] = _rmsnorm_rows(x_ref[...], g_ref[...]).astype(o_ref.dtype)


def _rmsnorm(x, g, out_dtype):
    t, d = x.shape
    tm = min(512, t)
    return pl.pallas_call(
        _norm_kernel,
        out_shape=jax.ShapeDtypeStruct((t, d), out_dtype),
        grid=(t // tm,),
        in_specs=[pl.BlockSpec((tm, d), lambda i: (i, 0)),
                  pl.BlockSpec((1, d), lambda i: (0, 0))],
        out_specs=pl.BlockSpec((tm, d), lambda i: (i, 0)),
        compiler_params=_params(("parallel",)),
        name="rmsnorm",
    )(x, g.reshape(1, d))


def _dot_nt(a, w_t):
    return lax.dot_general(a, w_t, (((1,), (1,)), ((), ())), preferred_element_type=F32)


def _mm_groups_kernel(a_ref, w_ref, o_ref, *, scaled_blocks, scale):
    acc = _dot_nt(a_ref[...], w_ref[...])
    j = pl.program_id(1)
    is_scaled = functools.reduce(jnp.logical_or, [j == b for b in scaled_blocks])
    acc = acc * jnp.where(is_scaled, scale, 1.0)
    for g in range(o_ref.shape[0]):
        o_ref[g] = acc[:, g * LANE:(g + 1) * LANE].astype(o_ref.dtype)


def _matmul_groups(a, w_t, tm, bn, segments, scaled_starts, scale):
    t, k = a.shape
    tm = min(tm, t)
    assert all(rows % bn == 0 for _, rows in segments) and all(s0 % bn == 0 for s0 in scaled_starts)
    n = sum(rows for _, rows in segments)

    def row_start(j):
        r = j * bn
        shift, done = 0, 0
        for r0, rows in segments:
            r = r + jnp.where(j >= done // bn, (r0 - done) - shift, 0)
            shift, done = r0 - done, done + rows
        return pl.multiple_of(r, math.gcd(bn, *(r0 for r0, _ in segments)))

    return pl.pallas_call(
        functools.partial(_mm_groups_kernel, scaled_blocks=[s0 // bn for s0 in scaled_starts], scale=scale),
        out_shape=jax.ShapeDtypeStruct((n // LANE, t, LANE), MXU_DTYPE),
        grid=(t // tm, n // bn),
        in_specs=[pl.BlockSpec((tm, k), lambda i, j: (i, 0)),
                  pl.BlockSpec((pl.Element(bn), pl.Element(k)), lambda i, j: (row_start(j), 0))],
        out_specs=pl.BlockSpec((bn // LANE, tm, LANE), lambda i, j: (j, i, 0)),
        compiler_params=_params(("parallel", "arbitrary")),
        name="in_proj_heads",
    )(a, w_t)


def _mm_plain_kernel(a_ref, w_ref, o_ref):
    o_ref[...] = _dot_nt(a_ref[...], w_ref[...]).astype(o_ref.dtype)


def _matmul_f32(a, w_t, tm):
    t, k = a.shape
    n = w_t.shape[0]
    tm = min(tm, t)
    return pl.pallas_call(
        _mm_plain_kernel,
        out_shape=jax.ShapeDtypeStruct((t, n), F32),
        grid=(t // tm,),
        in_specs=[pl.BlockSpec((tm, k), lambda i: (i, 0)),
                  pl.BlockSpec((n, k), lambda i: (0, 0))],
        out_specs=pl.BlockSpec((tm, n), lambda i: (i, 0)),
        compiler_params=_params(("parallel",)),
        name="in_proj_latent",
    )(a, w_t)


def _mla_up_kernel(c_ref, gq_ref, gkv_ref, wq_ref, wkv_ref, cos_ref, sin_ref,
                   qn_ref, qr_ref, kn_ref, v_ref, kr_ref, *, q_rank, kv_rank, scale):
    n_heads = qn_ref.shape[0]
    hw = n_heads * HEAD_DIM
    c = c_ref[...]
    cq = _rmsnorm_rows(c[:, :q_rank], gq_ref[...]).astype(MXU_DTYPE)
    ckv = _rmsnorm_rows(c[:, q_rank:q_rank + kv_rank], gkv_ref[...]).astype(MXU_DTYPE)
    q = jnp.dot(cq, wq_ref[...], preferred_element_type=F32)
    kv = jnp.dot(ckv, wkv_ref[...], preferred_element_type=F32)
    cos, sin = cos_ref[...], sin_ref[...]

    def rope(pair):
        return pair * cos + pltpu.roll(pair, ROPE_DIM, axis=1) * sin

    for h in range(n_heads):
        lo, hi = h * HEAD_DIM, (h + 1) * HEAD_DIM
        qn_ref[h] = (q[:, lo:hi] * scale).astype(qn_ref.dtype)
        qr_ref[h] = (rope(q[:, hw + lo:hw + hi]) * scale).astype(qr_ref.dtype)
        kn_ref[h] = kv[:, lo:hi].astype(kn_ref.dtype)
        v_ref[h] = kv[:, hw + lo:hw + hi].astype(v_ref.dtype)
    o = q_rank + kv_rank
    kr_ref[...] = rope(c[:, o:o + LANE]).astype(kr_ref.dtype)


def _mla_up(c, gq, gkv, wq_stack, wkv_stack, cos, sin, layer, n_heads, seq, scale):
    t, cw = c.shape
    q_rank, kv_rank = gq.shape[-1], gkv.shape[-1]
    tm = min(512, seq)
    per_seq = seq // tm
    heads = jax.ShapeDtypeStruct((n_heads, t, LANE), MXU_DTYPE)
    head_spec = pl.BlockSpec((n_heads, tm, LANE), lambda i: (0, i, 0))
    return pl.pallas_call(
        functools.partial(_mla_up_kernel, q_rank=q_rank, kv_rank=kv_rank, scale=scale),
        out_shape=(heads, heads, heads, heads, jax.ShapeDtypeStruct((t, LANE), MXU_DTYPE)),
        grid=(t // tm,),
        in_specs=[pl.BlockSpec((tm, cw), lambda i: (i, 0)),
                  pl.BlockSpec((1, q_rank), lambda i: (0, 0)),
                  pl.BlockSpec((1, kv_rank), lambda i: (0, 0)),
                  pl.BlockSpec((None,) + wq_stack.shape[1:], lambda i: (layer, 0, 0)),
                  pl.BlockSpec((None,) + wkv_stack.shape[1:], lambda i: (layer, 0, 0)),
                  pl.BlockSpec((tm, LANE), lambda i: (i % per_seq, 0)),
                  pl.BlockSpec((tm, LANE), lambda i: (i % per_seq, 0))],
        out_specs=(head_spec, head_spec, head_spec, head_spec,
                   pl.BlockSpec((tm, LANE), lambda i: (i, 0))),
        compiler_params=_params(("parallel",)),
        name="mla_up",
    )(c, gq.reshape(1, q_rank), gkv.reshape(1, kv_rank), wq_stack, wkv_stack, cos, sin)


def _fox_gate_kernel(f_ref, b_ref, qa_ref, ka_ref):
    n_heads = qa_ref.shape[0]
    z = f_ref[...] + b_ref[...]
    x = -(jnp.maximum(-z, 0.0) + jnp.log1p(jnp.exp(-jnp.abs(z))))
    s = z.shape[0]
    rows = lax.broadcasted_iota(jnp.int32, z.shape, 0)
    shift = 1
    while shift < s:
        x = x + jnp.where(rows >= shift, pltpu.roll(x, shift, axis=0), 0.0)
        shift *= 2
    x = x * LOG2E
    lane = lax.broadcasted_iota(jnp.int32, z.shape, 1)
    ones = jnp.where(lane < 3, 1.0, 0.0)
    for h in range(n_heads):
        hi, mid, lo = _split3(jnp.broadcast_to(x[:, h:h + 1], x.shape))
        pieces = jnp.where(lane == 0, hi, jnp.where(lane == 1, mid, jnp.where(lane == 2, lo, 0.0)))
        qa_ref[h] = (pieces + pltpu.roll(ones, 3, axis=1)).astype(qa_ref.dtype)
        ka_ref[h] = (ones - pltpu.roll(pieces, 3, axis=1)).astype(ka_ref.dtype)


def _fox_gate(c, b_f, f_col_block, batch, seq, n_heads):
    b_pad = jnp.zeros((1, LANE), F32).at[0, :n_heads].set(b_f)
    aux = jax.ShapeDtypeStruct((n_heads, batch * seq, LANE), MXU_DTYPE)
    aux_spec = pl.BlockSpec((n_heads, seq, LANE), lambda b: (0, b, 0))
    return pl.pallas_call(
        _fox_gate_kernel,
        out_shape=(aux, aux),
        grid=(batch,),
        in_specs=[pl.BlockSpec((seq, LANE), lambda b: (b, f_col_block)),
                  pl.BlockSpec((1, LANE), lambda b: (0, 0))],
        out_specs=(aux_spec, aux_spec),
        compiler_params=_params(("parallel",)),
        name="fox_gate",
    )(c, b_pad)


def _moba_query_aux(q, kmean_ref, qconst, slope2, blk):
    seq = q.shape[0]
    g = lax.dot_general(kmean_ref[...], q.astype(F32), (((1,), (1,)), ((), ())),
                        precision=lax.Precision.HIGHEST, preferred_element_type=F32)
    row = lax.broadcasted_iota(jnp.int32, g.shape, 0)
    bid = row & (MOBA_MAX_BLOCKS - 1)
    qblk = lax.broadcasted_iota(jnp.int32, g.shape, 1) // blk
    rank = jnp.zeros(g.shape, F32)
    for n in range(seq // blk - 1):
        gn = g[n:n + 1, :]
        beats = ((gn > g) | ((gn == g) & (n < bid))) & (n < qblk)
        rank = rank + jnp.where(beats, 1.0, 0.0)
    past = bid < qblk
    chosen = (rank < MOBA_TOPK) & past
    offset = (qblk - bid).astype(F32) * (-slope2 * blk)
    term = jnp.where(chosen, offset, jnp.where(past, NEG_INF, 0.0))
    hi = term.astype(MXU_DTYPE).astype(F32)
    lo = jnp.where(chosen, term - hi, 0.0)
    aux_t = jnp.where(row < MOBA_MAX_BLOCKS, hi, lo)
    aux = jnp.concatenate([aux_t, jnp.zeros((LANE - aux_t.shape[0], seq), F32)], axis=0).T
    return [(aux[r0:r0 + blk, :] + qconst).astype(MXU_DTYPE) for r0 in range(0, seq, blk)]


def _attention_kernel(*refs, moba, k2_shared):
    if moba:
        slopes_ref, q1_ref, k1_ref, v_ref, k2_ref, qc_ref, o_ref, kk_ref, vv_ref, kmean_ref = refs
    else:
        q1_ref, q2_ref, k1_ref, k2_ref, v_ref, o_ref, kk_ref, vv_ref = refs
    heads, seq, _ = k1_ref.shape
    tq = ATTN_BLOCK
    q2_tiles = []
    for hh in range(heads):
        kk_ref[hh, :, :HEAD_DIM] = k1_ref[hh]
        kk_ref[hh, :, HEAD_DIM:] = k2_ref[...] if k2_shared else k2_ref[hh]
        vv_ref[hh, :, :HEAD_DIM] = v_ref[hh]
        vv_ref[hh, :, HEAD_DIM:] = jnp.ones((seq, HEAD_DIM), vv_ref.dtype)
        if moba:
            slope2 = slopes_ref[pl.program_id(1) * heads + hh] * LOG2E
            kmean_ref[hh] = jnp.zeros(kmean_ref.shape[1:], F32)
            for n in range(seq // tq):
                mean = jnp.mean(k1_ref[hh, n * tq:(n + 1) * tq, :].astype(F32), axis=0, keepdims=True)
                kmean_ref[hh, n:n + 1, :] = mean
                kmean_ref[hh, MOBA_MAX_BLOCKS + n:MOBA_MAX_BLOCKS + n + 1, :] = mean
            q2_tiles.append(_moba_query_aux(q1_ref[hh], kmean_ref.at[hh], qc_ref[hh], slope2, tq))
    r = lax.broadcasted_iota(jnp.int32, (tq, tq), 0)
    c = lax.broadcasted_iota(jnp.int32, (tq, tq), 1)
    for qi in reversed(range(seq // tq)):
        rows = slice(qi * tq, (qi + 1) * tq)
        n = (qi + 1) * tq
        for hh in range(heads):
            q1 = q1_ref[hh, rows, :]
            q2 = q2_tiles[hh][qi] if moba else q2_ref[hh, rows, :]
            s = lax.dot_general(jnp.concatenate([q1, q2], axis=1), kk_ref[hh, :n, :],
                                (((1,), (1,)), ((), ())), preferred_element_type=F32)
            diag = jnp.where(c <= r, s[:, n - tq:], NEG_INF)
            s = diag if qi == 0 else jnp.concatenate([s[:, :n - tq], diag], axis=1)
            m = jnp.max(s, axis=-1, keepdims=True)
            p = jnp.exp2(s - m).astype(vv_ref.dtype)
            acc = jnp.dot(p, vv_ref[hh, :n, :], preferred_element_type=F32)
            o_ref[hh, rows, :] = (acc[:, :HEAD_DIM] / acc[:, HEAD_DIM:]).astype(o_ref.dtype)


def _attention(q1, q2, k1, k2, v, batch, seq, n_heads, q1_group=0, k1_group=0, v_group=0,
               k2_shared=False, moba_slopes=None):
    assert seq % ATTN_BLOCK == 0
    t = batch * seq
    moba = moba_slopes is not None
    hp = math.gcd(n_heads, ATTN_HEADS_PER_STEP)
    assert all(g % hp == 0 for g in (q1_group, k1_group, v_group))
    head = lambda g: pl.BlockSpec((hp, seq, HEAD_DIM), lambda b, h: (g // hp + h, b, 0))
    if moba:
        assert seq // ATTN_BLOCK <= MOBA_MAX_BLOCKS and ATTN_BLOCK == MOBA_BLOCK
        in_specs = [pl.BlockSpec(memory_space=pltpu.SMEM), head(q1_group), head(k1_group), head(v_group),
                    pl.BlockSpec((hp, seq, LANE), lambda b, h: (h, 0, 0)),
                    pl.BlockSpec((hp, ATTN_BLOCK, LANE), lambda b, h: (h, 0, 0))]
        args = (moba_slopes, q1, k1, v, k2, q2)
        scratch = [pltpu.VMEM((hp, 2 * MOBA_MAX_BLOCKS, HEAD_DIM), F32)]
    else:
        k2_spec = pl.BlockSpec((seq, LANE), lambda b, h: (b, 0)) if k2_shared else head(0)
        in_specs = [head(q1_group), head(0), head(k1_group), k2_spec, head(v_group)]
        args = (q1, q2, k1, k2, v)
        scratch = []
    return pl.pallas_call(
        functools.partial(_attention_kernel, moba=moba, k2_shared=k2_shared),
        out_shape=jax.ShapeDtypeStruct((n_heads, t, HEAD_DIM), MXU_DTYPE),
        grid=(batch, n_heads // hp),
        in_specs=in_specs,
        out_specs=head(0),
        scratch_shapes=[pltpu.VMEM((hp, seq, 2 * HEAD_DIM), MXU_DTYPE),
                        pltpu.VMEM((hp, seq, 2 * HEAD_DIM), MXU_DTYPE)] + scratch,
        compiler_params=_params(("parallel", "parallel")),
        name="moba_attention" if moba else "causal_attention",
    )(*args)


def _moba_constants(seq, n_heads):
    slopes = jnp.exp2(-8.0 * jnp.arange(1, n_heads + 1, dtype=F32) / n_heads)
    pieces = jnp.stack(_split3(slopes * LOG2E), axis=-1)
    pos = jnp.arange(seq)
    onehot = (pos[:, None] // MOBA_BLOCK == jnp.arange(MOBA_MAX_BLOCKS)[None, :]).astype(F32)
    within = jnp.broadcast_to((pos % MOBA_BLOCK).astype(F32)[None, :, None], (n_heads, seq, 3))
    k_aux = jnp.concatenate([jnp.broadcast_to(onehot, (n_heads, seq, MOBA_MAX_BLOCKS))] * 2
                            + [jnp.broadcast_to(-pieces[:, None, :], (n_heads, seq, 3)), within], axis=-1)
    q_aux = jnp.concatenate([jnp.zeros((n_heads, MOBA_BLOCK, 2 * MOBA_MAX_BLOCKS), F32),
                             within[:, :MOBA_BLOCK], jnp.broadcast_to(pieces[:, None, :], (n_heads, MOBA_BLOCK, 3))],
                            axis=-1)
    return slopes, _pad_last(k_aux, LANE).astype(MXU_DTYPE), _pad_last(q_aux, LANE)


def _sgu_kernel(u_ref, v_ref, lg_ref, lb_ref, w_ref, b_ref, o_ref):
    tm = u_ref.shape[0]
    nc = tm // SGU_CHUNK
    v = jax.nn.gelu(v_ref[...].astype(F32))
    vc = v - jnp.mean(v, axis=-1, keepdims=True)
    var = jnp.mean(vc * vc, axis=-1, keepdims=True)
    vn = (vc * lax.rsqrt(var + NORM_EPS) * lg_ref[...] + lb_ref[...]).astype(MXU_DTYPE)
    v_side = jnp.concatenate([vn[c * SGU_CHUNK:(c + 1) * SGU_CHUNK, :] for c in range(nc)], axis=1)
    r = lax.broadcasted_iota(jnp.int32, (SGU_CHUNK, SGU_CHUNK), 0)
    c_ix = lax.broadcasted_iota(jnp.int32, (SGU_CHUNK, SGU_CHUNK), 1)
    w = jnp.where(c_ix <= r, w_ref[...], 0.0).astype(MXU_DTYPE)
    mixed = jnp.dot(w, v_side, preferred_element_type=F32) + b_ref[...]
    u = jax.nn.gelu(u_ref[...].astype(F32))
    for c in range(nc):
        rows = slice(c * SGU_CHUNK, (c + 1) * SGU_CHUNK)
        o_ref[rows, :] = (u[rows, :] * mixed[:, c * HEAD_DIM:(c + 1) * HEAD_DIM]).astype(o_ref.dtype)


def _sgu(proj, ln_g, ln_b, w_s, b_s, n_heads, u_group, v_group):
    t = proj.shape[1]
    tm = min(2048, t)
    vec = pl.BlockSpec((None, 1, HEAD_DIM), lambda g, i: (g, 0, 0))
    return pl.pallas_call(
        _sgu_kernel,
        out_shape=jax.ShapeDtypeStruct((n_heads, t, HEAD_DIM), MXU_DTYPE),
        grid=(n_heads, t // tm),
        in_specs=[pl.BlockSpec((None, tm, HEAD_DIM), lambda g, i: (u_group + g, i, 0)),
                  pl.BlockSpec((None, tm, HEAD_DIM), lambda g, i: (v_group + g, i, 0)),
                  vec, vec,
                  pl.BlockSpec((None, SGU_CHUNK, SGU_CHUNK), lambda g, i: (g, 0, 0)),
                  pl.BlockSpec((None, SGU_CHUNK, 1), lambda g, i: (g, 0, 0))],
        out_specs=pl.BlockSpec((None, tm, HEAD_DIM), lambda g, i: (g, i, 0)),
        compiler_params=_params(("parallel", "parallel")),
        name="spatial_gating",
    )(proj, proj, ln_g[:, None, :], ln_b[:, None, :], w_s, b_s[:, :, None])


def _out_proj_kernel(*refs, n_mix):
    y_refs, (gn_ref, w_ref, resid_ref, o_ref, ob_ref, op_ref, a_ref) = refs[:n_mix], refs[n_mix:]
    per = y_refs[0].shape[0]
    width = per * LANE
    tm = a_ref.shape[0]

    @pl.when(pl.program_id(1) == 0)
    def _():
        for m, y_ref in enumerate(y_refs):
            for r0 in range(0, tm, NORM_ROW_CHUNK):
                rows = slice(r0, min(r0 + NORM_ROW_CHUNK, tm))
                parts = [y_ref[i, rows, :].astype(F32) for i in range(per)]
                ssq = jnp.sum(sum(p * p for p in parts), axis=-1, keepdims=True)
                rstd = lax.rsqrt(ssq / width + NORM_EPS)
                for i, p in enumerate(parts):
                    lo = m * width + i * LANE
                    a_ref[rows, lo:lo + LANE] = (p * rstd * gn_ref[m:m + 1, i * LANE:(i + 1) * LANE]).astype(a_ref.dtype)

    x_new = resid_ref[...] + jnp.dot(a_ref[...], w_ref[...], preferred_element_type=F32)
    o_ref[...] = x_new
    ob_ref[...] = x_new.astype(ob_ref.dtype)
    sq = x_new * x_new
    op_ref[...] = sum(sq[:, c0:c0 + LANE] for c0 in range(0, sq.shape[1], LANE))


def _out_proj(ys, group_gain, w_stack, layer, resid):
    t, n = resid.shape
    n_mix, width = group_gain.shape
    per = width // LANE
    tm, bn = min(1024, t), min(512, n)
    y_spec = pl.BlockSpec((per, tm, LANE), lambda i, j: (0, i, 0))
    tile = pl.BlockSpec((tm, bn), lambda i, j: (i, j))
    return pl.pallas_call(
        functools.partial(_out_proj_kernel, n_mix=n_mix),
        out_shape=(jax.ShapeDtypeStruct((t, n), F32), jax.ShapeDtypeStruct((t, n), MXU_DTYPE),
                   jax.ShapeDtypeStruct((t, n // bn * LANE), F32)),
        grid=(t // tm, n // bn),
        in_specs=[y_spec] * n_mix + [pl.BlockSpec((n_mix, width), lambda i, j: (0, 0)),
                                     pl.BlockSpec((None, n_mix * width, bn), lambda i, j: (layer, 0, j)),
                                     tile],
        out_specs=(tile, tile, pl.BlockSpec((tm, LANE), lambda i, j: (i, j))),
        scratch_shapes=[pltpu.VMEM((tm, n_mix * width), MXU_DTYPE)],
        compiler_params=_params(("parallel", "arbitrary")),
        name="out_proj",
    )(*ys, group_gain, w_stack, resid)


def _down_proj_norm_kernel(*refs, keep_x, k_total):
    if not keep_x:
        refs = refs + (refs[-1],)
    a_ref, w_ref, resid_ref, g_ref, ox_ref, oh_ref = refs
    k = pl.program_id(1)
    last = pl.num_programs(1) - 1
    tm, n = ox_ref.shape
    bk = a_ref.shape[1]
    k_rem = k_total % bk

    @pl.when(k == 0)
    def _():
        ox_ref[...] = resid_ref[...]

    def accumulate(depth):
        for c0 in range(0, n, PROJ_COL_CHUNK):
            cols = slice(c0, min(c0 + PROJ_COL_CHUNK, n))
            ox_ref[:, cols] += jnp.dot(a_ref[:, :depth], w_ref[:depth, cols], preferred_element_type=F32)

    if k_rem:
        pl.when(k < last)(lambda: accumulate(bk))
        pl.when(k == last)(lambda: accumulate(k_rem))
    else:
        accumulate(bk)

    @pl.when(k == last)
    def _():
        for r0 in range(0, tm, NORM_ROW_CHUNK):
            rows = slice(r0, min(r0 + NORM_ROW_CHUNK, tm))
            oh_ref[rows, :] = _rmsnorm_rows(ox_ref[rows, :], g_ref[...]).astype(oh_ref.dtype)


def _down_proj_norm(a, w_stack, layer, resid, g, h_dtype, keep_x=True):
    t, n = resid.shape
    k_total = a.shape[1]
    tm = min(512, t)
    bk = min(1024, k_total)
    row_block = pl.BlockSpec((tm, n), lambda i, k: (i, 0))
    h_shape = jax.ShapeDtypeStruct((t, n), h_dtype)
    assert keep_x or h_dtype == F32
    out_shape = (jax.ShapeDtypeStruct((t, n), F32), h_shape) if keep_x else h_shape
    return pl.pallas_call(
        functools.partial(_down_proj_norm_kernel, keep_x=keep_x, k_total=k_total),
        out_shape=out_shape,
        grid=(t // tm, pl.cdiv(k_total, bk)),
        in_specs=[pl.BlockSpec((tm, bk), lambda i, k: (i, k)),
                  pl.BlockSpec((None, bk, n), lambda i, k: (layer, k, 0)),
                  pl.BlockSpec((tm, n), lambda i, k: (i, 0), pipeline_mode=pl.Buffered(1)),
                  pl.BlockSpec((1, n), lambda i, k: (0, 0))],
        out_specs=(row_block, row_block) if keep_x else row_block,
        compiler_params=_params(("parallel", "arbitrary")),
        name="down_proj_norm",
    )(a, w_stack, resid, g.reshape(1, n))


def _ffn_up_kernel(h_ref, halo_ref, p_ref, phalo_ref, wg_ref, wv_ref, cw_ref, cb_ref, o_ref, hext_ref,
                   rstd_ref, *, blocks_per_seq):
    i, j = pl.program_id(0), pl.program_id(1)
    tm, d = h_ref.shape

    def inv_rms(p):
        r = lax.rsqrt(jnp.sum(p, axis=-1, keepdims=True) / d + NORM_EPS)
        return jnp.broadcast_to(r, (p.shape[0], LANE))

    @pl.when(j == 0)
    def _():
        halo = halo_ref[...]
        first = i % blocks_per_seq == 0
        hext_ref[:CONV_HALO, :] = jnp.where(first, jnp.zeros_like(halo), halo)
        hext_ref[CONV_HALO:, :] = h_ref[...]
        rstd_ref[:CONV_HALO, :] = inv_rms(phalo_ref[...])
        rstd_ref[CONV_HALO:, :] = inv_rms(p_ref[...])

    def scale_rows(m, r):
        return jnp.concatenate([m[:, c0:c0 + LANE] * r for c0 in range(0, m.shape[1], LANE)], axis=1)

    g = jnp.dot(hext_ref[...], wg_ref[...], preferred_element_type=F32)
    g = scale_rows(g, rstd_ref[...])
    val = scale_rows(jnp.dot(h_ref[...], wv_ref[...], preferred_element_type=F32), rstd_ref[CONV_HALO:, :])
    cw = cw_ref[...]
    gate = cb_ref[...]
    for tap in range(CONV_WIDTH):
        lo = CONV_HALO - (CONV_WIDTH - 1) + tap
        gate = gate + cw[tap:tap + 1, :] * g[lo:lo + tm, :]
    o_ref[...] = (gate * jax.nn.sigmoid(gate) * val).astype(o_ref.dtype)


def _ffn_up(h, ssq, wg_stack, wv_stack, conv_w, conv_b, layer, seq):
    t, d = h.shape
    n_p = ssq.shape[1]
    n = wg_stack.shape[-1]
    tm = min(1024, seq)
    bn = min(512, n)
    per_seq = seq // tm
    halo_blocks = tm // CONV_HALO
    return pl.pallas_call(
        functools.partial(_ffn_up_kernel, blocks_per_seq=per_seq),
        out_shape=jax.ShapeDtypeStruct((t, n), MXU_DTYPE),
        grid=(t // tm, pl.cdiv(n, bn)),
        in_specs=[pl.BlockSpec((tm, d), lambda i, j: (i, 0)),
                  pl.BlockSpec((CONV_HALO, d), lambda i, j: (jnp.maximum(i * halo_blocks - 1, 0), 0)),
                  pl.BlockSpec((tm, n_p), lambda i, j: (i, 0)),
                  pl.BlockSpec((CONV_HALO, n_p), lambda i, j: (jnp.maximum(i * halo_blocks - 1, 0), 0)),
                  pl.BlockSpec((None, d, bn), lambda i, j: (layer, 0, j)),
                  pl.BlockSpec((None, d, bn), lambda i, j: (layer, 0, j)),
                  pl.BlockSpec((None, CONV_WIDTH, bn), lambda i, j: (layer, 0, j)),
                  pl.BlockSpec((None, 1, bn), lambda i, j: (layer, 0, j))],
        out_specs=pl.BlockSpec((tm, bn), lambda i, j: (i, j)),
        scratch_shapes=[pltpu.VMEM((tm + CONV_HALO, d), MXU_DTYPE), pltpu.VMEM((tm + CONV_HALO, LANE), F32)],
        compiler_params=_params(("parallel", "arbitrary")),
        name="ffn_up_conv",
    )(h, h, ssq, ssq, wg_stack, wv_stack, conv_w, conv_b)


def _pad_last(w, width):
    return jnp.pad(w, [(0, 0)] * (w.ndim - 1) + [(0, width - w.shape[-1])])


def _rotate_half_columns(w):
    half = ROPE_DIM // 2
    return jnp.concatenate([-w[..., half:], w[..., :half]], axis=-1)


def _pack_weights(w_in, mla_w_uq, mla_w_ukv, w_o, norm_ffn_g, w_gate, w_val, conv_b, w_down,
                  group_width, q_rank, kv_rank, n_heads):
    gw = group_width
    a_end = 3 * gw
    b_end = a_end + 2 * gw
    cq_end = b_end + q_rank
    ckv_end = cq_end + kv_rank
    kr_end = ckv_end + ROPE_DIM
    d_end = kr_end + 3 * gw
    layers, qr, _ = mla_w_uq.shape
    head_segments = ((0, b_end), (kr_end, 3 * gw))
    half = ROPE_DIM // 2
    w_in_t, w_latent = [], []
    for l in range(layers):
        wt = jnp.swapaxes(w_in[l], 0, 1).astype(MXU_DTYPE)
        kr = wt[ckv_end:kr_end]
        forget = wt[d_end:]
        w_in_t.append(wt)
        w_latent.append(jnp.concatenate(
            [wt[b_end:ckv_end], kr, -kr[half:], kr[:half],
             jnp.pad(forget, ((0, LANE - forget.shape[0]), (0, 0)))], axis=0))

    uq = mla_w_uq.reshape(layers, qr, n_heads, HEAD_DIM + ROPE_DIM)
    nope, rope = uq[..., :HEAD_DIM], uq[..., HEAD_DIM:]
    flat = lambda w: w.reshape(layers, w.shape[1], n_heads * HEAD_DIM)
    wq = jnp.concatenate([flat(nope), flat(jnp.concatenate([rope, _rotate_half_columns(rope)], axis=-1))],
                         axis=-1).astype(MXU_DTYPE)
    ukv = mla_w_ukv.reshape(layers, kv_rank, n_heads, 2 * HEAD_DIM)
    wkv = jnp.concatenate([flat(ukv[..., :HEAD_DIM]), flat(ukv[..., HEAD_DIM:])], axis=-1).astype(MXU_DTYPE)

    ffn_gain = norm_ffn_g[:, :, None]
    return (w_in_t, head_segments, w_latent, wq, wkv, w_o.astype(MXU_DTYPE),
            (w_gate * ffn_gain).astype(MXU_DTYPE), (w_val * ffn_gain).astype(MXU_DTYPE),
            w_down.astype(MXU_DTYPE), conv_b[:, None, :])


def _rope_tables(seq):
    half = ROPE_DIM // 2
    inv_freq = ROPE_THETA ** (-jnp.arange(half, dtype=F32) / half)
    ang = jnp.arange(seq, dtype=F32)[:, None] * inv_freq[None, :]
    cos = jnp.concatenate([jnp.cos(ang), jnp.cos(ang)], axis=-1)
    sin = jnp.concatenate([jnp.sin(ang), jnp.sin(ang)], axis=-1)
    return _pad_last(cos, LANE), _pad_last(sin, LANE)


def kernel(x, norm_mix_g, w_in, sgu_ln_g, sgu_ln_b, sgu_w, sgu_b, mla_q_norm_g, mla_kv_norm_g,
           mla_w_uq, mla_w_ukv, fox_b_f, group_norm_g, w_o, norm_ffn_g, w_gate, w_val, conv_w,
           conv_b, w_down, final_norm_g):
    batch, seq, d_model = x.shape
    depth = w_in.shape[0]
    n_heads = fox_b_f.shape[-1]
    gw = group_norm_g.shape[-1]
    q_rank, kv_rank = mla_q_norm_g.shape[-1], mla_kv_norm_g.shape[-1]
    assert gw == n_heads * HEAD_DIM and seq % MOBA_BLOCK == 0
    assert q_rank % LANE == 0 and kv_rank % LANE == 0

    (w_in_t, head_segments, w_latent, wq, wkv, wo, wg, wv, wd, cb) = _pack_weights(
        w_in, mla_w_uq, mla_w_ukv, w_o, norm_ffn_g, w_gate, w_val, conv_b, w_down, gw, q_rank, kv_rank, n_heads)
    cos, sin = _rope_tables(seq)
    slopes, moba_k_aux, moba_q_aux = _moba_constants(seq, n_heads)
    f_col_block = (q_rank + kv_rank + LANE) // LANE
    mla_scale = (HEAD_DIM + ROPE_DIM) ** -0.5 * LOG2E
    a_q, a_k, a_v, b_u, b_v, d_q, d_k, d_v = (i * n_heads for i in range(8))
    q_scale, q_starts = HEAD_DIM ** -0.5 * LOG2E, (a_q * HEAD_DIM, d_q * HEAD_DIM)

    xf = x.reshape(batch * seq, d_model)
    h = _rmsnorm(xf, norm_mix_g[0], MXU_DTYPE)
    for l in range(depth):
        proj = _matmul_groups(h, w_in_t[l], 1024, min(1024, gw), head_segments, q_starts, q_scale)
        latent = _matmul_f32(h, w_latent[l], 512)
        qn, qr, kn, vv, kr = _mla_up(latent, mla_q_norm_g[l], mla_kv_norm_g[l], wq, wkv, cos, sin,
                                     l, n_heads, seq, mla_scale)
        fox_q_aux, fox_k_aux = _fox_gate(latent, fox_b_f[l], f_col_block, batch, seq, n_heads)
        y_a = _attention(proj, moba_q_aux, proj, moba_k_aux, proj, batch, seq, n_heads,
                         a_q, a_k, a_v, moba_slopes=slopes)
        y_b = _sgu(proj, sgu_ln_g[l], sgu_ln_b[l], sgu_w[l], sgu_b[l], n_heads, b_u, b_v)
        y_c = _attention(qn, qr, kn, kr, vv, batch, seq, n_heads, k2_shared=True)
        y_d = _attention(proj, fox_q_aux, proj, fox_k_aux, proj, batch, seq, n_heads, d_q, d_k, d_v)
        xf, xb, ssq = _out_proj((y_a, y_b, y_c, y_d), group_norm_g[l], wo, l, xf)
        act = _ffn_up(xb, ssq, wg, wv, conv_w, cb, l, seq)
        if l == depth - 1:
            h = _down_proj_norm(act, wd, l, xf, final_norm_g, F32, keep_x=False)
        else:
            xf, h = _down_proj_norm(act, wd, l, xf, norm_mix_g[l + 1], MXU_DTYPE)
    return h.reshape(batch, seq, d_model)
```

```python
import functools
import math

import jax
import jax.numpy as jnp
from jax import lax
from jax.experimental import pallas as pl
from jax.experimental.pallas import tpu as pltpu

F32 = jnp.float32
MXU_DTYPE = jnp.bfloat16

LANE = 128
HEAD_DIM = 128
MOBA_BLOCK = 256
MOBA_TOPK = 3
SGU_CHUNK = 128
ROPE_DIM = 64
ROPE_THETA = 10000.0
CONV_WIDTH = 3
NORM_EPS = 1e-6
NEG_INF = -1e30
LOG2E = math.log2(math.e)

ATTN_BLOCK = 256
MOBA_MAX_BLOCKS = 8
ATTN_HEADS_PER_STEP = 4
CONV_HALO = 16
PROJ_COL_CHUNK = 512
NORM_ROW_CHUNK = 128
VMEM_LIMIT_BYTES = 56 * 1024 * 1024
DOWN_PROJ_VMEM_LIMIT_BYTES = 62 * 1024 * 1024


def _params(semantics, vmem_limit_bytes=VMEM_LIMIT_BYTES):
    return pltpu.CompilerParams(dimension_semantics=semantics, vmem_limit_bytes=vmem_limit_bytes)


def _rmsnorm_rows(x, g):
    ms = jnp.mean(x * x, axis=-1, keepdims=True)
    return x * lax.rsqrt(ms + NORM_EPS) * g


def _split3(x):
    hi = x.astype(MXU_DTYPE).astype(F32)
    r = x - hi
    mid = r.astype(MXU_DTYPE).astype(F32)
    lo = (r - mid).astype(MXU_DTYPE).astype(F32)
    return hi, mid, lo


def _norm_kernel(x_ref, g_ref, o_ref):
    o_ref[...] = _rmsnorm_rows(x_ref[...], g_ref[...]).astype(o_ref.dtype)


def _rmsnorm(x, g, out_dtype):
    t, d = x.shape
    tm = min(512, t)
    return pl.pallas_call(
        _norm_kernel,
        out_shape=jax.ShapeDtypeStruct((t, d), out_dtype),
        grid=(t // tm,),
        in_specs=[pl.BlockSpec((tm, d), lambda i: (i, 0)),
                  pl.BlockSpec((1, d), lambda i: (0, 0))],
        out_specs=pl.BlockSpec((tm, d), lambda i: (i, 0)),
        compiler_params=_params(("parallel",)),
        name="rmsnorm",
    )(x, g.reshape(1, d))


def _dot_nt(a, w_t):
    return lax.dot_general(a, w_t, (((1,), (1,)), ((), ())), preferred_element_type=F32)


def _mm_groups_kernel(a_ref, w_ref, o_ref, *, scaled_blocks, scale):
    acc = _dot_nt(a_ref[...], w_ref[...])
    j = pl.program_id(1)
    is_scaled = functools.reduce(jnp.logical_or, [j == b for b in scaled_blocks])
    acc = acc * jnp.where(is_scaled, scale, 1.0)
    for g in range(o_ref.shape[0]):
        o_ref[g] = acc[:, g * LANE:(g + 1) * LANE].astype(o_ref.dtype)


def _matmul_groups(a, w_t, tm, bn, segments, scaled_starts, scale):
    t, k = a.shape
    tm = min(tm, t)
    assert all(rows % bn == 0 for _, rows in segments) and all(s0 % bn == 0 for s0 in scaled_starts)
    n = sum(rows for _, rows in segments)

    def row_start(j):
        r = j * bn
        shift, done = 0, 0
        for r0, rows in segments:
            r = r + jnp.where(j >= done // bn, (r0 - done) - shift, 0)
            shift, done = r0 - done, done + rows
        return pl.multiple_of(r, math.gcd(bn, *(r0 for r0, _ in segments)))

    return pl.pallas_call(
        functools.partial(_mm_groups_kernel, scaled_blocks=[s0 // bn for s0 in scaled_starts], scale=scale),
        out_shape=jax.ShapeDtypeStruct((n // LANE, t, LANE), MXU_DTYPE),
        grid=(t // tm, n // bn),
        in_specs=[pl.BlockSpec((tm, k), lambda i, j: (i, 0)),
                  pl.BlockSpec((pl.Element(bn), pl.Element(k)), lambda i, j: (row_start(j), 0))],
        out_specs=pl.BlockSpec((bn // LANE, tm, LANE), lambda i, j: (j, i, 0)),
        compiler_params=_params(("parallel", "arbitrary")),
        name="in_proj_heads",
    )(a, w_t)


def _mm_plain_kernel(a_ref, w_ref, o_ref):
    o_ref[...] = _dot_nt(a_ref[...], w_ref[...]).astype(o_ref.dtype)


def _matmul_f32(a, w_t, tm):
    t, k = a.shape
    n = w_t.shape[0]
    tm = min(tm, t)
    return pl.pallas_call(
        _mm_plain_kernel,
        out_shape=jax.ShapeDtypeStruct((t, n), F32),
        grid=(t // tm,),
        in_specs=[pl.BlockSpec((tm, k), lambda i: (i, 0)),
                  pl.BlockSpec((n, k), lambda i: (0, 0))],
        out_specs=pl.BlockSpec((tm, n), lambda i: (i, 0)),
        compiler_params=_params(("parallel",)),
        name="in_proj_latent",
    )(a, w_t)


def _mla_up_kernel(c_ref, gq_ref, gkv_ref, wq_ref, wkv_ref, cos_ref, sin_ref,
                   qn_ref, qr_ref, kn_ref, v_ref, kr_ref, *, q_rank, kv_rank, scale):
    n_heads = qn_ref.shape[0]
    hw = n_heads * HEAD_DIM
    c = c_ref[...]
    cq = _rmsnorm_rows(c[:, :q_rank], gq_ref[...]).astype(MXU_DTYPE)
    ckv = _rmsnorm_rows(c[:, q_rank:q_rank + kv_rank], gkv_ref[...]).astype(MXU_DTYPE)
    q = jnp.dot(cq, wq_ref[...], preferred_element_type=F32)
    kv = jnp.dot(ckv, wkv_ref[...], preferred_element_type=F32)
    cos, sin = cos_ref[...], sin_ref[...]

    def rope(pair):
        return pair * cos + pltpu.roll(pair, ROPE_DIM, axis=1) * sin

    for h in range(n_heads):
        lo, hi = h * HEAD_DIM, (h + 1) * HEAD_DIM
        qn_ref[h] = (q[:, lo:hi] * scale).astype(qn_ref.dtype)
        qr_ref[h] = (rope(q[:, hw + lo:hw + hi]) * scale).astype(qr_ref.dtype)
        kn_ref[h] = kv[:, lo:hi].astype(kn_ref.dtype)
        v_ref[h] = kv[:, hw + lo:hw + hi].astype(v_ref.dtype)
    o = q_rank + kv_rank
    kr_ref[...] = rope(c[:, o:o + LANE]).astype(kr_ref.dtype)


def _mla_up(c, gq, gkv, wq_stack, wkv_stack, cos, sin, layer, n_heads, seq, scale):
    t, cw = c.shape
    q_rank, kv_rank = gq.shape[-1], gkv.shape[-1]
    tm = min(512, seq)
    per_seq = seq // tm
    heads = jax.ShapeDtypeStruct((n_heads, t, LANE), MXU_DTYPE)
    head_spec = pl.BlockSpec((n_heads, tm, LANE), lambda i: (0, i, 0))
    return pl.pallas_call(
        functools.partial(_mla_up_kernel, q_rank=q_rank, kv_rank=kv_rank, scale=scale),
        out_shape=(heads, heads, heads, heads, jax.ShapeDtypeStruct((t, LANE), MXU_DTYPE)),
        grid=(t // tm,),
        in_specs=[pl.BlockSpec((tm, cw), lambda i: (i, 0)),
                  pl.BlockSpec((1, q_rank), lambda i: (0, 0)),
                  pl.BlockSpec((1, kv_rank), lambda i: (0, 0)),
                  pl.BlockSpec((None,) + wq_stack.shape[1:], lambda i: (layer, 0, 0)),
                  pl.BlockSpec((None,) + wkv_stack.shape[1:], lambda i: (layer, 0, 0)),
                  pl.BlockSpec((tm, LANE), lambda i: (i % per_seq, 0)),
                  pl.BlockSpec((tm, LANE), lambda i: (i % per_seq, 0))],
        out_specs=(head_spec, head_spec, head_spec, head_spec,
                   pl.BlockSpec((tm, LANE), lambda i: (i, 0))),
        compiler_params=_params(("parallel",)),
        name="mla_up",
    )(c, gq.reshape(1, q_rank), gkv.reshape(1, kv_rank), wq_stack, wkv_stack, cos, sin)


def _fox_gate_kernel(f_ref, b_ref, qa_ref, ka_ref):
    n_heads = qa_ref.shape[0]
    z = f_ref[...] + b_ref[...]
    x = -(jnp.maximum(-z, 0.0) + jnp.log1p(jnp.exp(-jnp.abs(z))))
    s = z.shape[0]
    rows = lax.broadcasted_iota(jnp.int32, z.shape, 0)
    shift = 1
    while shift < s:
        x = x + jnp.where(rows >= shift, pltpu.roll(x, shift, axis=0), 0.0)
        shift *= 2
    x = x * LOG2E
    lane = lax.broadcasted_iota(jnp.int32, z.shape, 1)
    ones = jnp.where(lane < 3, 1.0, 0.0)
    for h in range(n_heads):
        hi, mid, lo = _split3(jnp.broadcast_to(x[:, h:h + 1], x.shape))
        pieces = jnp.where(lane == 0, hi, jnp.where(lane == 1, mid, jnp.where(lane == 2, lo, 0.0)))
        qa_ref[h] = (pieces + pltpu.roll(ones, 3, axis=1)).astype(qa_ref.dtype)
        ka_ref[h] = (ones - pltpu.roll(pieces, 3, axis=1)).astype(ka_ref.dtype)


def _fox_gate(c, b_f, f_col_block, batch, seq, n_heads):
    b_pad = jnp.zeros((1, LANE), F32).at[0, :n_heads].set(b_f)
    aux = jax.ShapeDtypeStruct((n_heads, batch * seq, LANE), MXU_DTYPE)
    aux_spec = pl.BlockSpec((n_heads, seq, LANE), lambda b: (0, b, 0))
    return pl.pallas_call(
        _fox_gate_kernel,
        out_shape=(aux, aux),
        grid=(batch,),
        in_specs=[pl.BlockSpec((seq, LANE), lambda b: (b, f_col_block)),
                  pl.BlockSpec((1, LANE), lambda b: (0, 0))],
        out_specs=(aux_spec, aux_spec),
        compiler_params=_params(("parallel",)),
        name="fox_gate",
    )(c, b_pad)


def _moba_query_aux(q, kmean_ref, qconst, slope2, blk):
    seq = q.shape[0]
    g = lax.dot_general(kmean_ref[...], q.astype(F32), (((1,), (1,)), ((), ())),
                        precision=lax.Precision.HIGHEST, preferred_element_type=F32)
    row = lax.broadcasted_iota(jnp.int32, g.shape, 0)
    bid = row & (MOBA_MAX_BLOCKS - 1)
    qblk = lax.broadcasted_iota(jnp.int32, g.shape, 1) // blk
    rank = jnp.zeros(g.shape, F32)
    for n in range(seq // blk - 1):
        gn = g[n:n + 1, :]
        beats = ((gn > g) | ((gn == g) & (n < bid))) & (n < qblk)
        rank = rank + jnp.where(beats, 1.0, 0.0)
    past = bid < qblk
    chosen = (rank < MOBA_TOPK) & past
    offset = (qblk - bid).astype(F32) * (-slope2 * blk)
    term = jnp.where(chosen, offset, jnp.where(past, NEG_INF, 0.0))
    hi = term.astype(MXU_DTYPE).astype(F32)
    lo = jnp.where(chosen, term - hi, 0.0)
    aux_t = jnp.where(row < MOBA_MAX_BLOCKS, hi, lo)
    aux = jnp.concatenate([aux_t, jnp.zeros((LANE - aux_t.shape[0], seq), F32)], axis=0).T
    return [(aux[r0:r0 + blk, :] + qconst).astype(MXU_DTYPE) for r0 in range(0, seq, blk)]


def _attention_kernel(*refs, moba, k2_shared):
    if moba:
        slopes_ref, q1_ref, k1_ref, v_ref, k2_ref, qc_ref, o_ref, kk_ref, vv_ref, kmean_ref = refs
    else:
        q1_ref, q2_ref, k1_ref, k2_ref, v_ref, o_ref, kk_ref, vv_ref = refs
    heads, seq, _ = k1_ref.shape
    tq = ATTN_BLOCK
    q2_tiles = []
    for hh in range(heads):
        kk_ref[hh, :, :HEAD_DIM] = k1_ref[hh]
        kk_ref[hh, :, HEAD_DIM:] = k2_ref[...] if k2_shared else k2_ref[hh]
        vv_ref[hh, :, :HEAD_DIM] = v_ref[hh]
        vv_ref[hh, :, HEAD_DIM:] = jnp.ones((seq, HEAD_DIM), vv_ref.dtype)
        if moba:
            slope2 = slopes_ref[pl.program_id(1) * heads + hh] * LOG2E
            kmean_ref[hh] = jnp.zeros(kmean_ref.shape[1:], F32)
            for n in range(seq // tq):
                mean = jnp.mean(k1_ref[hh, n * tq:(n + 1) * tq, :].astype(F32), axis=0, keepdims=True)
                kmean_ref[hh, n:n + 1, :] = mean
                kmean_ref[hh, MOBA_MAX_BLOCKS + n:MOBA_MAX_BLOCKS + n + 1, :] = mean
            q2_tiles.append(_moba_query_aux(q1_ref[hh], kmean_ref.at[hh], qc_ref[hh], slope2, tq))
    r = lax.broadcasted_iota(jnp.int32, (tq, tq), 0)
    c = lax.broadcasted_iota(jnp.int32, (tq, tq), 1)
    for qi in reversed(range(seq // tq)):
        rows = slice(qi * tq, (qi + 1) * tq)
        n = (qi + 1) * tq
        for hh in range(heads):
            q1 = q1_ref[hh, rows, :]
            q2 = q2_tiles[hh][qi] if moba else q2_ref[hh, rows, :]
            s = lax.dot_general(jnp.concatenate([q1, q2], axis=1), kk_ref[hh, :n, :],
                                (((1,), (1,)), ((), ())), preferred_element_type=F32)
            diag = jnp.where(c <= r, s[:, n - tq:], NEG_INF)
            s = diag if qi == 0 else jnp.concatenate([s[:, :n - tq], diag], axis=1)
            m = jnp.max(s, axis=-1, keepdims=True)
            p = jnp.exp2(s - m).astype(vv_ref.dtype)
            acc = jnp.dot(p, vv_ref[hh, :n, :], preferred_element_type=F32)
            o_ref[hh, rows, :] = (acc[:, :HEAD_DIM] / acc[:, HEAD_DIM:]).astype(o_ref.dtype)


def _attention(q1, q2, k1, k2, v, batch, seq, n_heads, q1_group=0, k1_group=0, v_group=0,
               k2_shared=False, moba_slopes=None):
    assert seq % ATTN_BLOCK == 0
    t = batch * seq
    moba = moba_slopes is not None
    hp = math.gcd(n_heads, ATTN_HEADS_PER_STEP)
    assert all(g % hp == 0 for g in (q1_group, k1_group, v_group))
    head = lambda g: pl.BlockSpec((hp, seq, HEAD_DIM), lambda b, h: (g // hp + h, b, 0))
    if moba:
        assert seq // ATTN_BLOCK <= MOBA_MAX_BLOCKS and ATTN_BLOCK == MOBA_BLOCK
        in_specs = [pl.BlockSpec(memory_space=pltpu.SMEM), head(q1_group), head(k1_group), head(v_group),
                    pl.BlockSpec((hp, seq, LANE), lambda b, h: (h, 0, 0)),
                    pl.BlockSpec((hp, ATTN_BLOCK, LANE), lambda b, h: (h, 0, 0))]
        args = (moba_slopes, q1, k1, v, k2, q2)
        scratch = [pltpu.VMEM((hp, 2 * MOBA_MAX_BLOCKS, HEAD_DIM), F32)]
    else:
        k2_spec = pl.BlockSpec((seq, LANE), lambda b, h: (b, 0)) if k2_shared else head(0)
        in_specs = [head(q1_group), head(0), head(k1_group), k2_spec, head(v_group)]
        args = (q1, q2, k1, k2, v)
        scratch = []
    return pl.pallas_call(
        functools.partial(_attention_kernel, moba=moba, k2_shared=k2_shared),
        out_shape=jax.ShapeDtypeStruct((n_heads, t, HEAD_DIM), MXU_DTYPE),
        grid=(batch, n_heads // hp),
        in_specs=in_specs,
        out_specs=head(0),
        scratch_shapes=[pltpu.VMEM((hp, seq, 2 * HEAD_DIM), MXU_DTYPE),
                        pltpu.VMEM((hp, seq, 2 * HEAD_DIM), MXU_DTYPE)] + scratch,
        compiler_params=_params(("parallel", "parallel")),
        name="moba_attention" if moba else "causal_attention",
    )(*args)


def _moba_constants(seq, n_heads):
    slopes = jnp.exp2(-8.0 * jnp.arange(1, n_heads + 1, dtype=F32) / n_heads)
    pieces = jnp.stack(_split3(slopes * LOG2E), axis=-1)
    pos = jnp.arange(seq)
    onehot = (pos[:, None] // MOBA_BLOCK == jnp.arange(MOBA_MAX_BLOCKS)[None, :]).astype(F32)
    within = jnp.broadcast_to((pos % MOBA_BLOCK).astype(F32)[None, :, None], (n_heads, seq, 3))
    k_aux = jnp.concatenate([jnp.broadcast_to(onehot, (n_heads, seq, MOBA_MAX_BLOCKS))] * 2
                            + [jnp.broadcast_to(-pieces[:, None, :], (n_heads, seq, 3)), within], axis=-1)
    q_aux = jnp.concatenate([jnp.zeros((n_heads, MOBA_BLOCK, 2 * MOBA_MAX_BLOCKS), F32),
                             within[:, :MOBA_BLOCK], jnp.broadcast_to(pieces[:, None, :], (n_heads, MOBA_BLOCK, 3))],
                            axis=-1)
    return slopes, _pad_last(k_aux, LANE).astype(MXU_DTYPE), _pad_last(q_aux, LANE)


def _sgu_kernel(u_ref, v_ref, lg_ref, lb_ref, w_ref, b_ref, o_ref):
    tm = u_ref.shape[0]
    nc = tm // SGU_CHUNK
    v = jax.nn.gelu(v_ref[...].astype(F32))
    vc = v - jnp.mean(v, axis=-1, keepdims=True)
    var = jnp.mean(vc * vc, axis=-1, keepdims=True)
    vn = (vc * lax.rsqrt(var + NORM_EPS) * lg_ref[...] + lb_ref[...]).astype(MXU_DTYPE)
    v_side = jnp.concatenate([vn[c * SGU_CHUNK:(c + 1) * SGU_CHUNK, :] for c in range(nc)], axis=1)
    r = lax.broadcasted_iota(jnp.int32, (SGU_CHUNK, SGU_CHUNK), 0)
    c_ix = lax.broadcasted_iota(jnp.int32, (SGU_CHUNK, SGU_CHUNK), 1)
    w = jnp.where(c_ix <= r, w_ref[...], 0.0).astype(MXU_DTYPE)
    mixed = jnp.dot(w, v_side, preferred_element_type=F32) + b_ref[...]
    u = jax.nn.gelu(u_ref[...].astype(F32))
    for c in range(nc):
        rows = slice(c * SGU_CHUNK, (c + 1) * SGU_CHUNK)
        o_ref[rows, :] = (u[rows, :] * mixed[:, c * HEAD_DIM:(c + 1) * HEAD_DIM]).astype(o_ref.dtype)


def _sgu(proj, ln_g, ln_b, w_s, b_s, n_heads, u_group, v_group):
    t = proj.shape[1]
    tm = min(2048, t)
    vec = pl.BlockSpec((None, 1, HEAD_DIM), lambda g, i: (g, 0, 0))
    return pl.pallas_call(
        _sgu_kernel,
        out_shape=jax.ShapeDtypeStruct((n_heads, t, HEAD_DIM), MXU_DTYPE),
        grid=(n_heads, t // tm),
        in_specs=[pl.BlockSpec((None, tm, HEAD_DIM), lambda g, i: (u_group + g, i, 0)),
                  pl.BlockSpec((None, tm, HEAD_DIM), lambda g, i: (v_group + g, i, 0)),
                  vec, vec,
                  pl.BlockSpec((None, SGU_CHUNK, SGU_CHUNK), lambda g, i: (g, 0, 0)),
                  pl.BlockSpec((None, SGU_CHUNK, 1), lambda g, i: (g, 0, 0))],
        out_specs=pl.BlockSpec((None, tm, HEAD_DIM), lambda g, i: (g, i, 0)),
        compiler_params=_params(("parallel", "parallel")),
        name="spatial_gating",
    )(proj, proj, ln_g[:, None, :], ln_b[:, None, :], w_s, b_s[:, :, None])


def _out_proj_kernel(*refs, n_mix):
    y_refs, (gn_ref, w_ref, resid_ref, o_ref, ob_ref, op_ref, a_ref) = refs[:n_mix], refs[n_mix:]
    per = y_refs[0].shape[0]
    width = per * LANE
    tm = a_ref.shape[0]

    @pl.when(pl.program_id(1) == 0)
    def _():
        for m, y_ref in enumerate(y_refs):
            for r0 in range(0, tm, NORM_ROW_CHUNK):
                rows = slice(r0, min(r0 + NORM_ROW_CHUNK, tm))
                parts = [y_ref[i, rows, :].astype(F32) for i in range(per)]
                ssq = jnp.sum(sum(p * p for p in parts), axis=-1, keepdims=True)
                rstd = lax.rsqrt(ssq / width + NORM_EPS)
                for i, p in enumerate(parts):
                    lo = m * width + i * LANE
                    a_ref[rows, lo:lo + LANE] = (p * rstd * gn_ref[m:m + 1, i * LANE:(i + 1) * LANE]).astype(a_ref.dtype)

    x_new = resid_ref[...] + jnp.dot(a_ref[...], w_ref[...], preferred_element_type=F32)
    o_ref[...] = x_new
    ob_ref[...] = x_new.astype(ob_ref.dtype)
    sq = x_new * x_new
    op_ref[...] = sum(sq[:, c0:c0 + LANE] for c0 in range(0, sq.shape[1], LANE))


def _out_proj(ys, group_gain, w_stack, layer, resid):
    t, n = resid.shape
    n_mix, width = group_gain.shape
    per = width // LANE
    tm, bn = min(1024, t), min(512, n)
    y_spec = pl.BlockSpec((per, tm, LANE), lambda i, j: (0, i, 0))
    tile = pl.BlockSpec((tm, bn), lambda i, j: (i, j))
    return pl.pallas_call(
        functools.partial(_out_proj_kernel, n_mix=n_mix),
        out_shape=(jax.ShapeDtypeStruct((t, n), F32), jax.ShapeDtypeStruct((t, n), MXU_DTYPE),
                   jax.ShapeDtypeStruct((t, n // bn * LANE), F32)),
        grid=(t // tm, n // bn),
        in_specs=[y_spec] * n_mix + [pl.BlockSpec((n_mix, width), lambda i, j: (0, 0)),
                                     pl.BlockSpec((None, n_mix * width, bn), lambda i, j: (layer, 0, j)),
                                     tile],
        out_specs=(tile, tile, pl.BlockSpec((tm, LANE), lambda i, j: (i, j))),
        scratch_shapes=[pltpu.VMEM((tm, n_mix * width), MXU_DTYPE)],
        compiler_params=_params(("parallel", "arbitrary")),
        name="out_proj",
    )(*ys, group_gain, w_stack, resid)


def _down_proj_norm_kernel(*refs, keep_x, k_total):
    if not keep_x:
        refs = refs + (refs[-1],)
    a_ref, w_ref, resid_ref, g_ref, ox_ref, oh_ref = refs
    k = pl.program_id(1)
    last = pl.num_programs(1) - 1
    tm, n = ox_ref.shape
    bk = a_ref.shape[1]
    k_rem = k_total % bk

    @pl.when(k == 0)
    def _():
        ox_ref[...] = resid_ref[...]

    def accumulate(depth):
        for c0 in range(0, n, PROJ_COL_CHUNK):
            cols = slice(c0, min(c0 + PROJ_COL_CHUNK, n))
            ox_ref[:, cols] += jnp.dot(a_ref[:, :depth], w_ref[:depth, cols], preferred_element_type=F32)

    if k_rem:
        pl.when(k < last)(lambda: accumulate(bk))
        pl.when(k == last)(lambda: accumulate(k_rem))
    else:
        accumulate(bk)

    @pl.when(k == last)
    def _():
        for r0 in range(0, tm, NORM_ROW_CHUNK):
            rows = slice(r0, min(r0 + NORM_ROW_CHUNK, tm))
            oh_ref[rows, :] = _rmsnorm_rows(ox_ref[rows, :], g_ref[...]).astype(oh_ref.dtype)


def _down_proj_norm(a, w_stack, layer, resid, g, h_dtype, keep_x=True):
    t, n = resid.shape
    k_total = a.shape[1]
    tm = min(512, t)
    bk = min(1024, k_total)
    row_block = pl.BlockSpec((tm, n), lambda i, k: (i, 0))
    h_shape = jax.ShapeDtypeStruct((t, n), h_dtype)
    assert keep_x or h_dtype == F32
    out_shape = (jax.ShapeDtypeStruct((t, n), F32), h_shape) if keep_x else h_shape
    return pl.pallas_call(
        functools.partial(_down_proj_norm_kernel, keep_x=keep_x, k_total=k_total),
        out_shape=out_shape,
        grid=(t // tm, pl.cdiv(k_total, bk)),
        in_specs=[pl.BlockSpec((tm, bk), lambda i, k: (i, k)),
                  pl.BlockSpec((None, bk, n), lambda i, k: (layer, k, 0)),
                  row_block,
                  pl.BlockSpec((1, n), lambda i, k: (0, 0))],
        out_specs=(row_block, row_block) if keep_x else row_block,
        compiler_params=_params(("parallel", "arbitrary"), DOWN_PROJ_VMEM_LIMIT_BYTES),
        name="down_proj_norm",
    )(a, w_stack, resid, g.reshape(1, n))


def _ffn_up_kernel(h_ref, halo_ref, p_ref, phalo_ref, wg_ref, wv_ref, cw_ref, cb_ref, o_ref, hext_ref,
                   rstd_ref, *, blocks_per_seq):
    i, j = pl.program_id(0), pl.program_id(1)
    tm, d = h_ref.shape

    def inv_rms(p):
        r = lax.rsqrt(jnp.sum(p, axis=-1, keepdims=True) / d + NORM_EPS)
        return jnp.broadcast_to(r, (p.shape[0], LANE))

    @pl.when(j == 0)
    def _():
        halo = halo_ref[...]
        first = i % blocks_per_seq == 0
        hext_ref[:CONV_HALO, :] = jnp.where(first, jnp.zeros_like(halo), halo)
        hext_ref[CONV_HALO:, :] = h_ref[...]
        rstd_ref[:CONV_HALO, :] = inv_rms(phalo_ref[...])
        rstd_ref[CONV_HALO:, :] = inv_rms(p_ref[...])

    def scale_rows(m, r):
        return jnp.concatenate([m[:, c0:c0 + LANE] * r for c0 in range(0, m.shape[1], LANE)], axis=1)

    g = jnp.dot(hext_ref[...], wg_ref[...], preferred_element_type=F32)
    g = scale_rows(g, rstd_ref[...])
    val = scale_rows(jnp.dot(h_ref[...], wv_ref[...], preferred_element_type=F32), rstd_ref[CONV_HALO:, :])
    cw = cw_ref[...]
    gate = cb_ref[...]
    for tap in range(CONV_WIDTH):
        lo = CONV_HALO - (CONV_WIDTH - 1) + tap
        gate = gate + cw[tap:tap + 1, :] * g[lo:lo + tm, :]
    o_ref[...] = (gate * jax.nn.sigmoid(gate) * val).astype(o_ref.dtype)


def _ffn_up(h, ssq, wg_stack, wv_stack, conv_w, conv_b, layer, seq):
    t, d = h.shape
    n_p = ssq.shape[1]
    n = wg_stack.shape[-1]
    tm = min(1024, seq)
    bn = min(512, n)
    per_seq = seq // tm
    halo_blocks = tm // CONV_HALO
    return pl.pallas_call(
        functools.partial(_ffn_up_kernel, blocks_per_seq=per_seq),
        out_shape=jax.ShapeDtypeStruct((t, n), MXU_DTYPE),
        grid=(t // tm, pl.cdiv(n, bn)),
        in_specs=[pl.BlockSpec((tm, d), lambda i, j: (i, 0)),
                  pl.BlockSpec((CONV_HALO, d), lambda i, j: (jnp.maximum(i * halo_blocks - 1, 0), 0)),
                  pl.BlockSpec((tm, n_p), lambda i, j: (i, 0)),
                  pl.BlockSpec((CONV_HALO, n_p), lambda i, j: (jnp.maximum(i * halo_blocks - 1, 0), 0)),
                  pl.BlockSpec((None, d, bn), lambda i, j: (layer, 0, j)),
                  pl.BlockSpec((None, d, bn), lambda i, j: (layer, 0, j)),
                  pl.BlockSpec((None, CONV_WIDTH, bn), lambda i, j: (layer, 0, j)),
                  pl.BlockSpec((None, 1, bn), lambda i, j: (layer, 0, j))],
        out_specs=pl.BlockSpec((tm, bn), lambda i, j: (i, j)),
        scratch_shapes=[pltpu.VMEM((tm + CONV_HALO, d), MXU_DTYPE), pltpu.VMEM((tm + CONV_HALO, LANE), F32)],
        compiler_params=_params(("parallel", "arbitrary")),
        name="ffn_up_conv",
    )(h, h, ssq, ssq, wg_stack, wv_stack, conv_w, conv_b)


def _pad_last(w, width):
    return jnp.pad(w, [(0, 0)] * (w.ndim - 1) + [(0, width - w.shape[-1])])


def _rotate_half_columns(w):
    half = ROPE_DIM // 2
    return jnp.concatenate([-w[..., half:], w[..., :half]], axis=-1)


def _pack_weights(w_in, mla_w_uq, mla_w_ukv, w_o, norm_ffn_g, w_gate, w_val, conv_b, w_down,
                  group_width, q_rank, kv_rank, n_heads):
    gw = group_width
    a_end = 3 * gw
    b_end = a_end + 2 * gw
    cq_end = b_end + q_rank
    ckv_end = cq_end + kv_rank
    kr_end = ckv_end + ROPE_DIM
    d_end = kr_end + 3 * gw
    layers, qr, _ = mla_w_uq.shape
    head_segments = ((0, b_end), (kr_end, 3 * gw))
    half = ROPE_DIM // 2
    w_in_t, w_latent = [], []
    for l in range(layers):
        wt = jnp.swapaxes(w_in[l], 0, 1).astype(MXU_DTYPE)
        kr = wt[ckv_end:kr_end]
        forget = wt[d_end:]
        w_in_t.append(wt)
        w_latent.append(jnp.concatenate(
            [wt[b_end:ckv_end], kr, -kr[half:], kr[:half],
             jnp.pad(forget, ((0, LANE - forget.shape[0]), (0, 0)))], axis=0))

    uq = mla_w_uq.reshape(layers, qr, n_heads, HEAD_DIM + ROPE_DIM)
    nope, rope = uq[..., :HEAD_DIM], uq[..., HEAD_DIM:]
    flat = lambda w: w.reshape(layers, w.shape[1], n_heads * HEAD_DIM)
    wq = jnp.concatenate([flat(nope), flat(jnp.concatenate([rope, _rotate_half_columns(rope)], axis=-1))],
                         axis=-1).astype(MXU_DTYPE)
    ukv = mla_w_ukv.reshape(layers, kv_rank, n_heads, 2 * HEAD_DIM)
    wkv = jnp.concatenate([flat(ukv[..., :HEAD_DIM]), flat(ukv[..., HEAD_DIM:])], axis=-1).astype(MXU_DTYPE)

    ffn_gain = norm_ffn_g[:, :, None]
    return (w_in_t, head_segments, w_latent, wq, wkv, w_o.astype(MXU_DTYPE),
            (w_gate * ffn_gain).astype(MXU_DTYPE), (w_val * ffn_gain).astype(MXU_DTYPE),
            w_down.astype(MXU_DTYPE), conv_b[:, None, :])


def _rope_tables(seq):
    half = ROPE_DIM // 2
    inv_freq = ROPE_THETA ** (-jnp.arange(half, dtype=F32) / half)
    ang = jnp.arange(seq, dtype=F32)[:, None] * inv_freq[None, :]
    cos = jnp.concatenate([jnp.cos(ang), jnp.cos(ang)], axis=-1)
    sin = jnp.concatenate([jnp.sin(ang), jnp.sin(ang)], axis=-1)
    return _pad_last(cos, LANE), _pad_last(sin, LANE)


def kernel(x, norm_mix_g, w_in, sgu_ln_g, sgu_ln_b, sgu_w, sgu_b, mla_q_norm_g, mla_kv_norm_g,
           mla_w_uq, mla_w_ukv, fox_b_f, group_norm_g, w_o, norm_ffn_g, w_gate, w_val, conv_w,
           conv_b, w_down, final_norm_g):
    batch, seq, d_model = x.shape
    depth = w_in.shape[0]
    n_heads = fox_b_f.shape[-1]
    gw = group_norm_g.shape[-1]
    q_rank, kv_rank = mla_q_norm_g.shape[-1], mla_kv_norm_g.shape[-1]
    assert gw == n_heads * HEAD_DIM and seq % MOBA_BLOCK == 0
    assert q_rank % LANE == 0 and kv_rank % LANE == 0

    (w_in_t, head_segments, w_latent, wq, wkv, wo, wg, wv, wd, cb) = _pack_weights(
        w_in, mla_w_uq, mla_w_ukv, w_o, norm_ffn_g, w_gate, w_val, conv_b, w_down, gw, q_rank, kv_rank, n_heads)
    cos, sin = _rope_tables(seq)
    slopes, moba_k_aux, moba_q_aux = _moba_constants(seq, n_heads)
    f_col_block = (q_rank + kv_rank + LANE) // LANE
    mla_scale = (HEAD_DIM + ROPE_DIM) ** -0.5 * LOG2E
    a_q, a_k, a_v, b_u, b_v, d_q, d_k, d_v = (i * n_heads for i in range(8))
    q_scale, q_starts = HEAD_DIM ** -0.5 * LOG2E, (a_q * HEAD_DIM, d_q * HEAD_DIM)

    xf = x.reshape(batch * seq, d_model)
    h = _rmsnorm(xf, norm_mix_g[0], MXU_DTYPE)
    for l in range(depth):
        proj = _matmul_groups(h, w_in_t[l], 1024, min(1024, gw), head_segments, q_starts, q_scale)
        latent = _matmul_f32(h, w_latent[l], 512)
        qn, qr, kn, vv, kr = _mla_up(latent, mla_q_norm_g[l], mla_kv_norm_g[l], wq, wkv, cos, sin,
                                     l, n_heads, seq, mla_scale)
        fox_q_aux, fox_k_aux = _fox_gate(latent, fox_b_f[l], f_col_block, batch, seq, n_heads)
        y_a = _attention(proj, moba_q_aux, proj, moba_k_aux, proj, batch, seq, n_heads,
                         a_q, a_k, a_v, moba_slopes=slopes)
        y_b = _sgu(proj, sgu_ln_g[l], sgu_ln_b[l], sgu_w[l], sgu_b[l], n_heads, b_u, b_v)
        y_c = _attention(qn, qr, kn, kr, vv, batch, seq, n_heads, k2_shared=True)
        y_d = _attention(proj, fox_q_aux, proj, fox_k_aux, proj, batch, seq, n_heads, d_q, d_k, d_v)
        xf, xb, ssq = _out_proj((y_a, y_b, y_c, y_d), group_norm_g[l], wo, l, xf)
        act = _ffn_up(xb, ssq, wg, wv, conv_w, cb, l, seq)
        if l == depth - 1:
            h = _down_proj_norm(act, wd, l, xf, final_norm_g, F32, keep_x=False)
        else:
            xf, h = _down_proj_norm(act, wd, l, xf, norm_mix_g[l + 1], MXU_DTYPE)
    return h.reshape(batch, seq, d_model)
```

```python
import functools
import math

import jax
import jax.numpy as jnp
from jax import lax
from jax.experimental import pallas as pl
from jax.experimental.pallas import tpu as pltpu

F32 = jnp.float32
MXU_DTYPE = jnp.bfloat16

LANE = 128
HEAD_DIM = 128
MOBA_BLOCK = 256
MOBA_TOPK = 3
SGU_CHUNK = 128
ROPE_DIM = 64
ROPE_THETA = 10000.0
CONV_WIDTH = 3
NORM_EPS = 1e-6
NEG_INF = -1e30
LOG2E = math.log2(math.e)

ATTN_BLOCK = 256
MOBA_MAX_BLOCKS = 8
ATTN_HEADS_PER_STEP = 4
CONV_HALO = 16
PROJ_COL_CHUNK = 512
NORM_ROW_CHUNK = 128
VMEM_LIMIT_BYTES = 56 * 1024 * 1024
DOWN_PROJ_VMEM_LIMIT_BYTES = 62 * 1024 * 1024


def _params(semantics, vmem_limit_bytes=VMEM_LIMIT_BYTES):
    return pltpu.CompilerParams(dimension_semantics=semantics, vmem_limit_bytes=vmem_limit_bytes)


def _rmsnorm_rows(x, g):
    ms = jnp.mean(x * x, axis=-1, keepdims=True)
    return x * lax.rsqrt(ms + NORM_EPS) * g


def _split3(x):
    hi = x.astype(MXU_DTYPE).astype(F32)
    r = x - hi
    mid = r.astype(MXU_DTYPE).astype(F32)
    lo = (r - mid).astype(MXU_DTYPE).astype(F32)
    return hi, mid, lo


def _norm_kernel(x_ref, g_ref, o_ref):
    o_ref[...] = _rmsnorm_rows(x_ref[...], g_ref[...]).astype(o_ref.dtype)


def _rmsnorm(x, g, out_dtype):
    t, d = x.shape
    tm = min(512, t)
    return pl.pallas_call(
        _norm_kernel,
        out_shape=jax.ShapeDtypeStruct((t, d), out_dtype),
        grid=(t // tm,),
        in_specs=[pl.BlockSpec((tm, d), lambda i: (i, 0)),
                  pl.BlockSpec((1, d), lambda i: (0, 0))],
        out_specs=pl.BlockSpec((tm, d), lambda i: (i, 0)),
        compiler_params=_params(("parallel",)),
        name="rmsnorm",
    )(x, g.reshape(1, d))


def _dot_nt(a, w_t):
    return lax.dot_general(a, w_t, (((1,), (1,)), ((), ())), preferred_element_type=F32)


def _mm_groups_kernel(a_ref, w_ref, o_ref, *, scaled_blocks, scale):
    acc = _dot_nt(a_ref[...], w_ref[...])
    j = pl.program_id(1)
    is_scaled = functools.reduce(jnp.logical_or, [j == b for b in scaled_blocks])
    acc = acc * jnp.where(is_scaled, scale, 1.0)
    for g in range(o_ref.shape[0]):
        o_ref[g] = acc[:, g * LANE:(g + 1) * LANE].astype(o_ref.dtype)


def _matmul_groups(a, w_t, tm, bn, segments, scaled_starts, scale):
    t, k = a.shape
    tm = min(tm, t)
    assert all(rows % bn == 0 for _, rows in segments) and all(s0 % bn == 0 for s0 in scaled_starts)
    n = sum(rows for _, rows in segments)

    def row_start(j):
        r = j * bn
        shift, done = 0, 0
        for r0, rows in segments:
            r = r + jnp.where(j >= done // bn, (r0 - done) - shift, 0)
            shift, done = r0 - done, done + rows
        return pl.multiple_of(r, math.gcd(bn, *(r0 for r0, _ in segments)))

    return pl.pallas_call(
        functools.partial(_mm_groups_kernel, scaled_blocks=[s0 // bn for s0 in scaled_starts], scale=scale),
        out_shape=jax.ShapeDtypeStruct((n // LANE, t, LANE), MXU_DTYPE),
        grid=(t // tm, n // bn),
        in_specs=[pl.BlockSpec((tm, k), lambda i, j: (i, 0)),
                  pl.BlockSpec((pl.Element(bn), pl.Element(k)), lambda i, j: (row_start(j), 0))],
        out_specs=pl.BlockSpec((bn // LANE, tm, LANE), lambda i, j: (j, i, 0)),
        compiler_params=_params(("parallel", "arbitrary")),
        name="in_proj_heads",
    )(a, w_t)


def _mm_plain_kernel(a_ref, w_ref, o_ref):
    o_ref[...] = _dot_nt(a_ref[...], w_ref[...]).astype(o_ref.dtype)


def _matmul_f32(a, w_t, tm):
    t, k = a.shape
    n = w_t.shape[0]
    tm = min(tm, t)
    return pl.pallas_call(
        _mm_plain_kernel,
        out_shape=jax.ShapeDtypeStruct((t, n), F32),
        grid=(t // tm,),
        in_specs=[pl.BlockSpec((tm, k), lambda i: (i, 0)),
                  pl.BlockSpec((n, k), lambda i: (0, 0))],
        out_specs=pl.BlockSpec((tm, n), lambda i: (i, 0)),
        compiler_params=_params(("parallel",)),
        name="in_proj_latent",
    )(a, w_t)


def _mla_up_kernel(c_ref, gq_ref, gkv_ref, wq_ref, wkv_ref, cos_ref, sin_ref,
                   qn_ref, qr_ref, kn_ref, v_ref, kr_ref, *, q_rank, kv_rank, scale):
    n_heads = qn_ref.shape[0]
    hw = n_heads * HEAD_DIM
    c = c_ref[...]
    cq = _rmsnorm_rows(c[:, :q_rank], gq_ref[...]).astype(MXU_DTYPE)
    ckv = _rmsnorm_rows(c[:, q_rank:q_rank + kv_rank], gkv_ref[...]).astype(MXU_DTYPE)
    q = jnp.dot(cq, wq_ref[...], preferred_element_type=F32)
    kv = jnp.dot(ckv, wkv_ref[...], preferred_element_type=F32)
    cos, sin = cos_ref[...], sin_ref[...]

    def rope(pair):
        return pair * cos + pltpu.roll(pair, ROPE_DIM, axis=1) * sin

    for h in range(n_heads):
        lo, hi = h * HEAD_DIM, (h + 1) * HEAD_DIM
        qn_ref[h] = (q[:, lo:hi] * scale).astype(qn_ref.dtype)
        qr_ref[h] = (rope(q[:, hw + lo:hw + hi]) * scale).astype(qr_ref.dtype)
        kn_ref[h] = kv[:, lo:hi].astype(kn_ref.dtype)
        v_ref[h] = kv[:, hw + lo:hw + hi].astype(v_ref.dtype)
    o = q_rank + kv_rank
    kr_ref[...] = rope(c[:, o:o + LANE]).astype(kr_ref.dtype)


def _mla_up(c, gq, gkv, wq_stack, wkv_stack, cos, sin, layer, n_heads, seq, scale):
    t, cw = c.shape
    q_rank, kv_rank = gq.shape[-1], gkv.shape[-1]
    tm = min(512, seq)
    per_seq = seq // tm
    heads = jax.ShapeDtypeStruct((n_heads, t, LANE), MXU_DTYPE)
    head_spec = pl.BlockSpec((n_heads, tm, LANE), lambda i: (0, i, 0))
    return pl.pallas_call(
        functools.partial(_mla_up_kernel, q_rank=q_rank, kv_rank=kv_rank, scale=scale),
        out_shape=(heads, heads, heads, heads, jax.ShapeDtypeStruct((t, LANE), MXU_DTYPE)),
        grid=(t // tm,),
        in_specs=[pl.BlockSpec((tm, cw), lambda i: (i, 0)),
                  pl.BlockSpec((1, q_rank), lambda i: (0, 0)),
                  pl.BlockSpec((1, kv_rank), lambda i: (0, 0)),
                  pl.BlockSpec((None,) + wq_stack.shape[1:], lambda i: (layer, 0, 0)),
                  pl.BlockSpec((None,) + wkv_stack.shape[1:], lambda i: (layer, 0, 0)),
                  pl.BlockSpec((tm, LANE), lambda i: (i % per_seq, 0)),
                  pl.BlockSpec((tm, LANE), lambda i: (i % per_seq, 0))],
        out_specs=(head_spec, head_spec, head_spec, head_spec,
                   pl.BlockSpec((tm, LANE), lambda i: (i, 0))),
        compiler_params=_params(("parallel",)),
        name="mla_up",
    )(c, gq.reshape(1, q_rank), gkv.reshape(1, kv_rank), wq_stack, wkv_stack, cos, sin)


def _fox_gate_kernel(f_ref, b_ref, qa_ref, ka_ref):
    n_heads = qa_ref.shape[0]
    z = f_ref[...] + b_ref[...]
    x = -(jnp.maximum(-z, 0.0) + jnp.log1p(jnp.exp(-jnp.abs(z))))
    s = z.shape[0]
    rows = lax.broadcasted_iota(jnp.int32, z.shape, 0)
    shift = 1
    while shift < s:
        x = x + jnp.where(rows >= shift, pltpu.roll(x, shift, axis=0), 0.0)
        shift *= 2
    x = x * LOG2E
    lane = lax.broadcasted_iota(jnp.int32, z.shape, 1)
    ones = jnp.where(lane < 3, 1.0, 0.0)
    for h in range(n_heads):
        hi, mid, lo = _split3(jnp.broadcast_to(x[:, h:h + 1], x.shape))
        pieces = jnp.where(lane == 0, hi, jnp.where(lane == 1, mid, jnp.where(lane == 2, lo, 0.0)))
        qa_ref[h] = (pieces + pltpu.roll(ones, 3, axis=1)).astype(qa_ref.dtype)
        ka_ref[h] = (ones - pltpu.roll(pieces, 3, axis=1)).astype(ka_ref.dtype)


def _fox_gate(c, b_f, f_col_block, batch, seq, n_heads):
    b_pad = jnp.zeros((1, LANE), F32).at[0, :n_heads].set(b_f)
    aux = jax.ShapeDtypeStruct((n_heads, batch * seq, LANE), MXU_DTYPE)
    aux_spec = pl.BlockSpec((n_heads, seq, LANE), lambda b: (0, b, 0))
    return pl.pallas_call(
        _fox_gate_kernel,
        out_shape=(aux, aux),
        grid=(batch,),
        in_specs=[pl.BlockSpec((seq, LANE), lambda b: (b, f_col_block)),
                  pl.BlockSpec((1, LANE), lambda b: (0, 0))],
        out_specs=(aux_spec, aux_spec),
        compiler_params=_params(("parallel",)),
        name="fox_gate",
    )(c, b_pad)


def _moba_query_aux(q, kmean_ref, qconst, slope2, blk):
    seq = q.shape[0]
    g = lax.dot_general(kmean_ref[...], q.astype(F32), (((1,), (1,)), ((), ())),
                        precision=lax.Precision.HIGHEST, preferred_element_type=F32)
    row = lax.broadcasted_iota(jnp.int32, g.shape, 0)
    bid = row & (MOBA_MAX_BLOCKS - 1)
    qblk = lax.broadcasted_iota(jnp.int32, g.shape, 1) // blk
    rank = jnp.zeros(g.shape, F32)
    for n in range(seq // blk - 1):
        gn = g[n:n + 1, :]
        beats = ((gn > g) | ((gn == g) & (n < bid))) & (n < qblk)
        rank = rank + jnp.where(beats, 1.0, 0.0)
    past = bid < qblk
    chosen = (rank < MOBA_TOPK) & past
    offset = (qblk - bid).astype(F32) * (-slope2 * blk)
    term = jnp.where(chosen, offset, jnp.where(past, NEG_INF, 0.0))
    hi = term.astype(MXU_DTYPE).astype(F32)
    lo = jnp.where(chosen, term - hi, 0.0)
    aux_t = jnp.where(row < MOBA_MAX_BLOCKS, hi, lo)
    aux = jnp.concatenate([aux_t, jnp.zeros((LANE - aux_t.shape[0], seq), F32)], axis=0).T
    return [(aux[r0:r0 + blk, :] + qconst).astype(MXU_DTYPE) for r0 in range(0, seq, blk)]


def _attention_kernel(*refs, moba, k2_shared):
    if moba:
        slopes_ref, q1_ref, k1_ref, v_ref, k2_ref, qc_ref, o_ref, kk_ref, vv_ref, kmean_ref = refs
    else:
        q1_ref, q2_ref, k1_ref, k2_ref, v_ref, o_ref, kk_ref, vv_ref = refs
    heads, seq, _ = k1_ref.shape
    tq = ATTN_BLOCK
    q2_tiles = []
    for hh in range(heads):
        kk_ref[hh, :, :HEAD_DIM] = k1_ref[hh]
        kk_ref[hh, :, HEAD_DIM:] = k2_ref[...] if k2_shared else k2_ref[hh]
        vv_ref[hh, :, :HEAD_DIM] = v_ref[hh]
        vv_ref[hh, :, HEAD_DIM:] = jnp.ones((seq, HEAD_DIM), vv_ref.dtype)
        if moba:
            slope2 = slopes_ref[pl.program_id(1) * heads + hh] * LOG2E
            kmean_ref[hh] = jnp.zeros(kmean_ref.shape[1:], F32)
            for n in range(seq // tq):
                mean = jnp.mean(k1_ref[hh, n * tq:(n + 1) * tq, :].astype(F32), axis=0, keepdims=True)
                kmean_ref[hh, n:n + 1, :] = mean
                kmean_ref[hh, MOBA_MAX_BLOCKS + n:MOBA_MAX_BLOCKS + n + 1, :] = mean
            q2_tiles.append(_moba_query_aux(q1_ref[hh], kmean_ref.at[hh], qc_ref[hh], slope2, tq))
    r = lax.broadcasted_iota(jnp.int32, (tq, tq), 0)
    c = lax.broadcasted_iota(jnp.int32, (tq, tq), 1)
    for qi in reversed(range(seq // tq)):
        rows = slice(qi * tq, (qi + 1) * tq)
        n = (qi + 1) * tq
        for hh in range(heads):
            q1 = q1_ref[hh, rows, :]
            q2 = q2_tiles[hh][qi] if moba else q2_ref[hh, rows, :]
            s = lax.dot_general(jnp.concatenate([q1, q2], axis=1), kk_ref[hh, :n, :],
                                (((1,), (1,)), ((), ())), preferred_element_type=F32)
            diag = jnp.where(c <= r, s[:, n - tq:], NEG_INF)
            s = diag if qi == 0 else jnp.concatenate([s[:, :n - tq], diag], axis=1)
            m = jnp.max(s, axis=-1, keepdims=True)
            p = jnp.exp2(s - m).astype(vv_ref.dtype)
            acc = jnp.dot(p, vv_ref[hh, :n, :], preferred_element_type=F32)
            o_ref[hh, rows, :] = (acc[:, :HEAD_DIM] / acc[:, HEAD_DIM:]).astype(o_ref.dtype)


def _attention(q1, q2, k1, k2, v, batch, seq, n_heads, q1_group=0, k1_group=0, v_group=0,
               k2_shared=False, moba_slopes=None):
    assert seq % ATTN_BLOCK == 0
    t = batch * seq
    moba = moba_slopes is not None
    hp = math.gcd(n_heads, ATTN_HEADS_PER_STEP)
    assert all(g % hp == 0 for g in (q1_group, k1_group, v_group))
    head = lambda g: pl.BlockSpec((hp, seq, HEAD_DIM), lambda b, h: (g // hp + h, b, 0))
    if moba:
        assert seq // ATTN_BLOCK <= MOBA_MAX_BLOCKS and ATTN_BLOCK == MOBA_BLOCK
        in_specs = [pl.BlockSpec(memory_space=pltpu.SMEM), head(q1_group), head(k1_group), head(v_group),
                    pl.BlockSpec((hp, seq, LANE), lambda b, h: (h, 0, 0)),
                    pl.BlockSpec((hp, ATTN_BLOCK, LANE), lambda b, h: (h, 0, 0))]
        args = (moba_slopes, q1, k1, v, k2, q2)
        scratch = [pltpu.VMEM((hp, 2 * MOBA_MAX_BLOCKS, HEAD_DIM), F32)]
    else:
        k2_spec = pl.BlockSpec((seq, LANE), lambda b, h: (b, 0)) if k2_shared else head(0)
        in_specs = [head(q1_group), head(0), head(k1_group), k2_spec, head(v_group)]
        args = (q1, q2, k1, k2, v)
        scratch = []
    return pl.pallas_call(
        functools.partial(_attention_kernel, moba=moba, k2_shared=k2_shared),
        out_shape=jax.ShapeDtypeStruct((n_heads, t, HEAD_DIM), MXU_DTYPE),
        grid=(batch, n_heads // hp),
        in_specs=in_specs,
        out_specs=head(0),
        scratch_shapes=[pltpu.VMEM((hp, seq, 2 * HEAD_DIM), MXU_DTYPE),
                        pltpu.VMEM((hp, seq, 2 * HEAD_DIM), MXU_DTYPE)] + scratch,
        compiler_params=_params(("parallel", "parallel")),
        name="moba_attention" if moba else "causal_attention",
    )(*args)


def _moba_constants(seq, n_heads):
    slopes = jnp.exp2(-8.0 * jnp.arange(1, n_heads + 1, dtype=F32) / n_heads)
    pieces = jnp.stack(_split3(slopes * LOG2E), axis=-1)
    pos = jnp.arange(seq)
    onehot = (pos[:, None] // MOBA_BLOCK == jnp.arange(MOBA_MAX_BLOCKS)[None, :]).astype(F32)
    within = jnp.broadcast_to((pos % MOBA_BLOCK).astype(F32)[None, :, None], (n_heads, seq, 3))
    k_aux = jnp.concatenate([jnp.broadcast_to(onehot, (n_heads, seq, MOBA_MAX_BLOCKS))] * 2
                            + [jnp.broadcast_to(-pieces[:, None, :], (n_heads, seq, 3)), within], axis=-1)
    q_aux = jnp.concatenate([jnp.zeros((n_heads, MOBA_BLOCK, 2 * MOBA_MAX_BLOCKS), F32),
                             within[:, :MOBA_BLOCK], jnp.broadcast_to(pieces[:, None, :], (n_heads, MOBA_BLOCK, 3))],
                            axis=-1)
    return slopes, _pad_last(k_aux, LANE).astype(MXU_DTYPE), _pad_last(q_aux, LANE)


def _sgu_kernel(u_ref, v_ref, lg_ref, lb_ref, w_ref, b_ref, o_ref):
    tm = u_ref.shape[0]
    nc = tm // SGU_CHUNK
    v = jax.nn.gelu(v_ref[...].astype(F32))
    vc = v - jnp.mean(v, axis=-1, keepdims=True)
    var = jnp.mean(vc * vc, axis=-1, keepdims=True)
    vn = (vc * lax.rsqrt(var + NORM_EPS) * lg_ref[...] + lb_ref[...]).astype(MXU_DTYPE)
    v_side = jnp.concatenate([vn[c * SGU_CHUNK:(c + 1) * SGU_CHUNK, :] for c in range(nc)], axis=1)
    r = lax.broadcasted_iota(jnp.int32, (SGU_CHUNK, SGU_CHUNK), 0)
    c_ix = lax.broadcasted_iota(jnp.int32, (SGU_CHUNK, SGU_CHUNK), 1)
    w = jnp.where(c_ix <= r, w_ref[...], 0.0).astype(MXU_DTYPE)
    mixed = jnp.dot(w, v_side, preferred_element_type=F32) + b_ref[...]
    u = jax.nn.gelu(u_ref[...].astype(F32))
    for c in range(nc):
        rows = slice(c * SGU_CHUNK, (c + 1) * SGU_CHUNK)
        o_ref[rows, :] = (u[rows, :] * mixed[:, c * HEAD_DIM:(c + 1) * HEAD_DIM]).astype(o_ref.dtype)


def _sgu(proj, ln_g, ln_b, w_s, b_s, n_heads, u_group, v_group):
    t = proj.shape[1]
    tm = min(2048, t)
    vec = pl.BlockSpec((None, 1, HEAD_DIM), lambda g, i: (g, 0, 0))
    return pl.pallas_call(
        _sgu_kernel,
        out_shape=jax.ShapeDtypeStruct((n_heads, t, HEAD_DIM), MXU_DTYPE),
        grid=(n_heads, t // tm),
        in_specs=[pl.BlockSpec((None, tm, HEAD_DIM), lambda g, i: (u_group + g, i, 0)),
                  pl.BlockSpec((None, tm, HEAD_DIM), lambda g, i: (v_group + g, i, 0)),
                  vec, vec,
                  pl.BlockSpec((None, SGU_CHUNK, SGU_CHUNK), lambda g, i: (g, 0, 0)),
                  pl.BlockSpec((None, SGU_CHUNK, 1), lambda g, i: (g, 0, 0))],
        out_specs=pl.BlockSpec((None, tm, HEAD_DIM), lambda g, i: (g, i, 0)),
        compiler_params=_params(("parallel", "parallel")),
        name="spatial_gating",
    )(proj, proj, ln_g[:, None, :], ln_b[:, None, :], w_s, b_s[:, :, None])


def _out_proj_kernel(*refs, n_mix):
    y_refs, (gn_ref, w_ref, resid_ref, o_ref, ob_ref, op_ref, a_ref) = refs[:n_mix], refs[n_mix:]
    per = y_refs[0].shape[0]
    width = per * LANE
    tm = a_ref.shape[0]

    @pl.when(pl.program_id(1) == 0)
    def _():
        for m, y_ref in enumerate(y_refs):
            for r0 in range(0, tm, NORM_ROW_CHUNK):
                rows = slice(r0, min(r0 + NORM_ROW_CHUNK, tm))
                parts = [y_ref[i, rows, :].astype(F32) for i in range(per)]
                ssq = jnp.sum(sum(p * p for p in parts), axis=-1, keepdims=True)
                rstd = lax.rsqrt(ssq / width + NORM_EPS)
                for i, p in enumerate(parts):
                    lo = m * width + i * LANE
                    a_ref[rows, lo:lo + LANE] = (p * rstd * gn_ref[m:m + 1, i * LANE:(i + 1) * LANE]).astype(a_ref.dtype)

    x_new = resid_ref[...] + jnp.dot(a_ref[...], w_ref[...], preferred_element_type=F32)
    o_ref[...] = x_new
    ob_ref[...] = x_new.astype(ob_ref.dtype)
    sq = x_new * x_new
    op_ref[...] = sum(sq[:, c0:c0 + LANE] for c0 in range(0, sq.shape[1], LANE))


def _out_proj(ys, group_gain, w_stack, layer, resid):
    t, n = resid.shape
    n_mix, width = group_gain.shape
    per = width // LANE
    tm, bn = min(1024, t), min(512, n)
    y_spec = pl.BlockSpec((per, tm, LANE), lambda i, j: (0, i, 0))
    tile = pl.BlockSpec((tm, bn), lambda i, j: (i, j))
    return pl.pallas_call(
        functools.partial(_out_proj_kernel, n_mix=n_mix),
        out_shape=(jax.ShapeDtypeStruct((t, n), F32), jax.ShapeDtypeStruct((t, n), MXU_DTYPE),
                   jax.ShapeDtypeStruct((t, n // bn * LANE), F32)),
        grid=(t // tm, n // bn),
        in_specs=[y_spec] * n_mix + [pl.BlockSpec((n_mix, width), lambda i, j: (0, 0)),
                                     pl.BlockSpec((None, n_mix * width, bn), lambda i, j: (layer, 0, j)),
                                     tile],
        out_specs=(tile, tile, pl.BlockSpec((tm, LANE), lambda i, j: (i, j))),
        scratch_shapes=[pltpu.VMEM((tm, n_mix * width), MXU_DTYPE)],
        compiler_params=_params(("parallel", "arbitrary")),
        name="out_proj",
    )(*ys, group_gain, w_stack, resid)


def _down_proj_norm_kernel(*refs, keep_x, k_total):
    if not keep_x:
        refs = refs + (refs[-1],)
    a_ref, w_ref, resid_ref, g_ref, ox_ref, oh_ref = refs
    k = pl.program_id(1)
    last = pl.num_programs(1) - 1
    tm, n = ox_ref.shape
    bk = a_ref.shape[1]
    k_rem = k_total % bk

    @pl.when(k == 0)
    def _():
        ox_ref[...] = resid_ref[...]

    def accumulate(depth):
        for c0 in range(0, n, PROJ_COL_CHUNK):
            cols = slice(c0, min(c0 + PROJ_COL_CHUNK, n))
            ox_ref[:, cols] += jnp.dot(a_ref[:, :depth], w_ref[:depth, cols], preferred_element_type=F32)

    if k_rem:
        pl.when(k < last)(lambda: accumulate(bk))
        pl.when(k == last)(lambda: accumulate(k_rem))
    else:
        accumulate(bk)

    @pl.when(k == last)
    def _():
        for r0 in range(0, tm, NORM_ROW_CHUNK):
            rows = slice(r0, min(r0 + NORM_ROW_CHUNK, tm))
            oh_ref[rows, :] = _rmsnorm_rows(ox_ref[rows, :], g_ref[...]).astype(oh_ref.dtype)


def _down_proj_norm(a, w_stack, layer, resid, g, h_dtype, keep_x=True):
    t, n = resid.shape
    k_total = a.shape[1]
    tm = min(512, t)
    bk = min(1024, k_total)
    row_block = pl.BlockSpec((tm, n), lambda i, k: (i, 0))
    h_shape = jax.ShapeDtypeStruct((t, n), h_dtype)
    assert keep_x or h_dtype == F32
    out_shape = (jax.ShapeDtypeStruct((t, n), F32), h_shape) if keep_x else h_shape
    return pl.pallas_call(
        functools.partial(_down_proj_norm_kernel, keep_x=keep_x, k_total=k_total),
        out_shape=out_shape,
        grid=(t // tm, pl.cdiv(k_total, bk)),
        in_specs=[pl.BlockSpec((tm, bk), lambda i, k: (i, k)),
                  pl.BlockSpec((None, bk, n), lambda i, k: (layer, k, 0)),
                  row_block,
                  pl.BlockSpec((1, n), lambda i, k: (0, 0))],
        out_specs=(row_block, row_block) if keep_x else row_block,
        compiler_params=_params(("parallel", "arbitrary"), DOWN_PROJ_VMEM_LIMIT_BYTES),
        name="down_proj_norm",
    )(a, w_stack, resid, g.reshape(1, n))


def _ffn_up_kernel(h_ref, halo_ref, p_ref, phalo_ref, wg_ref, wv_ref, cw_ref, cb_ref, o_ref, hext_ref,
                   rstd_ref, *, blocks_per_seq):
    i, j = pl.program_id(0), pl.program_id(1)
    tm, d = h_ref.shape

    def inv_rms(p):
        r = lax.rsqrt(jnp.sum(p, axis=-1, keepdims=True) / d + NORM_EPS)
        return jnp.broadcast_to(r, (p.shape[0], LANE))

    @pl.when(j == 0)
    def _():
        halo = halo_ref[...]
        first = i % blocks_per_seq == 0
        hext_ref[:CONV_HALO, :] = jnp.where(first, jnp.zeros_like(halo), halo)
        hext_ref[CONV_HALO:, :] = h_ref[...]
        rstd_ref[:CONV_HALO, :] = inv_rms(phalo_ref[...])
        rstd_ref[CONV_HALO:, :] = inv_rms(p_ref[...])

    def scale_rows(m, r):
        return jnp.concatenate([m[:, c0:c0 + LANE] * r for c0 in range(0, m.shape[1], LANE)], axis=1)

    g = jnp.dot(hext_ref[...], wg_ref[...], preferred_element_type=F32)
    g = scale_rows(g, rstd_ref[...])
    val = scale_rows(jnp.dot(h_ref[...], wv_ref[...], preferred_element_type=F32), rstd_ref[CONV_HALO:, :])
    cw = cw_ref[...]
    gate = cb_ref[...]
    for tap in range(CONV_WIDTH):
        lo = CONV_HALO - (CONV_WIDTH - 1) + tap
        gate = gate + cw[tap:tap + 1, :] * g[lo:lo + tm, :]
    o_ref[...] = (gate * jax.nn.sigmoid(gate) * val).astype(o_ref.dtype)


def _ffn_up(h, ssq, wg_stack, wv_stack, conv_w, conv_b, layer, seq):
    t, d = h.shape
    n_p = ssq.shape[1]
    n = wg_stack.shape[-1]
    tm = min(1024, seq)
    bn = min(512, n)
    per_seq = seq // tm
    halo_blocks = tm // CONV_HALO
    return pl.pallas_call(
        functools.partial(_ffn_up_kernel, blocks_per_seq=per_seq),
        out_shape=jax.ShapeDtypeStruct((t, n), MXU_DTYPE),
        grid=(t // tm, pl.cdiv(n, bn)),
        in_specs=[pl.BlockSpec((tm, d), lambda i, j: (i, 0)),
                  pl.BlockSpec((CONV_HALO, d), lambda i, j: (jnp.maximum(i * halo_blocks - 1, 0), 0)),
                  pl.BlockSpec((tm, n_p), lambda i, j: (i, 0)),
                  pl.BlockSpec((CONV_HALO, n_p), lambda i, j: (jnp.maximum(i * halo_blocks - 1, 0), 0)),
                  pl.BlockSpec((None, d, bn), lambda i, j: (layer, 0, j)),
                  pl.BlockSpec((None, d, bn), lambda i, j: (layer, 0, j)),
                  pl.BlockSpec((None, CONV_WIDTH, bn), lambda i, j: (layer, 0, j)),
                  pl.BlockSpec((None, 1, bn), lambda i, j: (layer, 0, j))],
        out_specs=pl.BlockSpec((tm, bn), lambda i, j: (i, j)),
        scratch_shapes=[pltpu.VMEM((tm + CONV_HALO, d), MXU_DTYPE), pltpu.VMEM((tm + CONV_HALO, LANE), F32)],
        compiler_params=_params(("parallel", "arbitrary")),
        name="ffn_up_conv",
    )(h, h, ssq, ssq, wg_stack, wv_stack, conv_w, conv_b)


def _pad_last(w, width):
    return jnp.pad(w, [(0, 0)] * (w.ndim - 1) + [(0, width - w.shape[-1])])


def _rotate_half_columns(w):
    half = ROPE_DIM // 2
    return jnp.concatenate([-w[..., half:], w[..., :half]], axis=-1)


def _pack_weights(norm_mix_g, w_in, mla_w_uq, mla_w_ukv, w_o, norm_ffn_g, w_gate, w_val, conv_b, w_down,
                  group_width, q_rank, kv_rank, n_heads):
    gw = group_width
    a_end = 3 * gw
    b_end = a_end + 2 * gw
    cq_end = b_end + q_rank
    ckv_end = cq_end + kv_rank
    kr_end = ckv_end + ROPE_DIM
    d_end = kr_end + 3 * gw
    layers, qr, _ = mla_w_uq.shape
    head_segments = ((0, b_end), (kr_end, 3 * gw))
    half = ROPE_DIM // 2
    w_in_t, w_latent = [], []
    for l in range(layers):
        wt = (jnp.swapaxes(w_in[l], 0, 1) * norm_mix_g[l][None, :]).astype(MXU_DTYPE)
        kr = wt[ckv_end:kr_end]
        forget = wt[d_end:]
        w_in_t.append(wt)
        w_latent.append(jnp.concatenate(
            [wt[b_end:ckv_end], kr, -kr[half:], kr[:half],
             jnp.pad(forget, ((0, LANE - forget.shape[0]), (0, 0)))], axis=0))

    uq = mla_w_uq.reshape(layers, qr, n_heads, HEAD_DIM + ROPE_DIM)
    nope, rope = uq[..., :HEAD_DIM], uq[..., HEAD_DIM:]
    flat = lambda w: w.reshape(layers, w.shape[1], n_heads * HEAD_DIM)
    wq = jnp.concatenate([flat(nope), flat(jnp.concatenate([rope, _rotate_half_columns(rope)], axis=-1))],
                         axis=-1).astype(MXU_DTYPE)
    ukv = mla_w_ukv.reshape(layers, kv_rank, n_heads, 2 * HEAD_DIM)
    wkv = jnp.concatenate([flat(ukv[..., :HEAD_DIM]), flat(ukv[..., HEAD_DIM:])], axis=-1).astype(MXU_DTYPE)

    ffn_gain = norm_ffn_g[:, :, None]
    return (w_in_t, head_segments, w_latent, wq, wkv, w_o.astype(MXU_DTYPE),
            (w_gate * ffn_gain).astype(MXU_DTYPE), (w_val * ffn_gain).astype(MXU_DTYPE),
            w_down.astype(MXU_DTYPE), conv_b[:, None, :])


def _rope_tables(seq):
    half = ROPE_DIM // 2
    inv_freq = ROPE_THETA ** (-jnp.arange(half, dtype=F32) / half)
    ang = jnp.arange(seq, dtype=F32)[:, None] * inv_freq[None, :]
    cos = jnp.concatenate([jnp.cos(ang), jnp.cos(ang)], axis=-1)
    sin = jnp.concatenate([jnp.sin(ang), jnp.sin(ang)], axis=-1)
    return _pad_last(cos, LANE), _pad_last(sin, LANE)


def kernel(x, norm_mix_g, w_in, sgu_ln_g, sgu_ln_b, sgu_w, sgu_b, mla_q_norm_g, mla_kv_norm_g,
           mla_w_uq, mla_w_ukv, fox_b_f, group_norm_g, w_o, norm_ffn_g, w_gate, w_val, conv_w,
           conv_b, w_down, final_norm_g):
    batch, seq, d_model = x.shape
    depth = w_in.shape[0]
    n_heads = fox_b_f.shape[-1]
    gw = group_norm_g.shape[-1]
    q_rank, kv_rank = mla_q_norm_g.shape[-1], mla_kv_norm_g.shape[-1]
    assert gw == n_heads * HEAD_DIM and seq % MOBA_BLOCK == 0
    assert q_rank % LANE == 0 and kv_rank % LANE == 0

    (w_in_t, head_segments, w_latent, wq, wkv, wo, wg, wv, wd, cb) = _pack_weights(
        norm_mix_g, w_in, mla_w_uq, mla_w_ukv, w_o, norm_ffn_g, w_gate, w_val, conv_b, w_down, gw, q_rank,
        kv_rank, n_heads)
    cos, sin = _rope_tables(seq)
    slopes, moba_k_aux, moba_q_aux = _moba_constants(seq, n_heads)
    f_col_block = (q_rank + kv_rank + LANE) // LANE
    mla_scale = (HEAD_DIM + ROPE_DIM) ** -0.5 * LOG2E
    a_q, a_k, a_v, b_u, b_v, d_q, d_k, d_v = (i * n_heads for i in range(8))
    q_scale, q_starts = HEAD_DIM ** -0.5 * LOG2E, (a_q * HEAD_DIM, d_q * HEAD_DIM)

    xf = x.reshape(batch * seq, d_model)
    no_gain = jnp.ones((d_model,), F32)
    h = _rmsnorm(xf, no_gain, MXU_DTYPE)
    for l in range(depth):
        proj = _matmul_groups(h, w_in_t[l], 1024, min(1024, gw), head_segments, q_starts, q_scale)
        latent = _matmul_f32(h, w_latent[l], 512)
        qn, qr, kn, vv, kr = _mla_up(latent, mla_q_norm_g[l], mla_kv_norm_g[l], wq, wkv, cos, sin,
                                     l, n_heads, seq, mla_scale)
        fox_q_aux, fox_k_aux = _fox_gate(latent, fox_b_f[l], f_col_block, batch, seq, n_heads)
        y_a = _attention(proj, moba_q_aux, proj, moba_k_aux, proj, batch, seq, n_heads,
                         a_q, a_k, a_v, moba_slopes=slopes)
        y_b = _sgu(proj, sgu_ln_g[l], sgu_ln_b[l], sgu_w[l], sgu_b[l], n_heads, b_u, b_v)
        y_c = _attention(qn, qr, kn, kr, vv, batch, seq, n_heads, k2_shared=True)
        y_d = _attention(proj, fox_q_aux, proj, fox_k_aux, proj, batch, seq, n_heads, d_q, d_k, d_v)
        xf, xb, ssq = _out_proj((y_a, y_b, y_c, y_d), group_norm_g[l], wo, l, xf)
        act = _ffn_up(xb, ssq, wg, wv, conv_w, cb, l, seq)
        if l == depth - 1:
            h = _down_proj_norm(act, wd, l, xf, final_norm_g, F32, keep_x=False)
        else:
            xf, h = _down_proj_norm(act, wd, l, xf, no_gain, MXU_DTYPE)
    return h.reshape(batch, seq, d_model)
```

```python
import functools
import math

import jax
import jax.numpy as jnp
from jax import lax
from jax.experimental import pallas as pl
from jax.experimental.pallas import tpu as pltpu

F32 = jnp.float32
MXU_DTYPE = jnp.bfloat16

LANE = 128
HEAD_DIM = 128
MOBA_BLOCK = 256
MOBA_TOPK = 3
SGU_CHUNK = 128
ROPE_DIM = 64
ROPE_THETA = 10000.0
CONV_WIDTH = 3
NORM_EPS = 1e-6
NEG_INF = -1e30
LOG2E = math.log2(math.e)

NORM_ROWS = 512
IN_PROJ_TILE = (1024, 1024)
LATENT_ROWS = 512
MLA_UP_ROWS = 512
SGU_ROWS = 2048
OUT_PROJ_TILE = (1024, 512)
FFN_TILE = (1024, 512)
DOWN_PROJ_ROWS, DOWN_PROJ_DEPTH = 512, 1024
ATTN_BLOCK = 256
MOBA_MAX_BLOCKS = 8
ATTN_HEADS_PER_STEP = 4
CONV_HALO = 16
PROJ_COL_CHUNK = 512
NORM_ROW_CHUNK = 128
VMEM_LIMIT_BYTES = 56 * 1024 * 1024
DOWN_PROJ_VMEM_LIMIT_BYTES = 62 * 1024 * 1024


def _params(semantics, vmem_limit_bytes=VMEM_LIMIT_BYTES):
    return pltpu.CompilerParams(dimension_semantics=semantics, vmem_limit_bytes=vmem_limit_bytes)


def _rmsnorm_rows(x, g):
    ms = jnp.mean(x * x, axis=-1, keepdims=True)
    return x * lax.rsqrt(ms + NORM_EPS) * g


def _split3(x):
    hi = x.astype(MXU_DTYPE).astype(F32)
    r = x - hi
    mid = r.astype(MXU_DTYPE).astype(F32)
    lo = (r - mid).astype(MXU_DTYPE).astype(F32)
    return hi, mid, lo


def _norm_kernel(x_ref, g_ref, o_ref):
    o_ref[...] = _rmsnorm_rows(x_ref[...], g_ref[...]).astype(o_ref.dtype)


def _rmsnorm(x, g, out_dtype):
    t, d = x.shape
    tm = min(NORM_ROWS, t)
    return pl.pallas_call(
        _norm_kernel,
        out_shape=jax.ShapeDtypeStruct((t, d), out_dtype),
        grid=(t // tm,),
        in_specs=[pl.BlockSpec((tm, d), lambda i: (i, 0)),
                  pl.BlockSpec((1, d), lambda i: (0, 0))],
        out_specs=pl.BlockSpec((tm, d), lambda i: (i, 0)),
        compiler_params=_params(("parallel",)),
        name="rmsnorm",
    )(x, g.reshape(1, d))


def _dot_nt(a, w_t):
    return lax.dot_general(a, w_t, (((1,), (1,)), ((), ())), preferred_element_type=F32)


def _mm_groups_kernel(a_ref, w_ref, o_ref, *, scaled_blocks, scale):
    acc = _dot_nt(a_ref[...], w_ref[...])
    j = pl.program_id(1)
    is_scaled = functools.reduce(jnp.logical_or, [j == b for b in scaled_blocks])
    acc = acc * jnp.where(is_scaled, scale, 1.0)
    for g in range(o_ref.shape[0]):
        o_ref[g] = acc[:, g * LANE:(g + 1) * LANE].astype(o_ref.dtype)


def _matmul_groups(a, w_t, tm, bn, segments, scaled_starts, scale):
    t, k = a.shape
    tm = min(tm, t)
    assert all(rows % bn == 0 for _, rows in segments) and all(s0 % bn == 0 for s0 in scaled_starts)
    n = sum(rows for _, rows in segments)

    def row_start(j):
        r = j * bn
        shift, done = 0, 0
        for r0, rows in segments:
            r = r + jnp.where(j >= done // bn, (r0 - done) - shift, 0)
            shift, done = r0 - done, done + rows
        return pl.multiple_of(r, math.gcd(bn, *(r0 for r0, _ in segments)))

    return pl.pallas_call(
        functools.partial(_mm_groups_kernel, scaled_blocks=[s0 // bn for s0 in scaled_starts], scale=scale),
        out_shape=jax.ShapeDtypeStruct((n // LANE, t, LANE), MXU_DTYPE),
        grid=(t // tm, n // bn),
        in_specs=[pl.BlockSpec((tm, k), lambda i, j: (i, 0)),
                  pl.BlockSpec((pl.Element(bn), pl.Element(k)), lambda i, j: (row_start(j), 0))],
        out_specs=pl.BlockSpec((bn // LANE, tm, LANE), lambda i, j: (j, i, 0)),
        compiler_params=_params(("parallel", "arbitrary")),
        name="in_proj_heads",
    )(a, w_t)


def _mm_plain_kernel(a_ref, w_ref, o_ref):
    o_ref[...] = _dot_nt(a_ref[...], w_ref[...]).astype(o_ref.dtype)


def _matmul_f32(a, w_t, tm):
    t, k = a.shape
    n = w_t.shape[0]
    tm = min(tm, t)
    return pl.pallas_call(
        _mm_plain_kernel,
        out_shape=jax.ShapeDtypeStruct((t, n), F32),
        grid=(t // tm,),
        in_specs=[pl.BlockSpec((tm, k), lambda i: (i, 0)),
                  pl.BlockSpec((n, k), lambda i: (0, 0))],
        out_specs=pl.BlockSpec((tm, n), lambda i: (i, 0)),
        compiler_params=_params(("parallel",)),
        name="in_proj_latent",
    )(a, w_t)


def _mla_up_kernel(c_ref, gq_ref, gkv_ref, wq_ref, wkv_ref, cos_ref, sin_ref,
                   qn_ref, qr_ref, kn_ref, v_ref, kr_ref, *, q_rank, kv_rank, scale):
    n_heads = qn_ref.shape[0]
    hw = n_heads * HEAD_DIM
    c = c_ref[...]
    cq = _rmsnorm_rows(c[:, :q_rank], gq_ref[...]).astype(MXU_DTYPE)
    ckv = _rmsnorm_rows(c[:, q_rank:q_rank + kv_rank], gkv_ref[...]).astype(MXU_DTYPE)
    q = jnp.dot(cq, wq_ref[...], preferred_element_type=F32)
    kv = jnp.dot(ckv, wkv_ref[...], preferred_element_type=F32)
    cos, sin = cos_ref[...], sin_ref[...]

    def rope(pair):
        return pair * cos + pltpu.roll(pair, ROPE_DIM, axis=1) * sin

    for h in range(n_heads):
        lo, hi = h * HEAD_DIM, (h + 1) * HEAD_DIM
        qn_ref[h] = (q[:, lo:hi] * scale).astype(qn_ref.dtype)
        qr_ref[h] = (rope(q[:, hw + lo:hw + hi]) * scale).astype(qr_ref.dtype)
        kn_ref[h] = kv[:, lo:hi].astype(kn_ref.dtype)
        v_ref[h] = kv[:, hw + lo:hw + hi].astype(v_ref.dtype)
    o = q_rank + kv_rank
    kr_ref[...] = rope(c[:, o:o + LANE]).astype(kr_ref.dtype)


def _mla_up(c, gq, gkv, wq_stack, wkv_stack, cos, sin, layer, n_heads, seq, scale):
    t, cw = c.shape
    q_rank, kv_rank = gq.shape[-1], gkv.shape[-1]
    tm = min(MLA_UP_ROWS, seq)
    per_seq = seq // tm
    heads = jax.ShapeDtypeStruct((n_heads, t, LANE), MXU_DTYPE)
    head_spec = pl.BlockSpec((n_heads, tm, LANE), lambda i: (0, i, 0))
    return pl.pallas_call(
        functools.partial(_mla_up_kernel, q_rank=q_rank, kv_rank=kv_rank, scale=scale),
        out_shape=(heads, heads, heads, heads, jax.ShapeDtypeStruct((t, LANE), MXU_DTYPE)),
        grid=(t // tm,),
        in_specs=[pl.BlockSpec((tm, cw), lambda i: (i, 0)),
                  pl.BlockSpec((1, q_rank), lambda i: (0, 0)),
                  pl.BlockSpec((1, kv_rank), lambda i: (0, 0)),
                  pl.BlockSpec((None,) + wq_stack.shape[1:], lambda i: (layer, 0, 0)),
                  pl.BlockSpec((None,) + wkv_stack.shape[1:], lambda i: (layer, 0, 0)),
                  pl.BlockSpec((tm, LANE), lambda i: (i % per_seq, 0)),
                  pl.BlockSpec((tm, LANE), lambda i: (i % per_seq, 0))],
        out_specs=(head_spec, head_spec, head_spec, head_spec,
                   pl.BlockSpec((tm, LANE), lambda i: (i, 0))),
        compiler_params=_params(("parallel",)),
        name="mla_up",
    )(c, gq.reshape(1, q_rank), gkv.reshape(1, kv_rank), wq_stack, wkv_stack, cos, sin)


def _fox_gate_kernel(f_ref, b_ref, qa_ref, ka_ref):
    n_heads = qa_ref.shape[0]
    z = f_ref[...] + b_ref[...]
    x = -(jnp.maximum(-z, 0.0) + jnp.log1p(jnp.exp(-jnp.abs(z))))
    s = z.shape[0]
    rows = lax.broadcasted_iota(jnp.int32, z.shape, 0)
    shift = 1
    while shift < s:
        x = x + jnp.where(rows >= shift, pltpu.roll(x, shift, axis=0), 0.0)
        shift *= 2
    x = x * LOG2E
    lane = lax.broadcasted_iota(jnp.int32, z.shape, 1)
    ones = jnp.where(lane < 3, 1.0, 0.0)
    for h in range(n_heads):
        hi, mid, lo = _split3(jnp.broadcast_to(x[:, h:h + 1], x.shape))
        pieces = jnp.where(lane == 0, hi, jnp.where(lane == 1, mid, jnp.where(lane == 2, lo, 0.0)))
        qa_ref[h] = (pieces + pltpu.roll(ones, 3, axis=1)).astype(qa_ref.dtype)
        ka_ref[h] = (ones - pltpu.roll(pieces, 3, axis=1)).astype(ka_ref.dtype)


def _fox_gate(c, b_f, f_col_block, batch, seq, n_heads):
    b_pad = jnp.zeros((1, LANE), F32).at[0, :n_heads].set(b_f)
    aux = jax.ShapeDtypeStruct((n_heads, batch * seq, LANE), MXU_DTYPE)
    aux_spec = pl.BlockSpec((n_heads, seq, LANE), lambda b: (0, b, 0))
    return pl.pallas_call(
        _fox_gate_kernel,
        out_shape=(aux, aux),
        grid=(batch,),
        in_specs=[pl.BlockSpec((seq, LANE), lambda b: (b, f_col_block)),
                  pl.BlockSpec((1, LANE), lambda b: (0, 0))],
        out_specs=(aux_spec, aux_spec),
        compiler_params=_params(("parallel",)),
        name="fox_gate",
    )(c, b_pad)


def _moba_query_aux(q, kmean_ref, qconst, slope2, blk):
    seq = q.shape[0]
    g = lax.dot_general(kmean_ref[...], q.astype(F32), (((1,), (1,)), ((), ())),
                        precision=lax.Precision.HIGHEST, preferred_element_type=F32)
    row = lax.broadcasted_iota(jnp.int32, g.shape, 0)
    bid = row & (MOBA_MAX_BLOCKS - 1)
    qblk = lax.broadcasted_iota(jnp.int32, g.shape, 1) // blk
    rank = jnp.zeros(g.shape, F32)
    for n in range(seq // blk - 1):
        gn = g[n:n + 1, :]
        beats = ((gn > g) | ((gn == g) & (n < bid))) & (n < qblk)
        rank = rank + jnp.where(beats, 1.0, 0.0)
    past = bid < qblk
    chosen = (rank < MOBA_TOPK) & past
    offset = (qblk - bid).astype(F32) * (-slope2 * blk)
    term = jnp.where(chosen, offset, jnp.where(past, NEG_INF, 0.0))
    hi = term.astype(MXU_DTYPE).astype(F32)
    lo = jnp.where(chosen, term - hi, 0.0)
    aux_t = jnp.where(row < MOBA_MAX_BLOCKS, hi, lo)
    aux = jnp.concatenate([aux_t, jnp.zeros((LANE - aux_t.shape[0], seq), F32)], axis=0).T
    return [(aux[r0:r0 + blk, :] + qconst).astype(MXU_DTYPE) for r0 in range(0, seq, blk)]


def _attention_kernel(*refs, moba, k2_shared):
    if moba:
        slopes_ref, q1_ref, k1_ref, v_ref, k2_ref, qc_ref, o_ref, kk_ref, vv_ref, kmean_ref = refs
    else:
        q1_ref, q2_ref, k1_ref, k2_ref, v_ref, o_ref, kk_ref, vv_ref = refs
    heads, seq, _ = k1_ref.shape
    tq = ATTN_BLOCK
    q2_tiles = []
    for hh in range(heads):
        kk_ref[hh, :, :HEAD_DIM] = k1_ref[hh]
        kk_ref[hh, :, HEAD_DIM:] = k2_ref[...] if k2_shared else k2_ref[hh]
        vv_ref[hh, :, :HEAD_DIM] = v_ref[hh]
        vv_ref[hh, :, HEAD_DIM:] = jnp.ones((seq, HEAD_DIM), vv_ref.dtype)
        if moba:
            slope2 = slopes_ref[pl.program_id(1) * heads + hh] * LOG2E
            kmean_ref[hh] = jnp.zeros(kmean_ref.shape[1:], F32)
            for n in range(seq // tq):
                mean = jnp.mean(k1_ref[hh, n * tq:(n + 1) * tq, :].astype(F32), axis=0, keepdims=True)
                kmean_ref[hh, n:n + 1, :] = mean
                kmean_ref[hh, MOBA_MAX_BLOCKS + n:MOBA_MAX_BLOCKS + n + 1, :] = mean
            q2_tiles.append(_moba_query_aux(q1_ref[hh], kmean_ref.at[hh], qc_ref[hh], slope2, tq))
    r = lax.broadcasted_iota(jnp.int32, (tq, tq), 0)
    c = lax.broadcasted_iota(jnp.int32, (tq, tq), 1)
    for qi in reversed(range(seq // tq)):
        rows = slice(qi * tq, (qi + 1) * tq)
        n = (qi + 1) * tq
        for hh in range(heads):
            q1 = q1_ref[hh, rows, :]
            q2 = q2_tiles[hh][qi] if moba else q2_ref[hh, rows, :]
            s = lax.dot_general(jnp.concatenate([q1, q2], axis=1), kk_ref[hh, :n, :],
                                (((1,), (1,)), ((), ())), preferred_element_type=F32)
            diag = jnp.where(c <= r, s[:, n - tq:], NEG_INF)
            s = diag if qi == 0 else jnp.concatenate([s[:, :n - tq], diag], axis=1)
            m = jnp.max(s, axis=-1, keepdims=True)
            p = jnp.exp2(s - m).astype(vv_ref.dtype)
            acc = jnp.dot(p, vv_ref[hh, :n, :], preferred_element_type=F32)
            o_ref[hh, rows, :] = (acc[:, :HEAD_DIM] / acc[:, HEAD_DIM:]).astype(o_ref.dtype)


def _attention(q1, q2, k1, k2, v, batch, seq, n_heads, q1_group=0, k1_group=0, v_group=0,
               k2_shared=False, moba_slopes=None):
    assert seq % ATTN_BLOCK == 0
    t = batch * seq
    moba = moba_slopes is not None
    hp = math.gcd(n_heads, ATTN_HEADS_PER_STEP)
    assert all(g % hp == 0 for g in (q1_group, k1_group, v_group))
    head = lambda g: pl.BlockSpec((hp, seq, HEAD_DIM), lambda b, h: (g // hp + h, b, 0))
    if moba:
        assert seq // ATTN_BLOCK <= MOBA_MAX_BLOCKS and ATTN_BLOCK == MOBA_BLOCK
        in_specs = [pl.BlockSpec(memory_space=pltpu.SMEM), head(q1_group), head(k1_group), head(v_group),
                    pl.BlockSpec((hp, seq, LANE), lambda b, h: (h, 0, 0)),
                    pl.BlockSpec((hp, ATTN_BLOCK, LANE), lambda b, h: (h, 0, 0))]
        args = (moba_slopes, q1, k1, v, k2, q2)
        scratch = [pltpu.VMEM((hp, 2 * MOBA_MAX_BLOCKS, HEAD_DIM), F32)]
    else:
        k2_spec = pl.BlockSpec((seq, LANE), lambda b, h: (b, 0)) if k2_shared else head(0)
        in_specs = [head(q1_group), head(0), head(k1_group), k2_spec, head(v_group)]
        args = (q1, q2, k1, k2, v)
        scratch = []
    return pl.pallas_call(
        functools.partial(_attention_kernel, moba=moba, k2_shared=k2_shared),
        out_shape=jax.ShapeDtypeStruct((n_heads, t, HEAD_DIM), MXU_DTYPE),
        grid=(batch, n_heads // hp),
        in_specs=in_specs,
        out_specs=head(0),
        scratch_shapes=[pltpu.VMEM((hp, seq, 2 * HEAD_DIM), MXU_DTYPE),
                        pltpu.VMEM((hp, seq, 2 * HEAD_DIM), MXU_DTYPE)] + scratch,
        compiler_params=_params(("parallel", "parallel")),
        name="moba_attention" if moba else "causal_attention",
    )(*args)


def _moba_constants(seq, n_heads):
    slopes = jnp.exp2(-8.0 * jnp.arange(1, n_heads + 1, dtype=F32) / n_heads)
    pieces = jnp.stack(_split3(slopes * LOG2E), axis=-1)
    pos = jnp.arange(seq)
    onehot = (pos[:, None] // MOBA_BLOCK == jnp.arange(MOBA_MAX_BLOCKS)[None, :]).astype(F32)
    within = jnp.broadcast_to((pos % MOBA_BLOCK).astype(F32)[None, :, None], (n_heads, seq, 3))
    k_aux = jnp.concatenate([jnp.broadcast_to(onehot, (n_heads, seq, MOBA_MAX_BLOCKS))] * 2
                            + [jnp.broadcast_to(-pieces[:, None, :], (n_heads, seq, 3)), within], axis=-1)
    q_aux = jnp.concatenate([jnp.zeros((n_heads, MOBA_BLOCK, 2 * MOBA_MAX_BLOCKS), F32),
                             within[:, :MOBA_BLOCK], jnp.broadcast_to(pieces[:, None, :], (n_heads, MOBA_BLOCK, 3))],
                            axis=-1)
    return slopes, _pad_last(k_aux, LANE).astype(MXU_DTYPE), _pad_last(q_aux, LANE)


def _sgu_kernel(u_ref, v_ref, lg_ref, lb_ref, w_ref, b_ref, o_ref):
    tm = u_ref.shape[0]
    nc = tm // SGU_CHUNK
    v = jax.nn.gelu(v_ref[...].astype(F32))
    vc = v - jnp.mean(v, axis=-1, keepdims=True)
    var = jnp.mean(vc * vc, axis=-1, keepdims=True)
    vn = (vc * lax.rsqrt(var + NORM_EPS) * lg_ref[...] + lb_ref[...]).astype(MXU_DTYPE)
    v_side = jnp.concatenate([vn[c * SGU_CHUNK:(c + 1) * SGU_CHUNK, :] for c in range(nc)], axis=1)
    r = lax.broadcasted_iota(jnp.int32, (SGU_CHUNK, SGU_CHUNK), 0)
    c_ix = lax.broadcasted_iota(jnp.int32, (SGU_CHUNK, SGU_CHUNK), 1)
    w = jnp.where(c_ix <= r, w_ref[...], 0.0).astype(MXU_DTYPE)
    mixed = jnp.dot(w, v_side, preferred_element_type=F32) + b_ref[...]
    u = jax.nn.gelu(u_ref[...].astype(F32))
    for c in range(nc):
        rows = slice(c * SGU_CHUNK, (c + 1) * SGU_CHUNK)
        o_ref[rows, :] = (u[rows, :] * mixed[:, c * HEAD_DIM:(c + 1) * HEAD_DIM]).astype(o_ref.dtype)


def _sgu(proj, ln_g, ln_b, w_s, b_s, n_heads, u_group, v_group):
    t = proj.shape[1]
    tm = min(SGU_ROWS, t)
    vec = pl.BlockSpec((None, 1, HEAD_DIM), lambda g, i: (g, 0, 0))
    return pl.pallas_call(
        _sgu_kernel,
        out_shape=jax.ShapeDtypeStruct((n_heads, t, HEAD_DIM), MXU_DTYPE),
        grid=(n_heads, t // tm),
        in_specs=[pl.BlockSpec((None, tm, HEAD_DIM), lambda g, i: (u_group + g, i, 0)),
                  pl.BlockSpec((None, tm, HEAD_DIM), lambda g, i: (v_group + g, i, 0)),
                  vec, vec,
                  pl.BlockSpec((None, SGU_CHUNK, SGU_CHUNK), lambda g, i: (g, 0, 0)),
                  pl.BlockSpec((None, SGU_CHUNK, 1), lambda g, i: (g, 0, 0))],
        out_specs=pl.BlockSpec((None, tm, HEAD_DIM), lambda g, i: (g, i, 0)),
        compiler_params=_params(("parallel", "parallel")),
        name="spatial_gating",
    )(proj, proj, ln_g[:, None, :], ln_b[:, None, :], w_s, b_s[:, :, None])


def _out_proj_kernel(*refs, n_mix):
    y_refs, (gn_ref, w_ref, resid_ref, o_ref, ob_ref, op_ref, a_ref) = refs[:n_mix], refs[n_mix:]
    per = y_refs[0].shape[0]
    width = per * LANE
    tm = a_ref.shape[0]

    @pl.when(pl.program_id(1) == 0)
    def _():
        for m, y_ref in enumerate(y_refs):
            for r0 in range(0, tm, NORM_ROW_CHUNK):
                rows = slice(r0, min(r0 + NORM_ROW_CHUNK, tm))
                parts = [y_ref[i, rows, :].astype(F32) for i in range(per)]
                ssq = jnp.sum(sum(p * p for p in parts), axis=-1, keepdims=True)
                rstd = lax.rsqrt(ssq / width + NORM_EPS)
                for i, p in enumerate(parts):
                    lo = m * width + i * LANE
                    a_ref[rows, lo:lo + LANE] = (p * rstd * gn_ref[m:m + 1, i * LANE:(i + 1) * LANE]).astype(a_ref.dtype)

    x_new = resid_ref[...] + jnp.dot(a_ref[...], w_ref[...], preferred_element_type=F32)
    o_ref[...] = x_new
    ob_ref[...] = x_new.astype(ob_ref.dtype)
    sq = x_new * x_new
    op_ref[...] = sum(sq[:, c0:c0 + LANE] for c0 in range(0, sq.shape[1], LANE))


def _out_proj(ys, group_gain, w_stack, layer, resid):
    t, n = resid.shape
    n_mix, width = group_gain.shape
    per = width // LANE
    tm, bn = min(OUT_PROJ_TILE[0], t), min(OUT_PROJ_TILE[1], n)
    y_spec = pl.BlockSpec((per, tm, LANE), lambda i, j: (0, i, 0))
    tile = pl.BlockSpec((tm, bn), lambda i, j: (i, j))
    return pl.pallas_call(
        functools.partial(_out_proj_kernel, n_mix=n_mix),
        out_shape=(jax.ShapeDtypeStruct((t, n), F32), jax.ShapeDtypeStruct((t, n), MXU_DTYPE),
                   jax.ShapeDtypeStruct((t, n // bn * LANE), F32)),
        grid=(t // tm, n // bn),
        in_specs=[y_spec] * n_mix + [pl.BlockSpec((n_mix, width), lambda i, j: (0, 0)),
                                     pl.BlockSpec((None, n_mix * width, bn), lambda i, j: (layer, 0, j)),
                                     tile],
        out_specs=(tile, tile, pl.BlockSpec((tm, LANE), lambda i, j: (i, j))),
        scratch_shapes=[pltpu.VMEM((tm, n_mix * width), MXU_DTYPE)],
        compiler_params=_params(("parallel", "arbitrary")),
        name="out_proj",
    )(*ys, group_gain, w_stack, resid)


def _down_proj_norm_kernel(*refs, keep_x, k_total):
    if not keep_x:
        refs = refs + (refs[-1],)
    a_ref, w_ref, resid_ref, g_ref, ox_ref, oh_ref = refs
    k = pl.program_id(1)
    last = pl.num_programs(1) - 1
    tm, n = ox_ref.shape
    bk = a_ref.shape[1]
    k_rem = k_total % bk

    @pl.when(k == 0)
    def _():
        ox_ref[...] = resid_ref[...]

    def accumulate(depth):
        for c0 in range(0, n, PROJ_COL_CHUNK):
            cols = slice(c0, min(c0 + PROJ_COL_CHUNK, n))
            ox_ref[:, cols] += jnp.dot(a_ref[:, :depth], w_ref[:depth, cols], preferred_element_type=F32)

    if k_rem:
        pl.when(k < last)(lambda: accumulate(bk))
        pl.when(k == last)(lambda: accumulate(k_rem))
    else:
        accumulate(bk)

    @pl.when(k == last)
    def _():
        for r0 in range(0, tm, NORM_ROW_CHUNK):
            rows = slice(r0, min(r0 + NORM_ROW_CHUNK, tm))
            oh_ref[rows, :] = _rmsnorm_rows(ox_ref[rows, :], g_ref[...]).astype(oh_ref.dtype)


def _down_proj_norm(a, w_stack, layer, resid, g, h_dtype, keep_x=True):
    t, n = resid.shape
    k_total = a.shape[1]
    tm = min(DOWN_PROJ_ROWS, t)
    bk = min(DOWN_PROJ_DEPTH, k_total)
    row_block = pl.BlockSpec((tm, n), lambda i, k: (i, 0))
    h_shape = jax.ShapeDtypeStruct((t, n), h_dtype)
    assert keep_x or h_dtype == F32
    out_shape = (jax.ShapeDtypeStruct((t, n), F32), h_shape) if keep_x else h_shape
    return pl.pallas_call(
        functools.partial(_down_proj_norm_kernel, keep_x=keep_x, k_total=k_total),
        out_shape=out_shape,
        grid=(t // tm, pl.cdiv(k_total, bk)),
        in_specs=[pl.BlockSpec((tm, bk), lambda i, k: (i, k)),
                  pl.BlockSpec((None, bk, n), lambda i, k: (layer, k, 0)),
                  row_block,
                  pl.BlockSpec((1, n), lambda i, k: (0, 0))],
        out_specs=(row_block, row_block) if keep_x else row_block,
        compiler_params=_params(("parallel", "arbitrary"), DOWN_PROJ_VMEM_LIMIT_BYTES),
        name="down_proj_norm",
    )(a, w_stack, resid, g.reshape(1, n))


def _ffn_up_kernel(h_ref, halo_ref, p_ref, phalo_ref, wg_ref, wv_ref, cw_ref, cb_ref, o_ref, hext_ref,
                   rstd_ref, *, blocks_per_seq):
    i, j = pl.program_id(0), pl.program_id(1)
    tm, d = h_ref.shape

    def inv_rms(p):
        r = lax.rsqrt(jnp.sum(p, axis=-1, keepdims=True) / d + NORM_EPS)
        return jnp.broadcast_to(r, (p.shape[0], LANE))

    @pl.when(j == 0)
    def _():
        halo = halo_ref[...]
        first = i % blocks_per_seq == 0
        hext_ref[:CONV_HALO, :] = jnp.where(first, jnp.zeros_like(halo), halo)
        hext_ref[CONV_HALO:, :] = h_ref[...]
        rstd_ref[:CONV_HALO, :] = inv_rms(phalo_ref[...])
        rstd_ref[CONV_HALO:, :] = inv_rms(p_ref[...])

    def scale_rows(m, r):
        return jnp.concatenate([m[:, c0:c0 + LANE] * r for c0 in range(0, m.shape[1], LANE)], axis=1)

    g = jnp.dot(hext_ref[...], wg_ref[...], preferred_element_type=F32)
    g = scale_rows(g, rstd_ref[...])
    val = scale_rows(jnp.dot(h_ref[...], wv_ref[...], preferred_element_type=F32), rstd_ref[CONV_HALO:, :])
    cw = cw_ref[...]
    gate = cb_ref[...]
    for tap in range(CONV_WIDTH):
        lo = CONV_HALO - (CONV_WIDTH - 1) + tap
        gate = gate + cw[tap:tap + 1, :] * g[lo:lo + tm, :]
    o_ref[...] = (gate * jax.nn.sigmoid(gate) * val).astype(o_ref.dtype)


def _ffn_up(h, ssq, wg_stack, wv_stack, conv_w, conv_b, layer, seq):
    t, d = h.shape
    n_p = ssq.shape[1]
    n = wg_stack.shape[-1]
    tm = min(FFN_TILE[0], seq)
    bn = min(FFN_TILE[1], n)
    per_seq = seq // tm
    halo_blocks = tm // CONV_HALO
    return pl.pallas_call(
        functools.partial(_ffn_up_kernel, blocks_per_seq=per_seq),
        out_shape=jax.ShapeDtypeStruct((t, n), MXU_DTYPE),
        grid=(t // tm, pl.cdiv(n, bn)),
        in_specs=[pl.BlockSpec((tm, d), lambda i, j: (i, 0)),
                  pl.BlockSpec((CONV_HALO, d), lambda i, j: (jnp.maximum(i * halo_blocks - 1, 0), 0)),
                  pl.BlockSpec((tm, n_p), lambda i, j: (i, 0)),
                  pl.BlockSpec((CONV_HALO, n_p), lambda i, j: (jnp.maximum(i * halo_blocks - 1, 0), 0)),
                  pl.BlockSpec((None, d, bn), lambda i, j: (layer, 0, j)),
                  pl.BlockSpec((None, d, bn), lambda i, j: (layer, 0, j)),
                  pl.BlockSpec((None, CONV_WIDTH, bn), lambda i, j: (layer, 0, j)),
                  pl.BlockSpec((None, 1, bn), lambda i, j: (layer, 0, j))],
        out_specs=pl.BlockSpec((tm, bn), lambda i, j: (i, j)),
        scratch_shapes=[pltpu.VMEM((tm + CONV_HALO, d), MXU_DTYPE), pltpu.VMEM((tm + CONV_HALO, LANE), F32)],
        compiler_params=_params(("parallel", "arbitrary")),
        name="ffn_up_conv",
    )(h, h, ssq, ssq, wg_stack, wv_stack, conv_w, conv_b)


def _pad_last(w, width):
    return jnp.pad(w, [(0, 0)] * (w.ndim - 1) + [(0, width - w.shape[-1])])


def _rotate_half_columns(w):
    half = ROPE_DIM // 2
    return jnp.concatenate([-w[..., half:], w[..., :half]], axis=-1)


def _pack_weights(norm_mix_g, w_in, mla_w_uq, mla_w_ukv, w_o, norm_ffn_g, w_gate, w_val, conv_b, w_down,
                  group_width, q_rank, kv_rank, n_heads):
    gw = group_width
    a_end = 3 * gw
    b_end = a_end + 2 * gw
    cq_end = b_end + q_rank
    ckv_end = cq_end + kv_rank
    kr_end = ckv_end + ROPE_DIM
    d_end = kr_end + 3 * gw
    layers, qr, _ = mla_w_uq.shape
    head_segments = ((0, b_end), (kr_end, 3 * gw))
    half = ROPE_DIM // 2
    w_in_t, w_latent = [], []
    for l in range(layers):
        wt = (jnp.swapaxes(w_in[l], 0, 1) * norm_mix_g[l][None, :]).astype(MXU_DTYPE)
        kr = wt[ckv_end:kr_end]
        forget = wt[d_end:]
        w_in_t.append(wt)
        w_latent.append(jnp.concatenate(
            [wt[b_end:ckv_end], kr, -kr[half:], kr[:half],
             jnp.pad(forget, ((0, LANE - forget.shape[0]), (0, 0)))], axis=0))

    uq = mla_w_uq.reshape(layers, qr, n_heads, HEAD_DIM + ROPE_DIM)
    nope, rope = uq[..., :HEAD_DIM], uq[..., HEAD_DIM:]
    flat = lambda w: w.reshape(layers, w.shape[1], n_heads * HEAD_DIM)
    wq = jnp.concatenate([flat(nope), flat(jnp.concatenate([rope, _rotate_half_columns(rope)], axis=-1))],
                         axis=-1).astype(MXU_DTYPE)
    ukv = mla_w_ukv.reshape(layers, kv_rank, n_heads, 2 * HEAD_DIM)
    wkv = jnp.concatenate([flat(ukv[..., :HEAD_DIM]), flat(ukv[..., HEAD_DIM:])], axis=-1).astype(MXU_DTYPE)

    ffn_gain = norm_ffn_g[:, :, None]
    return (w_in_t, head_segments, w_latent, wq, wkv, w_o.astype(MXU_DTYPE),
            (w_gate * ffn_gain).astype(MXU_DTYPE), (w_val * ffn_gain).astype(MXU_DTYPE),
            w_down.astype(MXU_DTYPE), conv_b[:, None, :])


def _rope_tables(seq):
    half = ROPE_DIM // 2
    inv_freq = ROPE_THETA ** (-jnp.arange(half, dtype=F32) / half)
    ang = jnp.arange(seq, dtype=F32)[:, None] * inv_freq[None, :]
    cos = jnp.concatenate([jnp.cos(ang), jnp.cos(ang)], axis=-1)
    sin = jnp.concatenate([jnp.sin(ang), jnp.sin(ang)], axis=-1)
    return _pad_last(cos, LANE), _pad_last(sin, LANE)


def kernel(x, norm_mix_g, w_in, sgu_ln_g, sgu_ln_b, sgu_w, sgu_b, mla_q_norm_g, mla_kv_norm_g,
           mla_w_uq, mla_w_ukv, fox_b_f, group_norm_g, w_o, norm_ffn_g, w_gate, w_val, conv_w,
           conv_b, w_down, final_norm_g):
    batch, seq, d_model = x.shape
    depth = w_in.shape[0]
    n_heads = fox_b_f.shape[-1]
    gw = group_norm_g.shape[-1]
    q_rank, kv_rank = mla_q_norm_g.shape[-1], mla_kv_norm_g.shape[-1]
    assert gw == n_heads * HEAD_DIM and seq % MOBA_BLOCK == 0
    assert q_rank % LANE == 0 and kv_rank % LANE == 0

    (w_in_t, head_segments, w_latent, wq, wkv, wo, wg, wv, wd, cb) = _pack_weights(
        norm_mix_g, w_in, mla_w_uq, mla_w_ukv, w_o, norm_ffn_g, w_gate, w_val, conv_b, w_down, gw, q_rank,
        kv_rank, n_heads)
    cos, sin = _rope_tables(seq)
    slopes, moba_k_aux, moba_q_aux = _moba_constants(seq, n_heads)
    f_col_block = (q_rank + kv_rank + LANE) // LANE
    mla_scale = (HEAD_DIM + ROPE_DIM) ** -0.5 * LOG2E
    a_q, a_k, a_v, b_u, b_v, d_q, d_k, d_v = (i * n_heads for i in range(8))
    q_scale, q_starts = HEAD_DIM ** -0.5 * LOG2E, (a_q * HEAD_DIM, d_q * HEAD_DIM)

    xf = x.reshape(batch * seq, d_model)
    no_gain = jnp.ones((d_model,), F32)
    h = _rmsnorm(xf, no_gain, MXU_DTYPE)
    for l in range(depth):
        proj = _matmul_groups(h, w_in_t[l], IN_PROJ_TILE[0], min(IN_PROJ_TILE[1], gw), head_segments,
                              q_starts, q_scale)
        latent = _matmul_f32(h, w_latent[l], LATENT_ROWS)
        qn, qr, kn, vv, kr = _mla_up(latent, mla_q_norm_g[l], mla_kv_norm_g[l], wq, wkv, cos, sin,
                                     l, n_heads, seq, mla_scale)
        fox_q_aux, fox_k_aux = _fox_gate(latent, fox_b_f[l], f_col_block, batch, seq, n_heads)
        y_a = _attention(proj, moba_q_aux, proj, moba_k_aux, proj, batch, seq, n_heads,
                         a_q, a_k, a_v, moba_slopes=slopes)
        y_b = _sgu(proj, sgu_ln_g[l], sgu_ln_b[l], sgu_w[l], sgu_b[l], n_heads, b_u, b_v)
        y_c = _attention(qn, qr, kn, kr, vv, batch, seq, n_heads, k2_shared=True)
        y_d = _attention(proj, fox_q_aux, proj, fox_k_aux, proj, batch, seq, n_heads, d_q, d_k, d_v)
        xf, xb, ssq = _out_proj((y_a, y_b, y_c, y_d), group_norm_g[l], wo, l, xf)
        act = _ffn_up(xb, ssq, wg, wv, conv_w, cb, l, seq)
        if l == depth - 1:
            h = _down_proj_norm(act, wd, l, xf, final_norm_g, F32, keep_x=False)
        else:
            xf, h = _down_proj_norm(act, wd, l, xf, no_gain, MXU_DTYPE)
    return h.reshape(batch, seq, d_model)
```

```python
import functools
import math

import jax
import jax.numpy as jnp
from jax import lax
from jax.experimental import pallas as pl
from jax.experimental.pallas import tpu as pltpu

F32 = jnp.float32
MXU_DTYPE = jnp.bfloat16

LANE = 128
HEAD_DIM = 128
MOBA_BLOCK = 256
MOBA_TOPK = 3
SGU_CHUNK = 128
ROPE_DIM = 64
ROPE_THETA = 10000.0
CONV_WIDTH = 3
NORM_EPS = 1e-6
NEG_INF = -1e30
LOG2E = math.log2(math.e)

NORM_ROWS = 512
IN_PROJ_TILE = (1024, 1024)
LATENT_ROWS = 512
MLA_UP_ROWS = 512
SGU_ROWS = 2048
OUT_PROJ_TILE = (1024, 512)
FFN_TILE = (1024, 512)
DOWN_PROJ_ROWS, DOWN_PROJ_DEPTH = 512, 1024
ATTN_BLOCK = 256
MOBA_MAX_BLOCKS = 8
ATTN_HEADS_PER_STEP = 4
CONV_HALO = 16
PROJ_COL_CHUNK = 512
NORM_ROW_CHUNK = 128
VMEM_LIMIT_BYTES = 56 * 1024 * 1024
DOWN_PROJ_VMEM_LIMIT_BYTES = 62 * 1024 * 1024


def _params(semantics, vmem_limit_bytes=VMEM_LIMIT_BYTES):
    return pltpu.CompilerParams(dimension_semantics=semantics, vmem_limit_bytes=vmem_limit_bytes)


def _rmsnorm_rows(x, g):
    ms = jnp.mean(x * x, axis=-1, keepdims=True)
    return x * lax.rsqrt(ms + NORM_EPS) * g


def _split3(x):
    hi = x.astype(MXU_DTYPE).astype(F32)
    r = x - hi
    mid = r.astype(MXU_DTYPE).astype(F32)
    lo = (r - mid).astype(MXU_DTYPE).astype(F32)
    return hi, mid, lo


def _norm_kernel(x_ref, g_ref, o_ref):
    o_ref[...] = _rmsnorm_rows(x_ref[...], g_ref[...]).astype(o_ref.dtype)


def _rmsnorm(x, g, out_dtype):
    t, d = x.shape
    tm = min(NORM_ROWS, t)
    return pl.pallas_call(
        _norm_kernel,
        out_shape=jax.ShapeDtypeStruct((t, d), out_dtype),
        grid=(t // tm,),
        in_specs=[pl.BlockSpec((tm, d), lambda i: (i, 0)),
                  pl.BlockSpec((1, d), lambda i: (0, 0))],
        out_specs=pl.BlockSpec((tm, d), lambda i: (i, 0)),
        compiler_params=_params(("parallel",)),
        name="rmsnorm",
    )(x, g.reshape(1, d))


def _dot_nt(a, w_t):
    return lax.dot_general(a, w_t, (((1,), (1,)), ((), ())), preferred_element_type=F32)


def _mm_groups_kernel(a_ref, w_ref, o_ref, *, scaled_blocks, scale):
    acc = _dot_nt(a_ref[...], w_ref[...])
    j = pl.program_id(1)
    is_scaled = functools.reduce(jnp.logical_or, [j == b for b in scaled_blocks])
    acc = acc * jnp.where(is_scaled, scale, 1.0)
    for g in range(o_ref.shape[0]):
        o_ref[g] = acc[:, g * LANE:(g + 1) * LANE].astype(o_ref.dtype)


def _matmul_groups(a, w_t, tm, bn, segments, scaled_starts, scale):
    t, k = a.shape
    tm = min(tm, t)
    assert all(rows % bn == 0 for _, rows in segments) and all(s0 % bn == 0 for s0 in scaled_starts)
    n = sum(rows for _, rows in segments)

    def row_start(j):
        r = j * bn
        shift, done = 0, 0
        for r0, rows in segments:
            r = r + jnp.where(j >= done // bn, (r0 - done) - shift, 0)
            shift, done = r0 - done, done + rows
        return pl.multiple_of(r, math.gcd(bn, *(r0 for r0, _ in segments)))

    return pl.pallas_call(
        functools.partial(_mm_groups_kernel, scaled_blocks=[s0 // bn for s0 in scaled_starts], scale=scale),
        out_shape=jax.ShapeDtypeStruct((n // LANE, t, LANE), MXU_DTYPE),
        grid=(t // tm, n // bn),
        in_specs=[pl.BlockSpec((tm, k), lambda i, j: (i, 0)),
                  pl.BlockSpec((pl.Element(bn), pl.Element(k)), lambda i, j: (row_start(j), 0))],
        out_specs=pl.BlockSpec((bn // LANE, tm, LANE), lambda i, j: (j, i, 0)),
        compiler_params=_params(("parallel", "arbitrary")),
        name="in_proj_heads",
    )(a, w_t)


def _mm_plain_kernel(a_ref, w_ref, o_ref):
    o_ref[...] = _dot_nt(a_ref[...], w_ref[...]).astype(o_ref.dtype)


def _matmul_f32(a, w_t, tm):
    t, k = a.shape
    n = w_t.shape[0]
    tm = min(tm, t)
    return pl.pallas_call(
        _mm_plain_kernel,
        out_shape=jax.ShapeDtypeStruct((t, n), F32),
        grid=(t // tm,),
        in_specs=[pl.BlockSpec((tm, k), lambda i: (i, 0)),
                  pl.BlockSpec((n, k), lambda i: (0, 0))],
        out_specs=pl.BlockSpec((tm, n), lambda i: (i, 0)),
        compiler_params=_params(("parallel",)),
        name="in_proj_latent",
    )(a, w_t)


def _mla_up_kernel(c_ref, gq_ref, gkv_ref, wq_ref, wkv_ref, cos_ref, sin_ref,
                   qn_ref, qr_ref, kn_ref, v_ref, kr_ref, *, q_rank, kv_rank, scale):
    n_heads = qn_ref.shape[0]
    hw = n_heads * HEAD_DIM
    c = c_ref[...]
    cq = _rmsnorm_rows(c[:, :q_rank], gq_ref[...]).astype(MXU_DTYPE)
    ckv = _rmsnorm_rows(c[:, q_rank:q_rank + kv_rank], gkv_ref[...]).astype(MXU_DTYPE)
    q = jnp.dot(cq, wq_ref[...], preferred_element_type=F32)
    kv = jnp.dot(ckv, wkv_ref[...], preferred_element_type=F32)
    cos, sin = cos_ref[...], sin_ref[...]

    def rope(pair):
        return pair * cos + pltpu.roll(pair, ROPE_DIM, axis=1) * sin

    for h in range(n_heads):
        lo, hi = h * HEAD_DIM, (h + 1) * HEAD_DIM
        qn_ref[h] = (q[:, lo:hi] * scale).astype(qn_ref.dtype)
        qr_ref[h] = (rope(q[:, hw + lo:hw + hi]) * scale).astype(qr_ref.dtype)
        kn_ref[h] = kv[:, lo:hi].astype(kn_ref.dtype)
        v_ref[h] = kv[:, hw + lo:hw + hi].astype(v_ref.dtype)
    o = q_rank + kv_rank
    kr_ref[...] = rope(c[:, o:o + LANE]).astype(kr_ref.dtype)


def _mla_up(c, gq, gkv, wq_stack, wkv_stack, cos, sin, layer, n_heads, seq, scale):
    t, cw = c.shape
    q_rank, kv_rank = gq.shape[-1], gkv.shape[-1]
    tm = min(MLA_UP_ROWS, seq)
    per_seq = seq // tm
    heads = jax.ShapeDtypeStruct((n_heads, t, LANE), MXU_DTYPE)
    head_spec = pl.BlockSpec((n_heads, tm, LANE), lambda i: (0, i, 0))
    return pl.pallas_call(
        functools.partial(_mla_up_kernel, q_rank=q_rank, kv_rank=kv_rank, scale=scale),
        out_shape=(heads, heads, heads, heads, jax.ShapeDtypeStruct((t, LANE), MXU_DTYPE)),
        grid=(t // tm,),
        in_specs=[pl.BlockSpec((tm, cw), lambda i: (i, 0)),
                  pl.BlockSpec((1, q_rank), lambda i: (0, 0)),
                  pl.BlockSpec((1, kv_rank), lambda i: (0, 0)),
                  pl.BlockSpec((None,) + wq_stack.shape[1:], lambda i: (layer, 0, 0)),
                  pl.BlockSpec((None,) + wkv_stack.shape[1:], lambda i: (layer, 0, 0)),
                  pl.BlockSpec((tm, LANE), lambda i: (i % per_seq, 0)),
                  pl.BlockSpec((tm, LANE), lambda i: (i % per_seq, 0))],
        out_specs=(head_spec, head_spec, head_spec, head_spec,
                   pl.BlockSpec((tm, LANE), lambda i: (i, 0))),
        compiler_params=_params(("parallel",)),
        name="mla_up",
    )(c, gq.reshape(1, q_rank), gkv.reshape(1, kv_rank), wq_stack, wkv_stack, cos, sin)


def _fox_gate_kernel(f_ref, b_ref, qa_ref, ka_ref):
    n_heads = qa_ref.shape[0]
    z = f_ref[...] + b_ref[...]
    x = -(jnp.maximum(-z, 0.0) + jnp.log1p(jnp.exp(-jnp.abs(z))))
    s = z.shape[0]
    rows = lax.broadcasted_iota(jnp.int32, z.shape, 0)
    shift = 1
    while shift < s:
        x = x + jnp.where(rows >= shift, pltpu.roll(x, shift, axis=0), 0.0)
        shift *= 2
    x = x * LOG2E
    lane = lax.broadcasted_iota(jnp.int32, z.shape, 1)
    ones = jnp.where(lane < 3, 1.0, 0.0)
    for h in range(n_heads):
        hi, mid, lo = _split3(jnp.broadcast_to(x[:, h:h + 1], x.shape))
        pieces = jnp.where(lane == 0, hi, jnp.where(lane == 1, mid, jnp.where(lane == 2, lo, 0.0)))
        qa_ref[h] = (pieces + pltpu.roll(ones, 3, axis=1)).astype(qa_ref.dtype)
        ka_ref[h] = (ones - pltpu.roll(pieces, 3, axis=1)).astype(ka_ref.dtype)


def _fox_gate(c, b_f, f_col_block, batch, seq, n_heads):
    b_pad = jnp.zeros((1, LANE), F32).at[0, :n_heads].set(b_f)
    aux = jax.ShapeDtypeStruct((n_heads, batch * seq, LANE), MXU_DTYPE)
    aux_spec = pl.BlockSpec((n_heads, seq, LANE), lambda b: (0, b, 0))
    return pl.pallas_call(
        _fox_gate_kernel,
        out_shape=(aux, aux),
        grid=(batch,),
        in_specs=[pl.BlockSpec((seq, LANE), lambda b: (b, f_col_block)),
                  pl.BlockSpec((1, LANE), lambda b: (0, 0))],
        out_specs=(aux_spec, aux_spec),
        compiler_params=_params(("parallel",)),
        name="fox_gate",
    )(c, b_pad)


def _moba_query_aux(q, kmean_ref, qconst, slope2, blk):
    seq = q.shape[0]
    rows = kmean_ref.shape[0]
    pieces = jnp.concatenate(_split3(kmean_ref[...]), axis=0).astype(q.dtype)
    g = lax.dot_general(pieces, q, (((1,), (1,)), ((), ())), preferred_element_type=F32)
    g = g[:rows] + g[rows:2 * rows] + g[2 * rows:]
    row = lax.broadcasted_iota(jnp.int32, g.shape, 0)
    bid = row & (MOBA_MAX_BLOCKS - 1)
    qblk = lax.broadcasted_iota(jnp.int32, g.shape, 1) // blk
    rank = jnp.zeros(g.shape, F32)
    for n in range(seq // blk - 1):
        gn = g[n:n + 1, :]
        beats = ((gn > g) | ((gn == g) & (n < bid))) & (n < qblk)
        rank = rank + jnp.where(beats, 1.0, 0.0)
    past = bid < qblk
    chosen = (rank < MOBA_TOPK) & past
    offset = (qblk - bid).astype(F32) * (-slope2 * blk)
    term = jnp.where(chosen, offset, jnp.where(past, NEG_INF, 0.0))
    hi = term.astype(MXU_DTYPE).astype(F32)
    lo = jnp.where(chosen, term - hi, 0.0)
    aux_t = jnp.where(row < MOBA_MAX_BLOCKS, hi, lo)
    aux = jnp.concatenate([aux_t, jnp.zeros((LANE - aux_t.shape[0], seq), F32)], axis=0).T
    return [(aux[r0:r0 + blk, :] + qconst).astype(MXU_DTYPE) for r0 in range(0, seq, blk)]


def _attention_kernel(*refs, moba, k2_shared):
    if moba:
        slopes_ref, q1_ref, k1_ref, v_ref, k2_ref, qc_ref, o_ref, kk_ref, vv_ref, kmean_ref = refs
    else:
        q1_ref, q2_ref, k1_ref, k2_ref, v_ref, o_ref, kk_ref, vv_ref = refs
    heads, seq, _ = k1_ref.shape
    tq = ATTN_BLOCK
    q2_tiles = []
    for hh in range(heads):
        kk_ref[hh, :, :HEAD_DIM] = k1_ref[hh]
        kk_ref[hh, :, HEAD_DIM:] = k2_ref[...] if k2_shared else k2_ref[hh]
        vv_ref[hh, :, :HEAD_DIM] = v_ref[hh]
        vv_ref[hh, :, HEAD_DIM:] = jnp.ones((seq, HEAD_DIM), vv_ref.dtype)
        if moba:
            slope2 = slopes_ref[pl.program_id(1) * heads + hh] * LOG2E
            kmean_ref[hh] = jnp.zeros(kmean_ref.shape[1:], F32)
            for n in range(seq // tq):
                mean = jnp.mean(k1_ref[hh, n * tq:(n + 1) * tq, :].astype(F32), axis=0, keepdims=True)
                kmean_ref[hh, n:n + 1, :] = mean
                kmean_ref[hh, MOBA_MAX_BLOCKS + n:MOBA_MAX_BLOCKS + n + 1, :] = mean
            q2_tiles.append(_moba_query_aux(q1_ref[hh], kmean_ref.at[hh], qc_ref[hh], slope2, tq))
    r = lax.broadcasted_iota(jnp.int32, (tq, tq), 0)
    c = lax.broadcasted_iota(jnp.int32, (tq, tq), 1)
    for qi in reversed(range(seq // tq)):
        rows = slice(qi * tq, (qi + 1) * tq)
        n = (qi + 1) * tq
        for hh in range(heads):
            q1 = q1_ref[hh, rows, :]
            q2 = q2_tiles[hh][qi] if moba else q2_ref[hh, rows, :]
            s = lax.dot_general(jnp.concatenate([q1, q2], axis=1), kk_ref[hh, :n, :],
                                (((1,), (1,)), ((), ())), preferred_element_type=F32)
            diag = jnp.where(c <= r, s[:, n - tq:], NEG_INF)
            s = diag if qi == 0 else jnp.concatenate([s[:, :n - tq], diag], axis=1)
            m = jnp.max(s, axis=-1, keepdims=True)
            p = jnp.exp2(s - m).astype(vv_ref.dtype)
            acc = jnp.dot(p, vv_ref[hh, :n, :], preferred_element_type=F32)
            o_ref[hh, rows, :] = (acc[:, :HEAD_DIM] / acc[:, HEAD_DIM:]).astype(o_ref.dtype)


def _attention(q1, q2, k1, k2, v, batch, seq, n_heads, q1_group=0, k1_group=0, v_group=0,
               k2_shared=False, moba_slopes=None):
    assert seq % ATTN_BLOCK == 0
    t = batch * seq
    moba = moba_slopes is not None
    hp = math.gcd(n_heads, ATTN_HEADS_PER_STEP)
    assert all(g % hp == 0 for g in (q1_group, k1_group, v_group))
    head = lambda g: pl.BlockSpec((hp, seq, HEAD_DIM), lambda b, h: (g // hp + h, b, 0))
    if moba:
        assert seq // ATTN_BLOCK <= MOBA_MAX_BLOCKS and ATTN_BLOCK == MOBA_BLOCK
        in_specs = [pl.BlockSpec(memory_space=pltpu.SMEM), head(q1_group), head(k1_group), head(v_group),
                    pl.BlockSpec((hp, seq, LANE), lambda b, h: (h, 0, 0)),
                    pl.BlockSpec((hp, ATTN_BLOCK, LANE), lambda b, h: (h, 0, 0))]
        args = (moba_slopes, q1, k1, v, k2, q2)
        scratch = [pltpu.VMEM((hp, 2 * MOBA_MAX_BLOCKS, HEAD_DIM), F32)]
    else:
        k2_spec = pl.BlockSpec((seq, LANE), lambda b, h: (b, 0)) if k2_shared else head(0)
        in_specs = [head(q1_group), head(0), head(k1_group), k2_spec, head(v_group)]
        args = (q1, q2, k1, k2, v)
        scratch = []
    return pl.pallas_call(
        functools.partial(_attention_kernel, moba=moba, k2_shared=k2_shared),
        out_shape=jax.ShapeDtypeStruct((n_heads, t, HEAD_DIM), MXU_DTYPE),
        grid=(batch, n_heads // hp),
        in_specs=in_specs,
        out_specs=head(0),
        scratch_shapes=[pltpu.VMEM((hp, seq, 2 * HEAD_DIM), MXU_DTYPE),
                        pltpu.VMEM((hp, seq, 2 * HEAD_DIM), MXU_DTYPE)] + scratch,
        compiler_params=_params(("parallel", "parallel")),
        name="moba_attention" if moba else "causal_attention",
    )(*args)


def _moba_constants(seq, n_heads):
    slopes = jnp.exp2(-8.0 * jnp.arange(1, n_heads + 1, dtype=F32) / n_heads)
    pieces = jnp.stack(_split3(slopes * LOG2E), axis=-1)
    pos = jnp.arange(seq)
    onehot = (pos[:, None] // MOBA_BLOCK == jnp.arange(MOBA_MAX_BLOCKS)[None, :]).astype(F32)
    within = jnp.broadcast_to((pos % MOBA_BLOCK).astype(F32)[None, :, None], (n_heads, seq, 3))
    k_aux = jnp.concatenate([jnp.broadcast_to(onehot, (n_heads, seq, MOBA_MAX_BLOCKS))] * 2
                            + [jnp.broadcast_to(-pieces[:, None, :], (n_heads, seq, 3)), within], axis=-1)
    q_aux = jnp.concatenate([jnp.zeros((n_heads, MOBA_BLOCK, 2 * MOBA_MAX_BLOCKS), F32),
                             within[:, :MOBA_BLOCK], jnp.broadcast_to(pieces[:, None, :], (n_heads, MOBA_BLOCK, 3))],
                            axis=-1)
    return slopes, _pad_last(k_aux, LANE).astype(MXU_DTYPE), _pad_last(q_aux, LANE)


def _sgu_kernel(u_ref, v_ref, lg_ref, lb_ref, w_ref, b_ref, o_ref):
    tm = u_ref.shape[0]
    nc = tm // SGU_CHUNK
    v = jax.nn.gelu(v_ref[...].astype(F32))
    vc = v - jnp.mean(v, axis=-1, keepdims=True)
    var = jnp.mean(vc * vc, axis=-1, keepdims=True)
    vn = (vc * lax.rsqrt(var + NORM_EPS) * lg_ref[...] + lb_ref[...]).astype(MXU_DTYPE)
    v_side = jnp.concatenate([vn[c * SGU_CHUNK:(c + 1) * SGU_CHUNK, :] for c in range(nc)], axis=1)
    r = lax.broadcasted_iota(jnp.int32, (SGU_CHUNK, SGU_CHUNK), 0)
    c_ix = lax.broadcasted_iota(jnp.int32, (SGU_CHUNK, SGU_CHUNK), 1)
    w = jnp.where(c_ix <= r, w_ref[...], 0.0).astype(MXU_DTYPE)
    mixed = jnp.dot(w, v_side, preferred_element_type=F32) + b_ref[...]
    u = jax.nn.gelu(u_ref[...].astype(F32))
    for c in range(nc):
        rows = slice(c * SGU_CHUNK, (c + 1) * SGU_CHUNK)
        o_ref[rows, :] = (u[rows, :] * mixed[:, c * HEAD_DIM:(c + 1) * HEAD_DIM]).astype(o_ref.dtype)


def _sgu(proj, ln_g, ln_b, w_s, b_s, n_heads, u_group, v_group):
    t = proj.shape[1]
    tm = min(SGU_ROWS, t)
    vec = pl.BlockSpec((None, 1, HEAD_DIM), lambda g, i: (g, 0, 0))
    return pl.pallas_call(
        _sgu_kernel,
        out_shape=jax.ShapeDtypeStruct((n_heads, t, HEAD_DIM), MXU_DTYPE),
        grid=(n_heads, t // tm),
        in_specs=[pl.BlockSpec((None, tm, HEAD_DIM), lambda g, i: (u_group + g, i, 0)),
                  pl.BlockSpec((None, tm, HEAD_DIM), lambda g, i: (v_group + g, i, 0)),
                  vec, vec,
                  pl.BlockSpec((None, SGU_CHUNK, SGU_CHUNK), lambda g, i: (g, 0, 0)),
                  pl.BlockSpec((None, SGU_CHUNK, 1), lambda g, i: (g, 0, 0))],
        out_specs=pl.BlockSpec((None, tm, HEAD_DIM), lambda g, i: (g, i, 0)),
        compiler_params=_params(("parallel", "parallel")),
        name="spatial_gating",
    )(proj, proj, ln_g[:, None, :], ln_b[:, None, :], w_s, b_s[:, :, None])


def _out_proj_kernel(*refs, n_mix):
    y_refs, (gn_ref, w_ref, resid_ref, o_ref, ob_ref, op_ref, a_ref) = refs[:n_mix], refs[n_mix:]
    per = y_refs[0].shape[0]
    width = per * LANE
    tm = a_ref.shape[0]

    @pl.when(pl.program_id(1) == 0)
    def _():
        for m, y_ref in enumerate(y_refs):
            for r0 in range(0, tm, NORM_ROW_CHUNK):
                rows = slice(r0, min(r0 + NORM_ROW_CHUNK, tm))
                parts = [y_ref[i, rows, :].astype(F32) for i in range(per)]
                ssq = jnp.sum(sum(p * p for p in parts), axis=-1, keepdims=True)
                rstd = lax.rsqrt(ssq / width + NORM_EPS)
                for i, p in enumerate(parts):
                    lo = m * width + i * LANE
                    a_ref[rows, lo:lo + LANE] = (p * rstd * gn_ref[m:m + 1, i * LANE:(i + 1) * LANE]).astype(a_ref.dtype)

    x_new = resid_ref[...] + jnp.dot(a_ref[...], w_ref[...], preferred_element_type=F32)
    o_ref[...] = x_new
    ob_ref[...] = x_new.astype(ob_ref.dtype)
    sq = x_new * x_new
    op_ref[...] = sum(sq[:, c0:c0 + LANE] for c0 in range(0, sq.shape[1], LANE))


def _out_proj(ys, group_gain, w_stack, layer, resid):
    t, n = resid.shape
    n_mix, width = group_gain.shape
    per = width // LANE
    tm, bn = min(OUT_PROJ_TILE[0], t), min(OUT_PROJ_TILE[1], n)
    y_spec = pl.BlockSpec((per, tm, LANE), lambda i, j: (0, i, 0))
    tile = pl.BlockSpec((tm, bn), lambda i, j: (i, j))
    return pl.pallas_call(
        functools.partial(_out_proj_kernel, n_mix=n_mix),
        out_shape=(jax.ShapeDtypeStruct((t, n), F32), jax.ShapeDtypeStruct((t, n), MXU_DTYPE),
                   jax.ShapeDtypeStruct((t, n // bn * LANE), F32)),
        grid=(t // tm, n // bn),
        in_specs=[y_spec] * n_mix + [pl.BlockSpec((n_mix, width), lambda i, j: (0, 0)),
                                     pl.BlockSpec((None, n_mix * width, bn), lambda i, j: (layer, 0, j)),
                                     tile],
        out_specs=(tile, tile, pl.BlockSpec((tm, LANE), lambda i, j: (i, j))),
        scratch_shapes=[pltpu.VMEM((tm, n_mix * width), MXU_DTYPE)],
        compiler_params=_params(("parallel", "arbitrary")),
        name="out_proj",
    )(*ys, group_gain, w_stack, resid)


def _down_proj_norm_kernel(*refs, keep_x, k_total):
    if not keep_x:
        refs = refs + (refs[-1],)
    a_ref, w_ref, resid_ref, g_ref, ox_ref, oh_ref = refs
    k = pl.program_id(1)
    last = pl.num_programs(1) - 1
    tm, n = ox_ref.shape
    bk = a_ref.shape[1]
    k_rem = k_total % bk

    @pl.when(k == 0)
    def _():
        ox_ref[...] = resid_ref[...]

    def accumulate(depth):
        for c0 in range(0, n, PROJ_COL_CHUNK):
            cols = slice(c0, min(c0 + PROJ_COL_CHUNK, n))
            ox_ref[:, cols] += jnp.dot(a_ref[:, :depth], w_ref[:depth, cols], preferred_element_type=F32)

    if k_rem:
        pl.when(k < last)(lambda: accumulate(bk))
        pl.when(k == last)(lambda: accumulate(k_rem))
    else:
        accumulate(bk)

    @pl.when(k == last)
    def _():
        for r0 in range(0, tm, NORM_ROW_CHUNK):
            rows = slice(r0, min(r0 + NORM_ROW_CHUNK, tm))
            oh_ref[rows, :] = _rmsnorm_rows(ox_ref[rows, :], g_ref[...]).astype(oh_ref.dtype)


def _down_proj_norm(a, w_stack, layer, resid, g, h_dtype, keep_x=True):
    t, n = resid.shape
    k_total = a.shape[1]
    tm = min(DOWN_PROJ_ROWS, t)
    bk = min(DOWN_PROJ_DEPTH, k_total)
    row_block = pl.BlockSpec((tm, n), lambda i, k: (i, 0))
    h_shape = jax.ShapeDtypeStruct((t, n), h_dtype)
    assert keep_x or h_dtype == F32
    out_shape = (jax.ShapeDtypeStruct((t, n), F32), h_shape) if keep_x else h_shape
    return pl.pallas_call(
        functools.partial(_down_proj_norm_kernel, keep_x=keep_x, k_total=k_total),
        out_shape=out_shape,
        grid=(t // tm, pl.cdiv(k_total, bk)),
        in_specs=[pl.BlockSpec((tm, bk), lambda i, k: (i, k)),
                  pl.BlockSpec((None, bk, n), lambda i, k: (layer, k, 0)),
                  row_block,
                  pl.BlockSpec((1, n), lambda i, k: (0, 0))],
        out_specs=(row_block, row_block) if keep_x else row_block,
        compiler_params=_params(("parallel", "arbitrary"), DOWN_PROJ_VMEM_LIMIT_BYTES),
        name="down_proj_norm",
    )(a, w_stack, resid, g.reshape(1, n))


def _ffn_up_kernel(h_ref, halo_ref, p_ref, phalo_ref, wg_ref, wv_ref, cw_ref, cb_ref, o_ref, hext_ref,
                   rstd_ref, *, blocks_per_seq):
    i, j = pl.program_id(0), pl.program_id(1)
    tm, d = h_ref.shape

    def inv_rms(p):
        r = lax.rsqrt(jnp.sum(p, axis=-1, keepdims=True) / d + NORM_EPS)
        return jnp.broadcast_to(r, (p.shape[0], LANE))

    @pl.when(j == 0)
    def _():
        halo = halo_ref[...]
        first = i % blocks_per_seq == 0
        hext_ref[:CONV_HALO, :] = jnp.where(first, jnp.zeros_like(halo), halo)
        hext_ref[CONV_HALO:, :] = h_ref[...]
        rstd_ref[:CONV_HALO, :] = inv_rms(phalo_ref[...])
        rstd_ref[CONV_HALO:, :] = inv_rms(p_ref[...])

    def scale_rows(m, r):
        return jnp.concatenate([m[:, c0:c0 + LANE] * r for c0 in range(0, m.shape[1], LANE)], axis=1)

    g = jnp.dot(hext_ref[...], wg_ref[...], preferred_element_type=F32)
    g = scale_rows(g, rstd_ref[...])
    val = scale_rows(jnp.dot(h_ref[...], wv_ref[...], preferred_element_type=F32), rstd_ref[CONV_HALO:, :])
    cw = cw_ref[...]
    gate = cb_ref[...]
    for tap in range(CONV_WIDTH):
        lo = CONV_HALO - (CONV_WIDTH - 1) + tap
        gate = gate + cw[tap:tap + 1, :] * g[lo:lo + tm, :]
    o_ref[...] = (gate * jax.nn.sigmoid(gate) * val).astype(o_ref.dtype)


def _ffn_up(h, ssq, wg_stack, wv_stack, conv_w, conv_b, layer, seq):
    t, d = h.shape
    n_p = ssq.shape[1]
    n = wg_stack.shape[-1]
    tm = min(FFN_TILE[0], seq)
    bn = min(FFN_TILE[1], n)
    per_seq = seq // tm
    halo_blocks = tm // CONV_HALO
    return pl.pallas_call(
        functools.partial(_ffn_up_kernel, blocks_per_seq=per_seq),
        out_shape=jax.ShapeDtypeStruct((t, n), MXU_DTYPE),
        grid=(t // tm, pl.cdiv(n, bn)),
        in_specs=[pl.BlockSpec((tm, d), lambda i, j: (i, 0)),
                  pl.BlockSpec((CONV_HALO, d), lambda i, j: (jnp.maximum(i * halo_blocks - 1, 0), 0)),
                  pl.BlockSpec((tm, n_p), lambda i, j: (i, 0)),
                  pl.BlockSpec((CONV_HALO, n_p), lambda i, j: (jnp.maximum(i * halo_blocks - 1, 0), 0)),
                  pl.BlockSpec((None, d, bn), lambda i, j: (layer, 0, j)),
                  pl.BlockSpec((None, d, bn), lambda i, j: (layer, 0, j)),
                  pl.BlockSpec((None, CONV_WIDTH, bn), lambda i, j: (layer, 0, j)),
                  pl.BlockSpec((None, 1, bn), lambda i, j: (layer, 0, j))],
        out_specs=pl.BlockSpec((tm, bn), lambda i, j: (i, j)),
        scratch_shapes=[pltpu.VMEM((tm + CONV_HALO, d), MXU_DTYPE), pltpu.VMEM((tm + CONV_HALO, LANE), F32)],
        compiler_params=_params(("parallel", "arbitrary")),
        name="ffn_up_conv",
    )(h, h, ssq, ssq, wg_stack, wv_stack, conv_w, conv_b)


def _pad_last(w, width):
    return jnp.pad(w, [(0, 0)] * (w.ndim - 1) + [(0, width - w.shape[-1])])


def _rotate_half_columns(w):
    half = ROPE_DIM // 2
    return jnp.concatenate([-w[..., half:], w[..., :half]], axis=-1)


def _pack_weights(norm_mix_g, w_in, mla_w_uq, mla_w_ukv, w_o, norm_ffn_g, w_gate, w_val, conv_b, w_down,
                  group_width, q_rank, kv_rank, n_heads):
    gw = group_width
    a_end = 3 * gw
    b_end = a_end + 2 * gw
    cq_end = b_end + q_rank
    ckv_end = cq_end + kv_rank
    kr_end = ckv_end + ROPE_DIM
    d_end = kr_end + 3 * gw
    layers, qr, _ = mla_w_uq.shape
    head_segments = ((0, b_end), (kr_end, 3 * gw))
    half = ROPE_DIM // 2
    w_in_t, w_latent = [], []
    for l in range(layers):
        wt = (jnp.swapaxes(w_in[l], 0, 1) * norm_mix_g[l][None, :]).astype(MXU_DTYPE)
        kr = wt[ckv_end:kr_end]
        forget = wt[d_end:]
        w_in_t.append(wt)
        w_latent.append(jnp.concatenate(
            [wt[b_end:ckv_end], kr, -kr[half:], kr[:half],
             jnp.pad(forget, ((0, LANE - forget.shape[0]), (0, 0)))], axis=0))

    uq = mla_w_uq.reshape(layers, qr, n_heads, HEAD_DIM + ROPE_DIM)
    nope, rope = uq[..., :HEAD_DIM], uq[..., HEAD_DIM:]
    flat = lambda w: w.reshape(layers, w.shape[1], n_heads * HEAD_DIM)
    wq = jnp.concatenate([flat(nope), flat(jnp.concatenate([rope, _rotate_half_columns(rope)], axis=-1))],
                         axis=-1).astype(MXU_DTYPE)
    ukv = mla_w_ukv.reshape(layers, kv_rank, n_heads, 2 * HEAD_DIM)
    wkv = jnp.concatenate([flat(ukv[..., :HEAD_DIM]), flat(ukv[..., HEAD_DIM:])], axis=-1).astype(MXU_DTYPE)

    ffn_gain = norm_ffn_g[:, :, None]
    return (w_in_t, head_segments, w_latent, wq, wkv, w_o.astype(MXU_DTYPE),
            (w_gate * ffn_gain).astype(MXU_DTYPE), (w_val * ffn_gain).astype(MXU_DTYPE),
            w_down.astype(MXU_DTYPE), conv_b[:, None, :])


def _rope_tables(seq):
    half = ROPE_DIM // 2
    inv_freq = ROPE_THETA ** (-jnp.arange(half, dtype=F32) / half)
    ang = jnp.arange(seq, dtype=F32)[:, None] * inv_freq[None, :]
    cos = jnp.concatenate([jnp.cos(ang), jnp.cos(ang)], axis=-1)
    sin = jnp.concatenate([jnp.sin(ang), jnp.sin(ang)], axis=-1)
    return _pad_last(cos, LANE), _pad_last(sin, LANE)


def kernel(x, norm_mix_g, w_in, sgu_ln_g, sgu_ln_b, sgu_w, sgu_b, mla_q_norm_g, mla_kv_norm_g,
           mla_w_uq, mla_w_ukv, fox_b_f, group_norm_g, w_o, norm_ffn_g, w_gate, w_val, conv_w,
           conv_b, w_down, final_norm_g):
    batch, seq, d_model = x.shape
    depth = w_in.shape[0]
    n_heads = fox_b_f.shape[-1]
    gw = group_norm_g.shape[-1]
    q_rank, kv_rank = mla_q_norm_g.shape[-1], mla_kv_norm_g.shape[-1]
    assert gw == n_heads * HEAD_DIM and seq % MOBA_BLOCK == 0
    assert q_rank % LANE == 0 and kv_rank % LANE == 0

    (w_in_t, head_segments, w_latent, wq, wkv, wo, wg, wv, wd, cb) = _pack_weights(
        norm_mix_g, w_in, mla_w_uq, mla_w_ukv, w_o, norm_ffn_g, w_gate, w_val, conv_b, w_down, gw, q_rank,
        kv_rank, n_heads)
    cos, sin = _rope_tables(seq)
    slopes, moba_k_aux, moba_q_aux = _moba_constants(seq, n_heads)
    f_col_block = (q_rank + kv_rank + LANE) // LANE
    mla_scale = (HEAD_DIM + ROPE_DIM) ** -0.5 * LOG2E
    a_q, a_k, a_v, b_u, b_v, d_q, d_k, d_v = (i * n_heads for i in range(8))
    q_scale, q_starts = HEAD_DIM ** -0.5 * LOG2E, (a_q * HEAD_DIM, d_q * HEAD_DIM)

    xf = x.reshape(batch * seq, d_model)
    no_gain = jnp.ones((d_model,), F32)
    h = _rmsnorm(xf, no_gain, MXU_DTYPE)
    for l in range(depth):
        proj = _matmul_groups(h, w_in_t[l], IN_PROJ_TILE[0], min(IN_PROJ_TILE[1], gw), head_segments,
                              q_starts, q_scale)
        latent = _matmul_f32(h, w_latent[l], LATENT_ROWS)
        qn, qr, kn, vv, kr = _mla_up(latent, mla_q_norm_g[l], mla_kv_norm_g[l], wq, wkv, cos, sin,
                                     l, n_heads, seq, mla_scale)
        fox_q_aux, fox_k_aux = _fox_gate(latent, fox_b_f[l], f_col_block, batch, seq, n_heads)
        y_a = _attention(proj, moba_q_aux, proj, moba_k_aux, proj, batch, seq, n_heads,
                         a_q, a_k, a_v, moba_slopes=slopes)
        y_b = _sgu(proj, sgu_ln_g[l], sgu_ln_b[l], sgu_w[l], sgu_b[l], n_heads, b_u, b_v)
        y_c = _attention(qn, qr, kn, kr, vv, batch, seq, n_heads, k2_shared=True)
        y_d = _attention(proj, fox_q_aux, proj, fox_k_aux, proj, batch, seq, n_heads, d_q, d_k, d_v)
        xf, xb, ssq = _out_proj((y_a, y_b, y_c, y_d), group_norm_g[l], wo, l, xf)
        act = _ffn_up(xb, ssq, wg, wv, conv_w, cb, l, seq)
        if l == depth - 1:
            h = _down_proj_norm(act, wd, l, xf, final_norm_g, F32, keep_x=False)
        else:
            xf, h = _down_proj_norm(act, wd, l, xf, no_gain, MXU_DTYPE)
    return h.reshape(batch, seq, d_model)
```
